```python
import jax, jax.numpy as jnp
from jax import lax
import numpy as np

D_MODEL = 4096
BATCH = 4
SEQ = 2048
DEPTH = 2

HEAD_DIM = 128
N_HEADS = D_MODEL // 256
N_KV_HEADS = N_HEADS // 4
GROUP = N_HEADS // N_KV_HEADS
ATTN_WIDTH = N_HEADS * HEAD_DIM
KV_WIDTH = N_KV_HEADS * HEAD_DIM
WINDOW = 128
BLOCK = 128
ROPE_THETA = 500000.0
ROT_DIM = HEAD_DIM // 4
POOL_WIDTH = D_MODEL // 2
POOL_WINDOWS = (2, 4, 8, 16)
N_POOL_GROUPS = len(POOL_WINDOWS)
POOL_GROUP_DIM = POOL_WIDTH // N_POOL_GROUPS
IN_SPLITS = (ATTN_WIDTH, ATTN_WIDTH + KV_WIDTH, ATTN_WIDTH + 2 * KV_WIDTH,
             ATTN_WIDTH + 2 * KV_WIDTH + POOL_WIDTH,
             ATTN_WIDTH + 2 * KV_WIDTH + POOL_WIDTH + D_MODEL)
IN_WIDTH = ATTN_WIDTH + 2 * KV_WIDTH + POOL_WIDTH + 2 * D_MODEL
D_FF = ((-((-8 * D_MODEL) // 3)) + 255) // 256 * 256
RMS_EPS = 1e-6

kernel_name = 'hybrid_window_gqa_multiscale_pool_gated_encoder'


def rms_norm(x, g):
    x32 = x.astype(jnp.float32)
    y = x32 * lax.rsqrt(jnp.mean(x32 * x32, axis=-1, keepdims=True) + RMS_EPS)
    return (y * g.astype(jnp.float32)).astype(x.dtype)


def rotary_tables(seq):
    pos = jnp.arange(seq, dtype=jnp.float32)
    inv_freq = 1.0 / jnp.power(jnp.float32(ROPE_THETA),
                               jnp.arange(0, ROT_DIM, 2, dtype=jnp.float32) / ROT_DIM)
    ang = pos[:, None] * inv_freq[None, :]
    return jnp.cos(ang), jnp.sin(ang)


def partial_rotary(t, cos, sin):
    half = ROT_DIM // 2
    c = cos[None, :, None, :].astype(t.dtype)
    s = sin[None, :, None, :].astype(t.dtype)
    t1 = t[..., :half]
    t2 = t[..., half:ROT_DIM]
    return jnp.concatenate([t1 * c - t2 * s, t2 * c + t1 * s, t[..., ROT_DIM:]], axis=-1)


def windowed_gqa_with_sink(q, k, v, sink):
    B, S = q.shape[0], q.shape[1]
    nb = S // BLOCK
    qb = q.reshape(B, nb, BLOCK, N_KV_HEADS, GROUP, HEAD_DIM)
    pad = ((0, 0), (BLOCK, BLOCK), (0, 0), (0, 0))
    kp = jnp.pad(k, pad).reshape(B, nb + 2, BLOCK, N_KV_HEADS, HEAD_DIM)
    vp = jnp.pad(v, pad).reshape(B, nb + 2, BLOCK, N_KV_HEADS, HEAD_DIM)
    kw = jnp.concatenate([kp[:, :-2], kp[:, 1:-1], kp[:, 2:]], axis=2)
    vw = jnp.concatenate([vp[:, :-2], vp[:, 1:-1], vp[:, 2:]], axis=2)
    s = jnp.einsum('bnqhgd,bnjhd->bnhgqj', qb, kw).astype(jnp.float32) * (HEAD_DIM ** -0.5)
    blk = jnp.arange(nb)[:, None] * BLOCK
    qpos = blk + jnp.arange(BLOCK)[None, :]
    kpos = blk - BLOCK + jnp.arange(3 * BLOCK)[None, :]
    valid = ((kpos[:, None, :] >= 0) & (kpos[:, None, :] < S)
             & (jnp.abs(qpos[:, :, None] - kpos[:, None, :]) <= WINDOW))
    s = jnp.where(valid[None, :, None, None], s, -jnp.inf)
    sk = sink.astype(jnp.float32).reshape(1, 1, N_KV_HEADS, GROUP, 1, 1)
    m = jnp.maximum(jnp.max(s, axis=-1, keepdims=True), sk)
    p = jnp.exp(s - m)
    denom = jnp.sum(p, axis=-1, keepdims=True) + jnp.exp(sk - m)
    p = (p / denom).astype(v.dtype)
    o = jnp.einsum('bnhgqj,bnjhd->bnqhgd', p, vw)
    return o.reshape(B, S, ATTN_WIDTH)


def multiscale_pool(u, pool_w, pool_scale):
    B, S = u.shape[0], u.shape[1]
    ug = u.astype(jnp.float32).reshape(B, S, N_POOL_GROUPS, POOL_GROUP_DIM)
    t = jnp.arange(S)
    outs = []
    for gi, w in enumerate(POOL_WINDOWS):
        x_g = ug[:, :, gi]
        csum = jnp.concatenate([jnp.zeros((B, 1, POOL_GROUP_DIM), jnp.float32),
                                lax.cumsum(x_g, axis=1)], axis=1)
        lo = jnp.clip(t - w // 2, 0, S - 1)
        hi = jnp.clip(t + w // 2 - 1, 0, S - 1)
        cnt = (hi - lo + 1).astype(jnp.float32)[None, :, None]
        mean = (csum[:, hi + 1] - csum[:, lo]) / cnt
        outs.append(mean - x_g)
    pooled = jnp.stack(outs, axis=2)
    mixed = jnp.einsum('bsgc,gcd->bsgd', pooled, pool_w.astype(jnp.float32))
    mixed = mixed.reshape(B, S, POOL_WIDTH) * pool_scale.astype(jnp.float32)
    return mixed.astype(u.dtype)


def setup_inputs(seed: int = 0) -> dict:
    key = jax.random.key(seed)
    ks = jax.random.split(key, 14)
    f32 = jnp.float32

    def w(k, shape, fan_in):
        return jax.random.normal(k, shape, f32) * (fan_in ** -0.5)

    def gain(k, shape):
        return 1.0 + 0.02 * jax.random.normal(k, shape, f32)

    return {
        'x': jax.random.normal(ks[0], (BATCH, SEQ, D_MODEL), f32),
        'norm1_g': gain(ks[1], (DEPTH, D_MODEL)),
        'w_in': w(ks[2], (DEPTH, D_MODEL, IN_WIDTH), D_MODEL),
        'attn_sink': 0.5 * jax.random.normal(ks[3], (DEPTH, N_HEADS), f32),
        'pool_w': w(ks[4], (DEPTH, N_POOL_GROUPS, POOL_GROUP_DIM, POOL_GROUP_DIM), POOL_GROUP_DIM),
        'pool_scale': gain(ks[5], (DEPTH, POOL_WIDTH)),
        'w_branch_attn': w(ks[6], (DEPTH, ATTN_WIDTH, D_MODEL), ATTN_WIDTH),
        'w_branch_pool': w(ks[7], (DEPTH, POOL_WIDTH, D_MODEL), POOL_WIDTH),
        'w_out': w(ks[8], (DEPTH, D_MODEL, D_MODEL), D_MODEL),
        'norm2_g': gain(ks[9], (DEPTH, D_MODEL)),
        'w_gate_up': w(ks[10], (DEPTH, D_MODEL, 2 * D_FF), D_MODEL),
        'w_down': w(ks[11], (DEPTH, D_FF, D_MODEL), D_FF),
        'final_norm_g': gain(ks[12], (D_MODEL,)),
    }


def reference(x, norm1_g, w_in, attn_sink, pool_w, pool_scale, w_branch_attn,
              w_branch_pool, w_out, norm2_g, w_gate_up, w_down, final_norm_g):
    B, S = x.shape[0], x.shape[1]
    cos, sin = rotary_tables(S)
    for l in range(DEPTH):
        h = rms_norm(x, norm1_g[l])
        proj = h @ w_in[l]
        q, k, v, u, ga, gb = jnp.split(proj, IN_SPLITS, axis=-1)
        q = partial_rotary(q.reshape(B, S, N_HEADS, HEAD_DIM), cos, sin)
        k = partial_rotary(k.reshape(B, S, N_KV_HEADS, HEAD_DIM), cos, sin)
        v = v.reshape(B, S, N_KV_HEADS, HEAD_DIM)
        y_attn = windowed_gqa_with_sink(q, k, v, attn_sink[l]) @ w_branch_attn[l]
        y_pool = multiscale_pool(u, pool_w[l], pool_scale[l]) @ w_branch_pool[l]
        merged = jax.nn.sigmoid(ga) * y_attn + jax.nn.sigmoid(gb) * y_pool
        x = x + merged @ w_out[l]
        h2 = rms_norm(x, norm2_g[l])
        gate, up = jnp.split(h2 @ w_gate_up[l], 2, axis=-1)
        x = x + (jax.nn.silu(gate) * up) @ w_down[l]
    return rms_norm(x, final_norm_g)
```

```python
import functools

import jax
import jax.numpy as jnp
from jax import lax
from jax.experimental import pallas as pl
from jax.experimental.pallas import tpu as pltpu

D_MODEL = 4096
DEPTH = 2
HEAD_DIM = 128
N_HEADS = 16
N_KV_HEADS = 4
GROUP = N_HEADS // N_KV_HEADS
ATTN_WIDTH = N_HEADS * HEAD_DIM
KV_WIDTH = N_KV_HEADS * HEAD_DIM
BLOCK = 128
ROPE_THETA = 500000.0
ROT_DIM = HEAD_DIM // 4
POOL_WIDTH = D_MODEL // 2
POOL_WINDOWS = (2, 4, 8, 16)
N_POOL_GROUPS = len(POOL_WINDOWS)
POOL_GROUP_DIM = POOL_WIDTH // N_POOL_GROUPS
Q_OFF = 0
K_OFF = ATTN_WIDTH
V_OFF = K_OFF + KV_WIDTH
U_OFF = V_OFF + KV_WIDTH
GA_OFF = U_OFF + POOL_WIDTH
GB_OFF = GA_OFF + D_MODEL
IN_WIDTH = GB_OFF + D_MODEL
D_FF = 11008
RMS_EPS = 1e-6

POOL_HALO = 8
POOL_CHUNK = 128
MIB = 1024 * 1024


def _params(semantics, vmem_mib):
    return pltpu.CompilerParams(dimension_semantics=semantics,
                                vmem_limit_bytes=vmem_mib * MIB)


def _rmsnorm_kernel(x_ref, g_ref, o_ref):
    x = x_ref[...]
    ms = jnp.mean(x * x, axis=-1, keepdims=True)
    y = x * lax.rsqrt(ms + RMS_EPS)
    o_ref[...] = (y * g_ref[...]).astype(o_ref.dtype)


def _rmsnorm(x2d, g_row, out_dtype, tr=256):
    m, d = x2d.shape
    return pl.pallas_call(
        _rmsnorm_kernel,
        grid=(m // tr,),
        in_specs=[pl.BlockSpec((tr, d), lambda i: (i, 0)),
                  pl.BlockSpec((1, d), lambda i: (0, 0))],
        out_specs=pl.BlockSpec((tr, d), lambda i: (i, 0)),
        out_shape=jax.ShapeDtypeStruct((m, d), out_dtype),
        compiler_params=_params(("parallel",), 40),
        name="rmsnorm",
    )(x2d, g_row)


def _mm_kernel(x_ref, w_ref, o_ref):
    o_ref[...] = jnp.dot(x_ref[...], w_ref[...],
                         preferred_element_type=jnp.float32).astype(o_ref.dtype)


def _mm_res_kernel(x_ref, w_ref, r_ref, o_ref):
    acc = jnp.dot(x_ref[...], w_ref[...], preferred_element_type=jnp.float32)
    o_ref[...] = r_ref[...] + acc


def _matmul(x, w_stack, layer, out_dtype, tm, tn, vmem_mib, name, residual=None):
    m, k = x.shape
    n = w_stack.shape[-1]
    in_specs = [pl.BlockSpec((tm, k), lambda i, j: (i, 0)),
                pl.BlockSpec((None, k, tn), lambda i, j: (layer, 0, j))]
    args = [x, w_stack]
    kern = _mm_kernel
    if residual is not None:
        in_specs.append(pl.BlockSpec((tm, tn), lambda i, j: (i, j)))
        args.append(residual)
        kern = _mm_res_kernel
    return pl.pallas_call(
        kern,
        grid=(m // tm, n // tn),
        in_specs=in_specs,
        out_specs=pl.BlockSpec((tm, tn), lambda i, j: (i, j)),
        out_shape=jax.ShapeDtypeStruct((m, n), out_dtype),
        compiler_params=_params(("parallel", "arbitrary"), vmem_mib),
        name=name,
    )(*args)


def _rotary_mix(t, lane):
    half = ROT_DIM // 2
    return jnp.where(lane < half, pltpu.roll(t, HEAD_DIM - half, 1), pltpu.roll(t, half, 1))


def _attn_kernel(sink_ref, q_ref, k_ref, v_ref, cq_ref, sq_ref, ck_ref, sk_ref,
                 o_ref, kpad_ref, vpad_ref, *, seq):
    g = pl.program_id(1)
    nb = seq // BLOCK
    zeros = jnp.zeros((BLOCK, HEAD_DIM), jnp.bfloat16)
    kpad_ref[0:BLOCK, :] = zeros
    kpad_ref[BLOCK + seq:, :] = zeros
    vpad_ref[0:BLOCK, :] = zeros
    vpad_ref[BLOCK + seq:, :] = zeros
    lane_full = lax.broadcasted_iota(jnp.int32, (seq, HEAD_DIM), 1)
    k = k_ref[...]
    k_rot = k * ck_ref[...] + _rotary_mix(k, lane_full) * sk_ref[...]
    kpad_ref[BLOCK:BLOCK + seq, :] = k_rot.astype(jnp.bfloat16)
    vpad_ref[BLOCK:BLOCK + seq, :] = v_ref[...].astype(jnp.bfloat16)

    rows = GROUP * BLOCK
    row_head = lax.broadcasted_iota(jnp.int32, (rows, 1), 0) // BLOCK
    sink = jnp.zeros((rows, 1), jnp.float32)
    for h in range(GROUP):
        sink = jnp.where(row_head == h, sink_ref[g * GROUP + h], sink)

    def body(n, carry):
        r0 = pl.multiple_of(n * BLOCK, BLOCK)
        qi = lax.broadcasted_iota(jnp.int32, (rows, 3 * BLOCK), 0) % BLOCK
        kj = lax.broadcasted_iota(jnp.int32, (rows, 3 * BLOCK), 1)
        band = (kj >= qi) & (kj <= qi + 2 * BLOCK)
        lane_blk = lax.broadcasted_iota(jnp.int32, (BLOCK, HEAD_DIM), 1)
        cq = cq_ref[pl.ds(r0, BLOCK), :]
        sq = sq_ref[pl.ds(r0, BLOCK), :]
        heads = []
        for h in range(GROUP):
            qh = q_ref[pl.ds(r0, BLOCK), h * HEAD_DIM:(h + 1) * HEAD_DIM]
            heads.append((qh * cq + _rotary_mix(qh, lane_blk) * sq).astype(jnp.bfloat16))
        q = jnp.concatenate(heads, axis=0)
        kw = kpad_ref[pl.ds(r0, 3 * BLOCK), :]
        vw = vpad_ref[pl.ds(r0, 3 * BLOCK), :]
        s = lax.dot_general(q, kw, (((1,), (1,)), ((), ())),
                            preferred_element_type=jnp.float32)
        j_lo = jnp.where(n == 0, BLOCK, 0)
        j_hi = jnp.where(n == nb - 1, 2 * BLOCK, 3 * BLOCK)
        valid = band & (kj >= j_lo) & (kj < j_hi)
        s = jnp.where(valid, s, -jnp.inf)
        m = jnp.maximum(jnp.max(s, axis=-1, keepdims=True), sink)
        p = jnp.exp(s - m)
        denom = jnp.sum(p, axis=-1, keepdims=True) + jnp.exp(sink - m)
        o = jnp.dot(p.astype(jnp.bfloat16), vw, preferred_element_type=jnp.float32)
        o = o / denom
        for h in range(GROUP):
            o_ref[pl.ds(r0, BLOCK), h * HEAD_DIM:(h + 1) * HEAD_DIM] = (
                o[h * BLOCK:(h + 1) * BLOCK, :].astype(o_ref.dtype))
        return carry

    lax.fori_loop(0, nb, body, 0)


def _attention(proj, sink_l, tables, batch, seq):
    cq, sq, ck, sk = tables
    m = proj.shape[0]
    qw = GROUP * HEAD_DIM
    tab_spec = pl.BlockSpec((seq, HEAD_DIM), lambda b, g: (0, 0))
    return pl.pallas_call(
        functools.partial(_attn_kernel, seq=seq),
        grid=(batch, N_KV_HEADS),
        in_specs=[pl.BlockSpec(memory_space=pltpu.SMEM),
                  pl.BlockSpec((seq, qw), lambda b, g: (b, Q_OFF // qw + g)),
                  pl.BlockSpec((seq, HEAD_DIM), lambda b, g: (b, K_OFF // HEAD_DIM + g)),
                  pl.BlockSpec((seq, HEAD_DIM), lambda b, g: (b, V_OFF // HEAD_DIM + g)),
                  tab_spec, tab_spec, tab_spec, tab_spec],
        out_specs=pl.BlockSpec((seq, qw), lambda b, g: (b, g)),
        out_shape=jax.ShapeDtypeStruct((m, ATTN_WIDTH), jnp.bfloat16),
        scratch_shapes=[pltpu.VMEM((seq + 2 * BLOCK, HEAD_DIM), jnp.bfloat16),
                        pltpu.VMEM((seq + 2 * BLOCK, HEAD_DIM), jnp.bfloat16)],
        compiler_params=_params(("parallel", "arbitrary"), 40),
        name="window_gqa",
    )(sink_l, proj, proj, proj, cq, sq, ck, sk)


def _pool_kernel(u_ref, w_ref, scale_ref, o_ref, pad_ref, *, seq):
    g = pl.program_id(1)
    zeros = jnp.zeros((POOL_HALO, POOL_GROUP_DIM), jnp.float32)
    pad_ref[0:POOL_HALO, :] = zeros
    pad_ref[POOL_HALO + seq:, :] = zeros
    pad_ref[POOL_HALO:POOL_HALO + seq, :] = u_ref[...]
    wmat = w_ref[...].astype(jnp.bfloat16)
    scale = scale_ref[...]

    def run(win):
        half = win // 2
        ext = POOL_CHUNK + 2 * POOL_HALO

        def body(c, carry):
            t0 = pl.multiple_of(c * POOL_CHUNK, POOL_CHUNK)
            xe = pad_ref[pl.ds(t0, ext), :]
            a = xe
            step = 1
            while step < win:
                a = a + pltpu.roll(a, step, 0)
                step *= 2
            if half > 1:
                a = pltpu.roll(a, ext - (half - 1), 0)
            wsum = a[POOL_HALO:POOL_HALO + POOL_CHUNK, :]
            x = xe[POOL_HALO:POOL_HALO + POOL_CHUNK, :]
            t = t0 + lax.broadcasted_iota(jnp.int32, (POOL_CHUNK, 1), 0)
            lo = jnp.maximum(t - half, 0)
            hi = jnp.minimum(t + half - 1, seq - 1)
            cnt = (hi - lo + 1).astype(jnp.float32)
            pooled = wsum / cnt - x
            mixed = jnp.dot(pooled.astype(jnp.bfloat16), wmat,
                            preferred_element_type=jnp.float32)
            o_ref[pl.ds(t0, POOL_CHUNK), :] = (mixed * scale).astype(o_ref.dtype)
            return carry

        lax.fori_loop(0, seq // POOL_CHUNK, body, 0)

    for gi, win in enumerate(POOL_WINDOWS):
        pl.when(g == gi)(functools.partial(run, win))


def _pool(proj, pool_w, pool_scale3, layer, batch, seq):
    m = proj.shape[0]
    c = POOL_GROUP_DIM
    return pl.pallas_call(
        functools.partial(_pool_kernel, seq=seq),
        grid=(batch, N_POOL_GROUPS),
        in_specs=[pl.BlockSpec((seq, c), lambda b, g: (b, U_OFF // c + g)),
                  pl.BlockSpec((None, None, c, c), lambda b, g: (layer, g, 0, 0)),
                  pl.BlockSpec((None, 1, c), lambda b, g: (layer, 0, g))],
        out_specs=pl.BlockSpec((seq, c), lambda b, g: (b, g)),
        out_shape=jax.ShapeDtypeStruct((m, POOL_WIDTH), jnp.bfloat16),
        scratch_shapes=[pltpu.VMEM((seq + 2 * POOL_HALO, c), jnp.float32)],
        compiler_params=_params(("parallel", "arbitrary"), 40),
        name="multiscale_pool",
    )(proj, pool_w, pool_scale3)


def _merge_kernel(a_ref, p_ref, wa_ref, wp_ref, ga_ref, gb_ref, o_ref):
    ya = jnp.dot(a_ref[...], wa_ref[...], preferred_element_type=jnp.float32)
    yp = jnp.dot(p_ref[...], wp_ref[...], preferred_element_type=jnp.float32)
    merged = jax.nn.sigmoid(ga_ref[...]) * ya + jax.nn.sigmoid(gb_ref[...]) * yp
    o_ref[...] = merged.astype(o_ref.dtype)


def _merge(o_attn, mixed, wba, wbp, proj, layer, tm=1024, tn=512):
    m = o_attn.shape[0]
    return pl.pallas_call(
        _merge_kernel,
        grid=(m // tm, D_MODEL // tn),
        in_specs=[pl.BlockSpec((tm, ATTN_WIDTH), lambda i, j: (i, 0)),
                  pl.BlockSpec((tm, POOL_WIDTH), lambda i, j: (i, 0)),
                  pl.BlockSpec((None, ATTN_WIDTH, tn), lambda i, j: (layer, 0, j)),
                  pl.BlockSpec((None, POOL_WIDTH, tn), lambda i, j: (layer, 0, j)),
                  pl.BlockSpec((tm, tn), lambda i, j: (i, GA_OFF // tn + j)),
                  pl.BlockSpec((tm, tn), lambda i, j: (i, GB_OFF // tn + j))],
        out_specs=pl.BlockSpec((tm, tn), lambda i, j: (i, j)),
        out_shape=jax.ShapeDtypeStruct((m, D_MODEL), jnp.bfloat16),
        compiler_params=_params(("parallel", "arbitrary"), 48),
        name="branch_merge",
    )(o_attn, mixed, wba, wbp, proj, proj)


def _gate_up_kernel(x_ref, wg_ref, wu_ref, o_ref):
    x = x_ref[...]
    gate = jnp.dot(x, wg_ref[...], preferred_element_type=jnp.float32)
    up = jnp.dot(x, wu_ref[...], preferred_element_type=jnp.float32)
    o_ref[...] = (jax.nn.silu(gate) * up).astype(o_ref.dtype)


def _gate_up(h2, w_gate_up, layer, tm=1024, tn=256):
    m, k = h2.shape
    nt = D_FF // tn
    return pl.pallas_call(
        _gate_up_kernel,
        grid=(m // tm, nt),
        in_specs=[pl.BlockSpec((tm, k), lambda i, j: (i, 0)),
                  pl.BlockSpec((None, k, tn), lambda i, j: (layer, 0, j)),
                  pl.BlockSpec((None, k, tn), lambda i, j: (layer, 0, nt + j))],
        out_specs=pl.BlockSpec((tm, tn), lambda i, j: (i, j)),
        out_shape=jax.ShapeDtypeStruct((m, D_FF), jnp.bfloat16),
        compiler_params=_params(("parallel", "arbitrary"), 48),
        name="swiglu_gate_up",
    )(h2, w_gate_up, w_gate_up)


def _rotary_tables(seq):
    pos = jnp.arange(seq, dtype=jnp.float32)
    inv_freq = 1.0 / jnp.power(jnp.float32(ROPE_THETA),
                               jnp.arange(0, ROT_DIM, 2, dtype=jnp.float32) / ROT_DIM)
    ang = pos[:, None] * inv_freq[None, :]
    cos, sin = jnp.cos(ang), jnp.sin(ang)
    half = ROT_DIM // 2
    ones = jnp.ones((seq, HEAD_DIM - ROT_DIM), jnp.float32)
    zeros = jnp.zeros((seq, HEAD_DIM - ROT_DIM), jnp.float32)
    c_full = jnp.concatenate([cos, cos, ones], axis=-1)
    s_full = jnp.concatenate([-sin, sin, zeros], axis=-1)
    assert half * 2 == ROT_DIM
    scale = HEAD_DIM ** -0.5
    return c_full * scale, s_full * scale, c_full, s_full


def kernel(x, norm1_g, w_in, attn_sink, pool_w, pool_scale, w_branch_attn, w_branch_pool,
           w_out, norm2_g, w_gate_up, w_down, final_norm_g):
    batch, seq, d = x.shape
    m = batch * seq
    bf16 = jnp.bfloat16
    tables = _rotary_tables(seq)
    w_in_b = w_in.astype(bf16)
    wba_b = w_branch_attn.astype(bf16)
    wbp_b = w_branch_pool.astype(bf16)
    w_out_b = w_out.astype(bf16)
    w_gu_b = w_gate_up.astype(bf16)
    w_down_b = w_down.astype(bf16)
    pool_scale3 = pool_scale.reshape(DEPTH, 1, POOL_WIDTH)

    xs = x.reshape(m, d)
    for l in range(DEPTH):
        h = _rmsnorm(xs, norm1_g[l].reshape(1, d), bf16)
        proj = _matmul(h, w_in_b, l, jnp.float32, 1024, 512, 48, "in_proj")
        o_attn = _attention(proj, attn_sink[l], tables, batch, seq)
        mixed = _pool(proj, pool_w, pool_scale3, l, batch, seq)
        merged = _merge(o_attn, mixed, wba_b, wbp_b, proj, l)
        xs = _matmul(merged, w_out_b, l, jnp.float32, 1024, 512, 48, "out_proj", residual=xs)
        h2 = _rmsnorm(xs, norm2_g[l].reshape(1, d), bf16)
        act = _gate_up(h2, w_gu_b, l)
        xs = _matmul(act, w_down_b, l, jnp.float32, 512, 512, 56, "ffn_down", residual=xs)
    out = _rmsnorm(xs, final_norm_g.reshape(1, d), jnp.float32)
    return out.reshape(batch, seq, d)
```

```python
import functools

import jax
import jax.numpy as jnp
from jax import lax
from jax.experimental import pallas as pl
from jax.experimental.pallas import tpu as pltpu

D_MODEL = 4096
DEPTH = 2
HEAD_DIM = 128
N_HEADS = 16
N_KV_HEADS = 4
GROUP = N_HEADS // N_KV_HEADS
ATTN_WIDTH = N_HEADS * HEAD_DIM
KV_WIDTH = N_KV_HEADS * HEAD_DIM
BLOCK = 128
ROPE_THETA = 500000.0
ROT_DIM = HEAD_DIM // 4
POOL_WIDTH = D_MODEL // 2
POOL_WINDOWS = (2, 4, 8, 16)
N_POOL_GROUPS = len(POOL_WINDOWS)
POOL_GROUP_DIM = POOL_WIDTH // N_POOL_GROUPS
Q_OFF = 0
K_OFF = ATTN_WIDTH
V_OFF = K_OFF + KV_WIDTH
U_OFF = V_OFF + KV_WIDTH
GA_OFF = U_OFF + POOL_WIDTH
GB_OFF = GA_OFF + D_MODEL
IN_WIDTH = GB_OFF + D_MODEL
D_FF = 11008
RMS_EPS = 1e-6

POOL_HALO = 8
POOL_CHUNK = 128
MIB = 1024 * 1024


def _params(semantics, vmem_mib):
    return pltpu.CompilerParams(dimension_semantics=semantics,
                                vmem_limit_bytes=vmem_mib * MIB)


def _rmsnorm_kernel(x_ref, g_ref, o_ref):
    x = x_ref[...]
    ms = jnp.mean(x * x, axis=-1, keepdims=True)
    y = x * lax.rsqrt(ms + RMS_EPS)
    o_ref[...] = (y * g_ref[...]).astype(o_ref.dtype)


def _rmsnorm(x2d, g_row, out_dtype, tr=256):
    m, d = x2d.shape
    return pl.pallas_call(
        _rmsnorm_kernel,
        grid=(m // tr,),
        in_specs=[pl.BlockSpec((tr, d), lambda i: (i, 0)),
                  pl.BlockSpec((1, d), lambda i: (0, 0))],
        out_specs=pl.BlockSpec((tr, d), lambda i: (i, 0)),
        out_shape=jax.ShapeDtypeStruct((m, d), out_dtype),
        compiler_params=_params(("parallel",), 40),
        name="rmsnorm",
    )(x2d, g_row)


def _mm_kernel(x_ref, w_ref, o_ref):
    o_ref[...] = jnp.dot(x_ref[...], w_ref[...].astype(jnp.bfloat16),
                         preferred_element_type=jnp.float32).astype(o_ref.dtype)


def _mm_res_kernel(x_ref, w_ref, r_ref, o_ref):
    acc = jnp.dot(x_ref[...], w_ref[...].astype(jnp.bfloat16),
                  preferred_element_type=jnp.float32)
    o_ref[...] = r_ref[...] + acc


def _panel_spec(tm, k):
    return pl.BlockSpec((tm, k), lambda i, j: (i, 0), pipeline_mode=pl.Buffered(1))


def _matmul(x, w_stack, layer, out_dtype, tm, tn, vmem_mib, name, residual=None):
    m, k = x.shape
    n = w_stack.shape[-1]
    in_specs = [_panel_spec(tm, k),
                pl.BlockSpec((None, k, tn), lambda i, j: (layer, 0, j))]
    args = [x, w_stack]
    kern = _mm_kernel
    if residual is not None:
        in_specs.append(pl.BlockSpec((tm, tn), lambda i, j: (i, j)))
        args.append(residual)
        kern = _mm_res_kernel
    return pl.pallas_call(
        kern,
        grid=(m // tm, n // tn),
        in_specs=in_specs,
        out_specs=pl.BlockSpec((tm, tn), lambda i, j: (i, j)),
        out_shape=jax.ShapeDtypeStruct((m, n), out_dtype),
        compiler_params=_params(("parallel", "arbitrary"), vmem_mib),
        name=name,
    )(*args)


def _rotary_mix(t, lane):
    half = ROT_DIM // 2
    return jnp.where(lane < half, pltpu.roll(t, HEAD_DIM - half, 1), pltpu.roll(t, half, 1))


def _attn_kernel(sink_ref, q_ref, k_ref, v_ref, cq_ref, sq_ref, ck_ref, sk_ref,
                 o_ref, kpad_ref, vpad_ref, *, seq):
    g = pl.program_id(1)
    nb = seq // BLOCK
    zeros = jnp.zeros((BLOCK, HEAD_DIM), jnp.bfloat16)
    kpad_ref[0:BLOCK, :] = zeros
    kpad_ref[BLOCK + seq:, :] = zeros
    vpad_ref[0:BLOCK, :] = zeros
    vpad_ref[BLOCK + seq:, :] = zeros
    lane_full = lax.broadcasted_iota(jnp.int32, (seq, HEAD_DIM), 1)
    k = k_ref[...]
    k_rot = k * ck_ref[...] + _rotary_mix(k, lane_full) * sk_ref[...]
    kpad_ref[BLOCK:BLOCK + seq, :] = k_rot.astype(jnp.bfloat16)
    vpad_ref[BLOCK:BLOCK + seq, :] = v_ref[...].astype(jnp.bfloat16)

    rows = GROUP * BLOCK
    row_head = lax.broadcasted_iota(jnp.int32, (rows, 1), 0) // BLOCK
    sink = jnp.zeros((rows, 1), jnp.float32)
    for h in range(GROUP):
        sink = jnp.where(row_head == h, sink_ref[g * GROUP + h], sink)

    def body(n, carry):
        r0 = pl.multiple_of(n * BLOCK, BLOCK)
        qi = lax.broadcasted_iota(jnp.int32, (rows, 3 * BLOCK), 0) % BLOCK
        kj = lax.broadcasted_iota(jnp.int32, (rows, 3 * BLOCK), 1)
        band = (kj >= qi) & (kj <= qi + 2 * BLOCK)
        lane_blk = lax.broadcasted_iota(jnp.int32, (BLOCK, HEAD_DIM), 1)
        cq = cq_ref[pl.ds(r0, BLOCK), :]
        sq = sq_ref[pl.ds(r0, BLOCK), :]
        heads = []
        for h in range(GROUP):
            qh = q_ref[pl.ds(r0, BLOCK), h * HEAD_DIM:(h + 1) * HEAD_DIM]
            heads.append((qh * cq + _rotary_mix(qh, lane_blk) * sq).astype(jnp.bfloat16))
        q = jnp.concatenate(heads, axis=0)
        kw = kpad_ref[pl.ds(r0, 3 * BLOCK), :]
        vw = vpad_ref[pl.ds(r0, 3 * BLOCK), :]
        s = lax.dot_general(q, kw, (((1,), (1,)), ((), ())),
                            preferred_element_type=jnp.float32)
        j_lo = jnp.where(n == 0, BLOCK, 0)
        j_hi = jnp.where(n == nb - 1, 2 * BLOCK, 3 * BLOCK)
        valid = band & (kj >= j_lo) & (kj < j_hi)
        s = jnp.where(valid, s, -jnp.inf)
        m = jnp.maximum(jnp.max(s, axis=-1, keepdims=True), sink)
        p = jnp.exp(s - m)
        denom = jnp.sum(p, axis=-1, keepdims=True) + jnp.exp(sink - m)
        o = jnp.dot(p.astype(jnp.bfloat16), vw, preferred_element_type=jnp.float32)
        o = o / denom
        for h in range(GROUP):
            o_ref[pl.ds(r0, BLOCK), h * HEAD_DIM:(h + 1) * HEAD_DIM] = (
                o[h * BLOCK:(h + 1) * BLOCK, :].astype(o_ref.dtype))
        return carry

    lax.fori_loop(0, nb, body, 0)


def _attention(proj, sink_l, tables, batch, seq):
    cq, sq, ck, sk = tables
    m = proj.shape[0]
    qw = GROUP * HEAD_DIM
    tab_spec = pl.BlockSpec((seq, HEAD_DIM), lambda b, g: (0, 0))
    return pl.pallas_call(
        functools.partial(_attn_kernel, seq=seq),
        grid=(batch, N_KV_HEADS),
        in_specs=[pl.BlockSpec(memory_space=pltpu.SMEM),
                  pl.BlockSpec((seq, qw), lambda b, g: (b, Q_OFF // qw + g)),
                  pl.BlockSpec((seq, HEAD_DIM), lambda b, g: (b, K_OFF // HEAD_DIM + g)),
                  pl.BlockSpec((seq, HEAD_DIM), lambda b, g: (b, V_OFF // HEAD_DIM + g)),
                  tab_spec, tab_spec, tab_spec, tab_spec],
        out_specs=pl.BlockSpec((seq, qw), lambda b, g: (b, g)),
        out_shape=jax.ShapeDtypeStruct((m, ATTN_WIDTH), jnp.bfloat16),
        scratch_shapes=[pltpu.VMEM((seq + 2 * BLOCK, HEAD_DIM), jnp.bfloat16),
                        pltpu.VMEM((seq + 2 * BLOCK, HEAD_DIM), jnp.bfloat16)],
        compiler_params=_params(("parallel", "arbitrary"), 40),
        name="window_gqa",
    )(sink_l, proj, proj, proj, cq, sq, ck, sk)


def _pool_kernel(u_ref, w_ref, scale_ref, o_ref, pad_ref, *, seq):
    g = pl.program_id(1)
    zeros = jnp.zeros((POOL_HALO, POOL_GROUP_DIM), jnp.float32)
    pad_ref[0:POOL_HALO, :] = zeros
    pad_ref[POOL_HALO + seq:, :] = zeros
    pad_ref[POOL_HALO:POOL_HALO + seq, :] = u_ref[...]
    wmat = w_ref[...].astype(jnp.bfloat16)
    scale = scale_ref[...]

    def run(win):
        half = win // 2
        ext = POOL_CHUNK + 2 * POOL_HALO

        def body(c, carry):
            t0 = pl.multiple_of(c * POOL_CHUNK, POOL_CHUNK)
            xe = pad_ref[pl.ds(t0, ext), :]
            a = xe
            step = 1
            while step < win:
                a = a + pltpu.roll(a, step, 0)
                step *= 2
            if half > 1:
                a = pltpu.roll(a, ext - (half - 1), 0)
            wsum = a[POOL_HALO:POOL_HALO + POOL_CHUNK, :]
            x = xe[POOL_HALO:POOL_HALO + POOL_CHUNK, :]
            t = t0 + lax.broadcasted_iota(jnp.int32, (POOL_CHUNK, 1), 0)
            lo = jnp.maximum(t - half, 0)
            hi = jnp.minimum(t + half - 1, seq - 1)
            cnt = (hi - lo + 1).astype(jnp.float32)
            pooled = wsum / cnt - x
            mixed = jnp.dot(pooled.astype(jnp.bfloat16), wmat,
                            preferred_element_type=jnp.float32)
            o_ref[pl.ds(t0, POOL_CHUNK), :] = (mixed * scale).astype(o_ref.dtype)
            return carry

        lax.fori_loop(0, seq // POOL_CHUNK, body, 0)

    for gi, win in enumerate(POOL_WINDOWS):
        pl.when(g == gi)(functools.partial(run, win))


def _pool(proj, pool_w, pool_scale3, layer, batch, seq):
    m = proj.shape[0]
    c = POOL_GROUP_DIM
    return pl.pallas_call(
        functools.partial(_pool_kernel, seq=seq),
        grid=(batch, N_POOL_GROUPS),
        in_specs=[pl.BlockSpec((seq, c), lambda b, g: (b, U_OFF // c + g)),
                  pl.BlockSpec((None, None, c, c), lambda b, g: (layer, g, 0, 0)),
                  pl.BlockSpec((None, 1, c), lambda b, g: (layer, 0, g))],
        out_specs=pl.BlockSpec((seq, c), lambda b, g: (b, g)),
        out_shape=jax.ShapeDtypeStruct((m, POOL_WIDTH), jnp.bfloat16),
        scratch_shapes=[pltpu.VMEM((seq + 2 * POOL_HALO, c), jnp.float32)],
        compiler_params=_params(("parallel", "arbitrary"), 40),
        name="multiscale_pool",
    )(proj, pool_w, pool_scale3)


def _merge_kernel(a_ref, p_ref, wa_ref, wp_ref, ga_ref, gb_ref, o_ref):
    ya = jnp.dot(a_ref[...], wa_ref[...].astype(jnp.bfloat16),
                 preferred_element_type=jnp.float32)
    yp = jnp.dot(p_ref[...], wp_ref[...].astype(jnp.bfloat16),
                 preferred_element_type=jnp.float32)
    merged = jax.nn.sigmoid(ga_ref[...]) * ya + jax.nn.sigmoid(gb_ref[...]) * yp
    o_ref[...] = merged.astype(o_ref.dtype)


def _merge(o_attn, mixed, wba, wbp, proj, layer, tm=1024, tn=512):
    m = o_attn.shape[0]
    return pl.pallas_call(
        _merge_kernel,
        grid=(m // tm, D_MODEL // tn),
        in_specs=[_panel_spec(tm, ATTN_WIDTH),
                  _panel_spec(tm, POOL_WIDTH),
                  pl.BlockSpec((None, ATTN_WIDTH, tn), lambda i, j: (layer, 0, j)),
                  pl.BlockSpec((None, POOL_WIDTH, tn), lambda i, j: (layer, 0, j)),
                  pl.BlockSpec((tm, tn), lambda i, j: (i, GA_OFF // tn + j)),
                  pl.BlockSpec((tm, tn), lambda i, j: (i, GB_OFF // tn + j))],
        out_specs=pl.BlockSpec((tm, tn), lambda i, j: (i, j)),
        out_shape=jax.ShapeDtypeStruct((m, D_MODEL), jnp.bfloat16),
        compiler_params=_params(("parallel", "arbitrary"), 48),
        name="branch_merge",
    )(o_attn, mixed, wba, wbp, proj, proj)


def _gate_up_kernel(x_ref, wg_ref, wu_ref, o_ref):
    x = x_ref[...]
    gate = jnp.dot(x, wg_ref[...].astype(jnp.bfloat16), preferred_element_type=jnp.float32)
    up = jnp.dot(x, wu_ref[...].astype(jnp.bfloat16), preferred_element_type=jnp.float32)
    o_ref[...] = (jax.nn.silu(gate) * up).astype(o_ref.dtype)


def _gate_up(h2, w_gate_up, layer, tm=2048, tn=256):
    m, k = h2.shape
    nt = D_FF // tn
    return pl.pallas_call(
        _gate_up_kernel,
        grid=(m // tm, nt),
        in_specs=[_panel_spec(tm, k),
                  pl.BlockSpec((None, k, tn), lambda i, j: (layer, 0, j)),
                  pl.BlockSpec((None, k, tn), lambda i, j: (layer, 0, nt + j))],
        out_specs=pl.BlockSpec((tm, tn), lambda i, j: (i, j)),
        out_shape=jax.ShapeDtypeStruct((m, D_FF), jnp.bfloat16),
        compiler_params=_params(("parallel", "arbitrary"), 48),
        name="swiglu_gate_up",
    )(h2, w_gate_up, w_gate_up)


def _rotary_tables(seq):
    pos = jnp.arange(seq, dtype=jnp.float32)
    inv_freq = 1.0 / jnp.power(jnp.float32(ROPE_THETA),
                               jnp.arange(0, ROT_DIM, 2, dtype=jnp.float32) / ROT_DIM)
    ang = pos[:, None] * inv_freq[None, :]
    cos, sin = jnp.cos(ang), jnp.sin(ang)
    half = ROT_DIM // 2
    ones = jnp.ones((seq, HEAD_DIM - ROT_DIM), jnp.float32)
    zeros = jnp.zeros((seq, HEAD_DIM - ROT_DIM), jnp.float32)
    c_full = jnp.concatenate([cos, cos, ones], axis=-1)
    s_full = jnp.concatenate([-sin, sin, zeros], axis=-1)
    assert half * 2 == ROT_DIM
    scale = HEAD_DIM ** -0.5
    return c_full * scale, s_full * scale, c_full, s_full


def kernel(x, norm1_g, w_in, attn_sink, pool_w, pool_scale, w_branch_attn, w_branch_pool,
           w_out, norm2_g, w_gate_up, w_down, final_norm_g):
    batch, seq, d = x.shape
    m = batch * seq
    bf16 = jnp.bfloat16
    tables = _rotary_tables(seq)
    w_down_b = w_down.astype(bf16)
    pool_scale3 = pool_scale.reshape(DEPTH, 1, POOL_WIDTH)

    xs = x.reshape(m, d)
    for l in range(DEPTH):
        h = _rmsnorm(xs, norm1_g[l].reshape(1, d), bf16)
        proj = _matmul(h, w_in, l, jnp.float32, 2048, 512, 56, "in_proj")
        o_attn = _attention(proj, attn_sink[l], tables, batch, seq)
        mixed = _pool(proj, pool_w, pool_scale3, l, batch, seq)
        merged = _merge(o_attn, mixed, w_branch_attn, w_branch_pool, proj, l)
        xs = _matmul(merged, w_out, l, jnp.float32, 1024, 512, 52, "out_proj", residual=xs)
        h2 = _rmsnorm(xs, norm2_g[l].reshape(1, d), bf16)
        act = _gate_up(h2, w_gate_up, l)
        xs = _matmul(act, w_down_b, l, jnp.float32, 1024, 256, 52, "ffn_down", residual=xs)
    out = _rmsnorm(xs, final_norm_g.reshape(1, d), jnp.float32)
    return out.reshape(batch, seq, d)
```

```python
import functools

import jax
import jax.numpy as jnp
from jax import lax
from jax.experimental import pallas as pl
from jax.experimental.pallas import tpu as pltpu

D_MODEL = 4096
DEPTH = 2
HEAD_DIM = 128
N_HEADS = 16
N_KV_HEADS = 4
GROUP = N_HEADS // N_KV_HEADS
ATTN_WIDTH = N_HEADS * HEAD_DIM
KV_WIDTH = N_KV_HEADS * HEAD_DIM
BLOCK = 128
ROPE_THETA = 500000.0
ROT_DIM = HEAD_DIM // 4
POOL_WIDTH = D_MODEL // 2
POOL_WINDOWS = (2, 4, 8, 16)
N_POOL_GROUPS = len(POOL_WINDOWS)
POOL_GROUP_DIM = POOL_WIDTH // N_POOL_GROUPS
Q_OFF = 0
K_OFF = ATTN_WIDTH
V_OFF = K_OFF + KV_WIDTH
U_OFF = V_OFF + KV_WIDTH
GA_OFF = U_OFF + POOL_WIDTH
GB_OFF = GA_OFF + D_MODEL
IN_WIDTH = GB_OFF + D_MODEL
D_FF = 11008
RMS_EPS = 1e-6

POOL_HALO = 8
POOL_CHUNK = 128
SUB_M = 1024
MERGE_SUB_M = 256
MIB = 1024 * 1024


def _params(semantics, vmem_mib):
    return pltpu.CompilerParams(dimension_semantics=semantics,
                                vmem_limit_bytes=vmem_mib * MIB)


def _rmsnorm_kernel(x_ref, g_ref, o_ref):
    x = x_ref[...]
    ms = jnp.mean(x * x, axis=-1, keepdims=True)
    y = x * lax.rsqrt(ms + RMS_EPS)
    o_ref[...] = (y * g_ref[...]).astype(o_ref.dtype)


def _rmsnorm(x2d, g_row, out_dtype, tr=256):
    m, d = x2d.shape
    return pl.pallas_call(
        _rmsnorm_kernel,
        grid=(m // tr,),
        in_specs=[pl.BlockSpec((tr, d), lambda i: (i, 0)),
                  pl.BlockSpec((1, d), lambda i: (0, 0))],
        out_specs=pl.BlockSpec((tr, d), lambda i: (i, 0)),
        out_shape=jax.ShapeDtypeStruct((m, d), out_dtype),
        compiler_params=_params(("parallel",), 40),
        name="rmsnorm",
    )(x2d, g_row)


def _row_chunks(tm, sub=SUB_M):
    step = min(tm, sub)
    return [(r, step) for r in range(0, tm, step)]


def _mm_kernel(x_ref, w_ref, o_ref):
    w = w_ref[...].astype(jnp.bfloat16)
    for r, n in _row_chunks(x_ref.shape[0]):
        acc = jnp.dot(x_ref[r:r + n, :], w, preferred_element_type=jnp.float32)
        o_ref[r:r + n, :] = acc.astype(o_ref.dtype)


def _mm_res_kernel(x_ref, w_ref, r_ref, o_ref):
    w = w_ref[...].astype(jnp.bfloat16)
    for r, n in _row_chunks(x_ref.shape[0]):
        acc = jnp.dot(x_ref[r:r + n, :], w, preferred_element_type=jnp.float32)
        o_ref[r:r + n, :] = r_ref[r:r + n, :] + acc


def _panel_spec(tm, k, buffers=1):
    return pl.BlockSpec((tm, k), lambda i, j: (i, 0), pipeline_mode=pl.Buffered(buffers))


def _matmul(x, w_stack, layer, out_dtype, tm, tn, vmem_mib, name, residual=None,
            panel_buffers=1):
    m, k = x.shape
    n = w_stack.shape[-1]
    in_specs = [_panel_spec(tm, k, panel_buffers),
                pl.BlockSpec((None, k, tn), lambda i, j: (layer, 0, j))]
    args = [x, w_stack]
    kern = _mm_kernel
    if residual is not None:
        in_specs.append(pl.BlockSpec((tm, tn), lambda i, j: (i, j)))
        args.append(residual)
        kern = _mm_res_kernel
    return pl.pallas_call(
        kern,
        grid=(m // tm, n // tn),
        in_specs=in_specs,
        out_specs=pl.BlockSpec((tm, tn), lambda i, j: (i, j)),
        out_shape=jax.ShapeDtypeStruct((m, n), out_dtype),
        compiler_params=_params(("parallel", "arbitrary"), vmem_mib),
        name=name,
    )(*args)


def _rotary_mix(t, lane):
    half = ROT_DIM // 2
    return jnp.where(lane < half, pltpu.roll(t, HEAD_DIM - half, 1), pltpu.roll(t, half, 1))


def _attn_kernel(sink_ref, q_ref, k_ref, v_ref, cq_ref, sq_ref, ck_ref, sk_ref,
                 o_ref, kpad_ref, vpad_ref, *, seq):
    g = pl.program_id(1)
    nb = seq // BLOCK
    zeros = jnp.zeros((BLOCK, HEAD_DIM), jnp.bfloat16)
    kpad_ref[0:BLOCK, :] = zeros
    kpad_ref[BLOCK + seq:, :] = zeros
    vpad_ref[0:BLOCK, :] = zeros
    vpad_ref[BLOCK + seq:, :] = zeros
    lane_full = lax.broadcasted_iota(jnp.int32, (seq, HEAD_DIM), 1)
    k = k_ref[...]
    k_rot = k * ck_ref[...] + _rotary_mix(k, lane_full) * sk_ref[...]
    kpad_ref[BLOCK:BLOCK + seq, :] = k_rot.astype(jnp.bfloat16)
    vpad_ref[BLOCK:BLOCK + seq, :] = v_ref[...].astype(jnp.bfloat16)

    rows = GROUP * BLOCK
    row_head = lax.broadcasted_iota(jnp.int32, (rows, 1), 0) // BLOCK
    sink = jnp.zeros((rows, 1), jnp.float32)
    for h in range(GROUP):
        sink = jnp.where(row_head == h, sink_ref[g * GROUP + h], sink)

    def body(n, carry):
        r0 = pl.multiple_of(n * BLOCK, BLOCK)
        qi = lax.broadcasted_iota(jnp.int32, (rows, 3 * BLOCK), 0) % BLOCK
        kj = lax.broadcasted_iota(jnp.int32, (rows, 3 * BLOCK), 1)
        band = (kj >= qi) & (kj <= qi + 2 * BLOCK)
        lane_blk = lax.broadcasted_iota(jnp.int32, (BLOCK, HEAD_DIM), 1)
        cq = cq_ref[pl.ds(r0, BLOCK), :]
        sq = sq_ref[pl.ds(r0, BLOCK), :]
        heads = []
        for h in range(GROUP):
            qh = q_ref[pl.ds(r0, BLOCK), h * HEAD_DIM:(h + 1) * HEAD_DIM]
            heads.append((qh * cq + _rotary_mix(qh, lane_blk) * sq).astype(jnp.bfloat16))
        q = jnp.concatenate(heads, axis=0)
        kw = kpad_ref[pl.ds(r0, 3 * BLOCK), :]
        vw = vpad_ref[pl.ds(r0, 3 * BLOCK), :]
        s = lax.dot_general(q, kw, (((1,), (1,)), ((), ())),
                            preferred_element_type=jnp.float32)
        j_lo = jnp.where(n == 0, BLOCK, 0)
        j_hi = jnp.where(n == nb - 1, 2 * BLOCK, 3 * BLOCK)
        valid = band & (kj >= j_lo) & (kj < j_hi)
        s = jnp.where(valid, s, -jnp.inf)
        m = jnp.maximum(jnp.max(s, axis=-1, keepdims=True), sink)
        p = jnp.exp(s - m)
        denom = jnp.sum(p, axis=-1, keepdims=True) + jnp.exp(sink - m)
        o = jnp.dot(p.astype(jnp.bfloat16), vw, preferred_element_type=jnp.float32)
        o = o / denom
        for h in range(GROUP):
            o_ref[pl.ds(r0, BLOCK), h * HEAD_DIM:(h + 1) * HEAD_DIM] = (
                o[h * BLOCK:(h + 1) * BLOCK, :].astype(o_ref.dtype))
        return carry

    lax.fori_loop(0, nb, body, 0)


def _attention(proj, sink_l, tables, batch, seq):
    cq, sq, ck, sk = tables
    m = proj.shape[0]
    qw = GROUP * HEAD_DIM
    tab_spec = pl.BlockSpec((seq, HEAD_DIM), lambda b, g: (0, 0))
    return pl.pallas_call(
        functools.partial(_attn_kernel, seq=seq),
        grid=(batch, N_KV_HEADS),
        in_specs=[pl.BlockSpec(memory_space=pltpu.SMEM),
                  pl.BlockSpec((seq, qw), lambda b, g: (b, Q_OFF // qw + g)),
                  pl.BlockSpec((seq, HEAD_DIM), lambda b, g: (b, K_OFF // HEAD_DIM + g)),
                  pl.BlockSpec((seq, HEAD_DIM), lambda b, g: (b, V_OFF // HEAD_DIM + g)),
                  tab_spec, tab_spec, tab_spec, tab_spec],
        out_specs=pl.BlockSpec((seq, qw), lambda b, g: (b, g)),
        out_shape=jax.ShapeDtypeStruct((m, ATTN_WIDTH), jnp.bfloat16),
        scratch_shapes=[pltpu.VMEM((seq + 2 * BLOCK, HEAD_DIM), jnp.bfloat16),
                        pltpu.VMEM((seq + 2 * BLOCK, HEAD_DIM), jnp.bfloat16)],
        compiler_params=_params(("parallel", "arbitrary"), 40),
        name="window_gqa",
    )(sink_l, proj, proj, proj, cq, sq, ck, sk)


def _pool_kernel(u_ref, w_ref, scale_ref, o_ref, pad_ref, *, seq):
    g = pl.program_id(1)
    zeros = jnp.zeros((POOL_HALO, POOL_GROUP_DIM), jnp.float32)
    pad_ref[0:POOL_HALO, :] = zeros
    pad_ref[POOL_HALO + seq:, :] = zeros
    pad_ref[POOL_HALO:POOL_HALO + seq, :] = u_ref[...]
    wmat = w_ref[...].astype(jnp.bfloat16)
    scale = scale_ref[...]

    def run(win):
        half = win // 2
        ext = POOL_CHUNK + 2 * POOL_HALO

        def body(c, carry):
            t0 = pl.multiple_of(c * POOL_CHUNK, POOL_CHUNK)
            xe = pad_ref[pl.ds(t0, ext), :]
            a = xe
            step = 1
            while step < win:
                a = a + pltpu.roll(a, step, 0)
                step *= 2
            if half > 1:
                a = pltpu.roll(a, ext - (half - 1), 0)
            wsum = a[POOL_HALO:POOL_HALO + POOL_CHUNK, :]
            x = xe[POOL_HALO:POOL_HALO + POOL_CHUNK, :]
            t = t0 + lax.broadcasted_iota(jnp.int32, (POOL_CHUNK, 1), 0)
            lo = jnp.maximum(t - half, 0)
            hi = jnp.minimum(t + half - 1, seq - 1)
            cnt = (hi - lo + 1).astype(jnp.float32)
            pooled = wsum / cnt - x
            mixed = jnp.dot(pooled.astype(jnp.bfloat16), wmat,
                            preferred_element_type=jnp.float32)
            o_ref[pl.ds(t0, POOL_CHUNK), :] = (mixed * scale).astype(o_ref.dtype)
            return carry

        lax.fori_loop(0, seq // POOL_CHUNK, body, 0)

    for gi, win in enumerate(POOL_WINDOWS):
        pl.when(g == gi)(functools.partial(run, win))


def _pool(proj, pool_w, pool_scale3, layer, batch, seq):
    m = proj.shape[0]
    c = POOL_GROUP_DIM
    return pl.pallas_call(
        functools.partial(_pool_kernel, seq=seq),
        grid=(batch, N_POOL_GROUPS),
        in_specs=[pl.BlockSpec((seq, c), lambda b, g: (b, U_OFF // c + g)),
                  pl.BlockSpec((None, None, c, c), lambda b, g: (layer, g, 0, 0)),
                  pl.BlockSpec((None, 1, c), lambda b, g: (layer, 0, g))],
        out_specs=pl.BlockSpec((seq, c), lambda b, g: (b, g)),
        out_shape=jax.ShapeDtypeStruct((m, POOL_WIDTH), jnp.bfloat16),
        scratch_shapes=[pltpu.VMEM((seq + 2 * POOL_HALO, c), jnp.float32)],
        compiler_params=_params(("parallel", "arbitrary"), 40),
        name="multiscale_pool",
    )(proj, pool_w, pool_scale3)


def _merge_kernel(a_ref, p_ref, wa_ref, wp_ref, ga_ref, gb_ref, o_ref):
    wa = wa_ref[...].astype(jnp.bfloat16)
    wp = wp_ref[...].astype(jnp.bfloat16)
    for r, n in _row_chunks(a_ref.shape[0], MERGE_SUB_M):
        ya = jnp.dot(a_ref[r:r + n, :], wa, preferred_element_type=jnp.float32)
        yp = jnp.dot(p_ref[r:r + n, :], wp, preferred_element_type=jnp.float32)
        merged = (jax.nn.sigmoid(ga_ref[r:r + n, :]) * ya
                  + jax.nn.sigmoid(gb_ref[r:r + n, :]) * yp)
        o_ref[r:r + n, :] = merged.astype(o_ref.dtype)


def _merge(o_attn, mixed, wba, wbp, proj, layer, tm=2048, tn=256):
    m = o_attn.shape[0]
    return pl.pallas_call(
        _merge_kernel,
        grid=(m // tm, D_MODEL // tn),
        in_specs=[_panel_spec(tm, ATTN_WIDTH),
                  _panel_spec(tm, POOL_WIDTH),
                  pl.BlockSpec((None, ATTN_WIDTH, tn), lambda i, j: (layer, 0, j)),
                  pl.BlockSpec((None, POOL_WIDTH, tn), lambda i, j: (layer, 0, j)),
                  pl.BlockSpec((tm, tn), lambda i, j: (i, GA_OFF // tn + j)),
                  pl.BlockSpec((tm, tn), lambda i, j: (i, GB_OFF // tn + j))],
        out_specs=pl.BlockSpec((tm, tn), lambda i, j: (i, j)),
        out_shape=jax.ShapeDtypeStruct((m, D_MODEL), jnp.bfloat16),
        compiler_params=_params(("parallel", "arbitrary"), 48),
        name="branch_merge",
    )(o_attn, mixed, wba, wbp, proj, proj)


def _gate_up_kernel(x_ref, wg_ref, wu_ref, o_ref):
    wg = wg_ref[...].astype(jnp.bfloat16)
    wu = wu_ref[...].astype(jnp.bfloat16)
    for r, n in _row_chunks(x_ref.shape[0]):
        x = x_ref[r:r + n, :]
        gate = jnp.dot(x, wg, preferred_element_type=jnp.float32)
        up = jnp.dot(x, wu, preferred_element_type=jnp.float32)
        o_ref[r:r + n, :] = (jax.nn.silu(gate) * up).astype(o_ref.dtype)


def _gate_up(h2, w_gate_up, layer, tm=2048, tn=256):
    m, k = h2.shape
    nt = D_FF // tn
    return pl.pallas_call(
        _gate_up_kernel,
        grid=(m // tm, nt),
        in_specs=[_panel_spec(tm, k),
                  pl.BlockSpec((None, k, tn), lambda i, j: (layer, 0, j)),
                  pl.BlockSpec((None, k, tn), lambda i, j: (layer, 0, nt + j))],
        out_specs=pl.BlockSpec((tm, tn), lambda i, j: (i, j)),
        out_shape=jax.ShapeDtypeStruct((m, D_FF), jnp.bfloat16),
        compiler_params=_params(("parallel", "arbitrary"), 48),
        name="swiglu_gate_up",
    )(h2, w_gate_up, w_gate_up)


def _rotary_tables(seq):
    pos = jnp.arange(seq, dtype=jnp.float32)
    inv_freq = 1.0 / jnp.power(jnp.float32(ROPE_THETA),
                               jnp.arange(0, ROT_DIM, 2, dtype=jnp.float32) / ROT_DIM)
    ang = pos[:, None] * inv_freq[None, :]
    cos, sin = jnp.cos(ang), jnp.sin(ang)
    half = ROT_DIM // 2
    ones = jnp.ones((seq, HEAD_DIM - ROT_DIM), jnp.float32)
    zeros = jnp.zeros((seq, HEAD_DIM - ROT_DIM), jnp.float32)
    c_full = jnp.concatenate([cos, cos, ones], axis=-1)
    s_full = jnp.concatenate([-sin, sin, zeros], axis=-1)
    assert half * 2 == ROT_DIM
    scale = HEAD_DIM ** -0.5
    return c_full * scale, s_full * scale, c_full, s_full


def kernel(x, norm1_g, w_in, attn_sink, pool_w, pool_scale, w_branch_attn, w_branch_pool,
           w_out, norm2_g, w_gate_up, w_down, final_norm_g):
    batch, seq, d = x.shape
    m = batch * seq
    bf16 = jnp.bfloat16
    tables = _rotary_tables(seq)
    w_down_b = w_down.astype(bf16)
    pool_scale3 = pool_scale.reshape(DEPTH, 1, POOL_WIDTH)

    xs = x.reshape(m, d)
    for l in range(DEPTH):
        h = _rmsnorm(xs, norm1_g[l].reshape(1, d), bf16)
        proj = _matmul(h, w_in, l, jnp.float32, 2048, 512, 56, "in_proj")
        o_attn = _attention(proj, attn_sink[l], tables, batch, seq)
        mixed = _pool(proj, pool_w, pool_scale3, l, batch, seq)
        merged = _merge(o_attn, mixed, w_branch_attn, w_branch_pool, proj, l)
        xs = _matmul(merged, w_out, l, jnp.float32, 2048, 256, 52, "out_proj", residual=xs)
        h2 = _rmsnorm(xs, norm2_g[l].reshape(1, d), bf16)
        act = _gate_up(h2, w_gate_up, l)
        xs = _matmul(act, w_down_b, l, jnp.float32, 512, 512, 56, "ffn_down", residual=xs,
                     panel_buffers=2)
    out = _rmsnorm(xs, final_norm_g.reshape(1, d), jnp.float32)
    return out.reshape(batch, seq, d)
```

```python
import functools

import jax
import jax.numpy as jnp
import numpy as np
from jax import lax
from jax.experimental import pallas as pl
from jax.experimental.pallas import tpu as pltpu

D_MODEL = 4096
DEPTH = 2
HEAD_DIM = 128
N_HEADS = 16
N_KV_HEADS = 4
GROUP = N_HEADS // N_KV_HEADS
ATTN_WIDTH = N_HEADS * HEAD_DIM
KV_WIDTH = N_KV_HEADS * HEAD_DIM
BLOCK = 128
ROPE_THETA = 500000.0
ROT_DIM = HEAD_DIM // 4
POOL_WIDTH = D_MODEL // 2
POOL_WINDOWS = (2, 4, 8, 16)
N_POOL_GROUPS = len(POOL_WINDOWS)
POOL_GROUP_DIM = POOL_WIDTH // N_POOL_GROUPS
Q_OFF = 0
K_OFF = ATTN_WIDTH
V_OFF = K_OFF + KV_WIDTH
U_OFF = V_OFF + KV_WIDTH
GA_OFF = U_OFF + POOL_WIDTH
GB_OFF = GA_OFF + D_MODEL
IN_WIDTH = GB_OFF + D_MODEL
D_FF = 11008
RMS_EPS = 1e-6
MASK_NEG = -1e30

POOL_HALO = 8
POOL_CHUNK = 128
SUB_M = 1024
MERGE_SUB_M = 256
MIB = 1024 * 1024


def _params(semantics, vmem_mib):
    return pltpu.CompilerParams(dimension_semantics=semantics,
                                vmem_limit_bytes=vmem_mib * MIB)


def _rmsnorm_kernel(x_ref, g_ref, o_ref):
    x = x_ref[...]
    ms = jnp.mean(x * x, axis=-1, keepdims=True)
    y = x * lax.rsqrt(ms + RMS_EPS)
    o_ref[...] = (y * g_ref[...]).astype(o_ref.dtype)


def _rmsnorm(x2d, g_row, out_dtype, tr=256):
    m, d = x2d.shape
    return pl.pallas_call(
        _rmsnorm_kernel,
        grid=(m // tr,),
        in_specs=[pl.BlockSpec((tr, d), lambda i: (i, 0)),
                  pl.BlockSpec((1, d), lambda i: (0, 0))],
        out_specs=pl.BlockSpec((tr, d), lambda i: (i, 0)),
        out_shape=jax.ShapeDtypeStruct((m, d), out_dtype),
        compiler_params=_params(("parallel",), 40),
        name="rmsnorm",
    )(x2d, g_row)


def _row_chunks(tm, sub=SUB_M):
    step = min(tm, sub)
    return [(r, step) for r in range(0, tm, step)]


def _mm_kernel(x_ref, w_ref, o_ref):
    w = w_ref[...].astype(jnp.bfloat16)
    for r, n in _row_chunks(x_ref.shape[0]):
        acc = jnp.dot(x_ref[r:r + n, :], w, preferred_element_type=jnp.float32)
        o_ref[r:r + n, :] = acc.astype(o_ref.dtype)


def _mm_res_kernel(x_ref, w_ref, r_ref, o_ref):
    w = w_ref[...].astype(jnp.bfloat16)
    for r, n in _row_chunks(x_ref.shape[0]):
        acc = jnp.dot(x_ref[r:r + n, :], w, preferred_element_type=jnp.float32)
        o_ref[r:r + n, :] = r_ref[r:r + n, :] + acc


def _panel_spec(tm, k, buffers=1):
    return pl.BlockSpec((tm, k), lambda i, j: (i, 0), pipeline_mode=pl.Buffered(buffers))


def _matmul(x, w_stack, layer, out_dtype, tm, tn, vmem_mib, name, residual=None,
            panel_buffers=1):
    m, k = x.shape
    n = w_stack.shape[-1]
    in_specs = [_panel_spec(tm, k, panel_buffers),
                pl.BlockSpec((None, k, tn), lambda i, j: (layer, 0, j))]
    args = [x, w_stack]
    kern = _mm_kernel
    if residual is not None:
        in_specs.append(pl.BlockSpec((tm, tn), lambda i, j: (i, j)))
        args.append(residual)
        kern = _mm_res_kernel
    return pl.pallas_call(
        kern,
        grid=(m // tm, n // tn),
        in_specs=in_specs,
        out_specs=pl.BlockSpec((tm, tn), lambda i, j: (i, j)),
        out_shape=jax.ShapeDtypeStruct((m, n), out_dtype),
        compiler_params=_params(("parallel", "arbitrary"), vmem_mib),
        name=name,
    )(*args)


def _rotary_mix(t, lane):
    half = ROT_DIM // 2
    return jnp.where(lane < half, pltpu.roll(t, HEAD_DIM - half, 1), pltpu.roll(t, half, 1))


def _attn_kernel(sink_ref, q_ref, k_ref, v_ref, cq_ref, sq_ref, ck_ref, sk_ref, bias_ref,
                 o_ref, kpad_ref, vpad_ref, *, seq):
    g = pl.program_id(1)
    nb = seq // BLOCK
    zeros = jnp.zeros((BLOCK, HEAD_DIM), jnp.bfloat16)
    kpad_ref[0:BLOCK, :] = zeros
    kpad_ref[BLOCK + seq:, :] = zeros
    vpad_ref[0:BLOCK, :] = zeros
    vpad_ref[BLOCK + seq:, :] = zeros
    lane_full = lax.broadcasted_iota(jnp.int32, (seq, HEAD_DIM), 1)
    k = k_ref[...]
    k_rot = k * ck_ref[...] + _rotary_mix(k, lane_full) * sk_ref[...]
    kpad_ref[BLOCK:BLOCK + seq, :] = k_rot.astype(jnp.bfloat16)
    vpad_ref[BLOCK:BLOCK + seq, :] = v_ref[...].astype(jnp.bfloat16)

    rows = GROUP * BLOCK
    row_head = lax.broadcasted_iota(jnp.int32, (rows, 1), 0) // BLOCK
    sink = jnp.zeros((rows, 1), jnp.float32)
    for h in range(GROUP):
        sink = jnp.where(row_head == h, sink_ref[g * GROUP + h], sink)

    def body(n, carry):
        r0 = pl.multiple_of(n * BLOCK, BLOCK)
        lane_blk = lax.broadcasted_iota(jnp.int32, (BLOCK, HEAD_DIM), 1)
        cq = cq_ref[pl.ds(r0, BLOCK), :]
        sq = sq_ref[pl.ds(r0, BLOCK), :]
        heads = []
        for h in range(GROUP):
            qh = q_ref[pl.ds(r0, BLOCK), h * HEAD_DIM:(h + 1) * HEAD_DIM]
            heads.append((qh * cq + _rotary_mix(qh, lane_blk) * sq).astype(jnp.bfloat16))
        q = jnp.concatenate(heads, axis=0)
        kw = kpad_ref[pl.ds(r0, 3 * BLOCK), :]
        vw = vpad_ref[pl.ds(r0, 3 * BLOCK), :]
        s = lax.dot_general(q, kw, (((1,), (1,)), ((), ())),
                            preferred_element_type=jnp.float32)
        which = jnp.where(n == 0, 0, jnp.where(n == nb - 1, 2, 1))
        s = s + jnp.concatenate([bias_ref[which]] * GROUP, axis=0)
        m = jnp.maximum(jnp.max(s, axis=-1, keepdims=True), sink)
        p = jnp.exp(s - m)
        denom = jnp.sum(p, axis=-1, keepdims=True) + jnp.exp(sink - m)
        o = jnp.dot(p.astype(jnp.bfloat16), vw, preferred_element_type=jnp.float32)
        o = o / denom
        for h in range(GROUP):
            o_ref[pl.ds(r0, BLOCK), h * HEAD_DIM:(h + 1) * HEAD_DIM] = (
                o[h * BLOCK:(h + 1) * BLOCK, :].astype(o_ref.dtype))
        return carry

    lax.fori_loop(0, nb, body, 0, unroll=2)


def _window_mask_bias():
    qi = np.arange(BLOCK)[:, None]
    kj = np.arange(3 * BLOCK)[None, :]
    band = (kj >= qi) & (kj <= qi + 2 * BLOCK)
    first = band & (kj >= BLOCK)
    last = band & (kj < 2 * BLOCK)
    return np.where(np.stack([first, band, last]), 0.0, MASK_NEG).astype(np.float32)


def _attention(proj, sink_l, tables, batch, seq):
    cq, sq, ck, sk = tables
    m = proj.shape[0]
    qw = GROUP * HEAD_DIM
    assert seq // BLOCK >= 2
    bias = jnp.asarray(_window_mask_bias())
    tab_spec = pl.BlockSpec((seq, HEAD_DIM), lambda b, g: (0, 0))
    return pl.pallas_call(
        functools.partial(_attn_kernel, seq=seq),
        grid=(batch, N_KV_HEADS),
        in_specs=[pl.BlockSpec(memory_space=pltpu.SMEM),
                  pl.BlockSpec((seq, qw), lambda b, g: (b, Q_OFF // qw + g)),
                  pl.BlockSpec((seq, HEAD_DIM), lambda b, g: (b, K_OFF // HEAD_DIM + g)),
                  pl.BlockSpec((seq, HEAD_DIM), lambda b, g: (b, V_OFF // HEAD_DIM + g)),
                  tab_spec, tab_spec, tab_spec, tab_spec,
                  pl.BlockSpec(bias.shape, lambda b, g: (0, 0, 0))],
        out_specs=pl.BlockSpec((seq, qw), lambda b, g: (b, g)),
        out_shape=jax.ShapeDtypeStruct((m, ATTN_WIDTH), jnp.bfloat16),
        scratch_shapes=[pltpu.VMEM((seq + 2 * BLOCK, HEAD_DIM), jnp.bfloat16),
                        pltpu.VMEM((seq + 2 * BLOCK, HEAD_DIM), jnp.bfloat16)],
        compiler_params=_params(("parallel", "arbitrary"), 40),
        name="window_gqa",
    )(sink_l, proj, proj, proj, cq, sq, ck, sk, bias)


def _pool_kernel(u_ref, w_ref, scale_ref, o_ref, pad_ref, *, seq):
    g = pl.program_id(1)
    zeros = jnp.zeros((POOL_HALO, POOL_GROUP_DIM), jnp.float32)
    pad_ref[0:POOL_HALO, :] = zeros
    pad_ref[POOL_HALO + seq:, :] = zeros
    pad_ref[POOL_HALO:POOL_HALO + seq, :] = u_ref[...]
    wmat = w_ref[...].astype(jnp.bfloat16)
    scale = scale_ref[...]

    def run(win):
        half = win // 2
        ext = POOL_CHUNK + 2 * POOL_HALO

        def body(c, carry):
            t0 = pl.multiple_of(c * POOL_CHUNK, POOL_CHUNK)
            xe = pad_ref[pl.ds(t0, ext), :]
            a = xe
            step = 1
            while step < win:
                a = a + pltpu.roll(a, step, 0)
                step *= 2
            if half > 1:
                a = pltpu.roll(a, ext - (half - 1), 0)
            wsum = a[POOL_HALO:POOL_HALO + POOL_CHUNK, :]
            x = xe[POOL_HALO:POOL_HALO + POOL_CHUNK, :]
            t = t0 + lax.broadcasted_iota(jnp.int32, (POOL_CHUNK, 1), 0)
            lo = jnp.maximum(t - half, 0)
            hi = jnp.minimum(t + half - 1, seq - 1)
            cnt = (hi - lo + 1).astype(jnp.float32)
            pooled = wsum / cnt - x
            mixed = jnp.dot(pooled.astype(jnp.bfloat16), wmat,
                            preferred_element_type=jnp.float32)
            o_ref[pl.ds(t0, POOL_CHUNK), :] = (mixed * scale).astype(o_ref.dtype)
            return carry

        lax.fori_loop(0, seq // POOL_CHUNK, body, 0)

    for gi, win in enumerate(POOL_WINDOWS):
        pl.when(g == gi)(functools.partial(run, win))


def _pool(proj, pool_w, pool_scale3, layer, batch, seq):
    m = proj.shape[0]
    c = POOL_GROUP_DIM
    return pl.pallas_call(
        functools.partial(_pool_kernel, seq=seq),
        grid=(batch, N_POOL_GROUPS),
        in_specs=[pl.BlockSpec((seq, c), lambda b, g: (b, U_OFF // c + g)),
                  pl.BlockSpec((None, None, c, c), lambda b, g: (layer, g, 0, 0)),
                  pl.BlockSpec((None, 1, c), lambda b, g: (layer, 0, g))],
        out_specs=pl.BlockSpec((seq, c), lambda b, g: (b, g)),
        out_shape=jax.ShapeDtypeStruct((m, POOL_WIDTH), jnp.bfloat16),
        scratch_shapes=[pltpu.VMEM((seq + 2 * POOL_HALO, c), jnp.float32)],
        compiler_params=_params(("parallel", "arbitrary"), 40),
        name="multiscale_pool",
    )(proj, pool_w, pool_scale3)


def _merge_kernel(a_ref, p_ref, wa_ref, wp_ref, ga_ref, gb_ref, o_ref):
    wa = wa_ref[...].astype(jnp.bfloat16)
    wp = wp_ref[...].astype(jnp.bfloat16)
    for r, n in _row_chunks(a_ref.shape[0], MERGE_SUB_M):
        ya = jnp.dot(a_ref[r:r + n, :], wa, preferred_element_type=jnp.float32)
        yp = jnp.dot(p_ref[r:r + n, :], wp, preferred_element_type=jnp.float32)
        merged = (jax.nn.sigmoid(ga_ref[r:r + n, :]) * ya
                  + jax.nn.sigmoid(gb_ref[r:r + n, :]) * yp)
        o_ref[r:r + n, :] = merged.astype(o_ref.dtype)


def _merge(o_attn, mixed, wba, wbp, proj, layer, tm=2048, tn=256):
    m = o_attn.shape[0]
    return pl.pallas_call(
        _merge_kernel,
        grid=(m // tm, D_MODEL // tn),
        in_specs=[_panel_spec(tm, ATTN_WIDTH),
                  _panel_spec(tm, POOL_WIDTH),
                  pl.BlockSpec((None, ATTN_WIDTH, tn), lambda i, j: (layer, 0, j)),
                  pl.BlockSpec((None, POOL_WIDTH, tn), lambda i, j: (layer, 0, j)),
                  pl.BlockSpec((tm, tn), lambda i, j: (i, GA_OFF // tn + j)),
                  pl.BlockSpec((tm, tn), lambda i, j: (i, GB_OFF // tn + j))],
        out_specs=pl.BlockSpec((tm, tn), lambda i, j: (i, j)),
        out_shape=jax.ShapeDtypeStruct((m, D_MODEL), jnp.bfloat16),
        compiler_params=_params(("parallel", "arbitrary"), 48),
        name="branch_merge",
    )(o_attn, mixed, wba, wbp, proj, proj)


def _gate_up_kernel(x_ref, wg_ref, wu_ref, wd_ref, o_ref, wd_bf16_ref):
    wd_bf16_ref[...] = wd_ref[...].astype(jnp.bfloat16)
    wg = wg_ref[...].astype(jnp.bfloat16)
    wu = wu_ref[...].astype(jnp.bfloat16)
    for r, n in _row_chunks(x_ref.shape[0]):
        x = x_ref[r:r + n, :]
        gate = jnp.dot(x, wg, preferred_element_type=jnp.float32)
        up = jnp.dot(x, wu, preferred_element_type=jnp.float32)
        o_ref[r:r + n, :] = (jax.nn.silu(gate) * up).astype(o_ref.dtype)


def _gate_up(h2, w_gate_up, w_down, layer, tm=2048, tn=256):
    m, k = h2.shape
    nt = D_FF // tn
    steps = (m // tm) * nt
    slab = D_FF // steps
    assert slab * steps == D_FF and slab % 16 == 0
    return pl.pallas_call(
        _gate_up_kernel,
        grid=(m // tm, nt),
        in_specs=[_panel_spec(tm, k),
                  pl.BlockSpec((None, k, tn), lambda i, j: (layer, 0, j)),
                  pl.BlockSpec((None, k, tn), lambda i, j: (layer, 0, nt + j)),
                  pl.BlockSpec((None, slab, D_MODEL), lambda i, j: (layer, i * nt + j, 0))],
        out_specs=[pl.BlockSpec((tm, tn), lambda i, j: (i, j)),
                   pl.BlockSpec((slab, D_MODEL), lambda i, j: (i * nt + j, 0))],
        out_shape=[jax.ShapeDtypeStruct((m, D_FF), jnp.bfloat16),
                   jax.ShapeDtypeStruct((D_FF, D_MODEL), jnp.bfloat16)],
        compiler_params=_params(("arbitrary", "arbitrary"), 48),
        name="swiglu_gate_up",
    )(h2, w_gate_up, w_gate_up, w_down)


def _rotary_tables(seq):
    pos = jnp.arange(seq, dtype=jnp.float32)
    inv_freq = 1.0 / jnp.power(jnp.float32(ROPE_THETA),
                               jnp.arange(0, ROT_DIM, 2, dtype=jnp.float32) / ROT_DIM)
    ang = pos[:, None] * inv_freq[None, :]
    cos, sin = jnp.cos(ang), jnp.sin(ang)
    half = ROT_DIM // 2
    ones = jnp.ones((seq, HEAD_DIM - ROT_DIM), jnp.float32)
    zeros = jnp.zeros((seq, HEAD_DIM - ROT_DIM), jnp.float32)
    c_full = jnp.concatenate([cos, cos, ones], axis=-1)
    s_full = jnp.concatenate([-sin, sin, zeros], axis=-1)
    assert half * 2 == ROT_DIM
    scale = HEAD_DIM ** -0.5
    return c_full * scale, s_full * scale, c_full, s_full


def kernel(x, norm1_g, w_in, attn_sink, pool_w, pool_scale, w_branch_attn, w_branch_pool,
           w_out, norm2_g, w_gate_up, w_down, final_norm_g):
    batch, seq, d = x.shape
    m = batch * seq
    bf16 = jnp.bfloat16
    tables = _rotary_tables(seq)
    pool_scale3 = pool_scale.reshape(DEPTH, 1, POOL_WIDTH)

    xs = x.reshape(m, d)
    for l in range(DEPTH):
        h = _rmsnorm(xs, norm1_g[l].reshape(1, d), bf16)
        proj = _matmul(h, w_in, l, jnp.float32, 2048, 512, 56, "in_proj")
        o_attn = _attention(proj, attn_sink[l], tables, batch, seq)
        mixed = _pool(proj, pool_w, pool_scale3, l, batch, seq)
        merged = _merge(o_attn, mixed, w_branch_attn, w_branch_pool, proj, l)
        xs = _matmul(merged, w_out, l, jnp.float32, 2048, 256, 52, "out_proj", residual=xs)
        h2 = _rmsnorm(xs, norm2_g[l].reshape(1, d), bf16)
        act, w_down_b = _gate_up(h2, w_gate_up, w_down, l)
        xs = _matmul(act, w_down_b[None], 0, jnp.float32, 512, 512, 56, "ffn_down",
                     residual=xs, panel_buffers=2)
    out = _rmsnorm(xs, final_norm_g.reshape(1, d), jnp.float32)
    return out.reshape(batch, seq, d)
```

```python
import functools

import jax
import jax.numpy as jnp
import numpy as np
from jax import lax
from jax.experimental import pallas as pl
from jax.experimental.pallas import tpu as pltpu

D_MODEL = 4096
DEPTH = 2
HEAD_DIM = 128
N_HEADS = 16
N_KV_HEADS = 4
GROUP = N_HEADS // N_KV_HEADS
ATTN_WIDTH = N_HEADS * HEAD_DIM
KV_WIDTH = N_KV_HEADS * HEAD_DIM
BLOCK = 128
ROPE_THETA = 500000.0
ROT_DIM = HEAD_DIM // 4
POOL_WIDTH = D_MODEL // 2
POOL_WINDOWS = (2, 4, 8, 16)
N_POOL_GROUPS = len(POOL_WINDOWS)
POOL_GROUP_DIM = POOL_WIDTH // N_POOL_GROUPS
Q_OFF = 0
K_OFF = ATTN_WIDTH
V_OFF = K_OFF + KV_WIDTH
U_OFF = V_OFF + KV_WIDTH
GA_OFF = U_OFF + POOL_WIDTH
GB_OFF = GA_OFF + D_MODEL
IN_WIDTH = GB_OFF + D_MODEL
D_FF = 11008
RMS_EPS = 1e-6
MASK_NEG = -1e30

POOL_HALO = 8
POOL_CHUNK = 128
LANES = 128
SUB_M = 1024
TAIL_M = 256
MERGE_SUB_M = 256
MIB = 1024 * 1024


def _params(semantics, vmem_mib):
    return pltpu.CompilerParams(dimension_semantics=semantics,
                                vmem_limit_bytes=vmem_mib * MIB)


def _rmsnorm_kernel(x_ref, g_ref, o_ref):
    x = x_ref[...]
    ms = jnp.mean(x * x, axis=-1, keepdims=True)
    y = x * lax.rsqrt(ms + RMS_EPS)
    o_ref[...] = (y * g_ref[...]).astype(o_ref.dtype)


def _rmsnorm(x2d, g_row, out_dtype, tr=256):
    m, d = x2d.shape
    return pl.pallas_call(
        _rmsnorm_kernel,
        grid=(m // tr,),
        in_specs=[pl.BlockSpec((tr, d), lambda i: (i, 0)),
                  pl.BlockSpec((1, d), lambda i: (0, 0))],
        out_specs=pl.BlockSpec((tr, d), lambda i: (i, 0)),
        out_shape=jax.ShapeDtypeStruct((m, d), out_dtype),
        compiler_params=_params(("parallel",), 40),
        name="rmsnorm",
    )(x2d, g_row)


def _row_chunks(tm, sub=None):
    if sub is None:
        sizes, left = [], tm
        while left > 0:
            size = min(SUB_M, left)
            while size > TAIL_M and size * 2 > left:
                size //= 2
            sizes.append(size)
            left -= size
    else:
        sizes = [min(tm, sub)] * (tm // min(tm, sub))
    starts = np.cumsum([0] + sizes[:-1])
    assert sum(sizes) == tm
    return [(int(r), int(n)) for r, n in zip(starts, sizes)]


def _row_factor(inv_ref, r, n, width):
    inv = inv_ref[r:r + n, :]
    return jnp.concatenate([inv] * (width // LANES), axis=1)


def _lane_group_sum(y2):
    part = y2[:, 0:LANES]
    for c in range(LANES, y2.shape[1], LANES):
        part = part + y2[:, c:c + LANES]
    return part


def _mm_kernel(*refs, has_residual, has_row_scale, emits_norm):
    refs = list(refs)
    x_ref, w_ref = refs[0], refs[1]
    pos = 2
    r_ref = g_ref = ss_in_ref = None
    if has_residual:
        r_ref = refs[pos]; pos += 1
    if emits_norm:
        g_ref = refs[pos]; pos += 1
    if has_row_scale:
        ss_in_ref = refs[pos]; pos += 1
    o_ref = refs[pos]; pos += 1
    if emits_norm:
        xg_ref, ss_ref = refs[pos], refs[pos + 1]

        @pl.when(pl.program_id(1) == 0)
        def _():
            ss_ref[...] = jnp.zeros(ss_ref.shape, ss_ref.dtype)

    w = w_ref[...].astype(jnp.bfloat16)
    for r, n in _row_chunks(x_ref.shape[0]):
        y = jnp.dot(x_ref[r:r + n, :], w, preferred_element_type=jnp.float32)
        if has_row_scale:
            y = y * _row_factor(ss_in_ref, r, n, y.shape[1])
        if has_residual:
            y = r_ref[r:r + n, :] + y
        o_ref[r:r + n, :] = y.astype(o_ref.dtype)
        if emits_norm:
            xg_ref[r:r + n, :] = (y * g_ref[...]).astype(xg_ref.dtype)
            ss_ref[r:r + n, :] += _lane_group_sum(y * y)

    if emits_norm:
        @pl.when(pl.program_id(1) == pl.num_programs(1) - 1)
        def _():
            total = jnp.sum(ss_ref[...], axis=-1, keepdims=True)
            inv = lax.rsqrt(total * (1.0 / D_MODEL) + RMS_EPS)
            ss_ref[...] = jnp.broadcast_to(inv, ss_ref.shape)


def _panel_spec(tm, k, buffers=1):
    return pl.BlockSpec((tm, k), lambda i, j: (i, 0), pipeline_mode=pl.Buffered(buffers))


def _matmul(x, w_stack, layer, out_dtype, tm, tn, vmem_mib, name, residual=None,
            panel_buffers=1, row_ss=None, next_gain=None):
    m, k = x.shape
    n = w_stack.shape[-1]
    in_specs = [_panel_spec(tm, k, panel_buffers),
                pl.BlockSpec((None, k, tn), lambda i, j: (layer, 0, j))]
    args = [x, w_stack]
    if residual is not None:
        in_specs.append(pl.BlockSpec((tm, tn), lambda i, j: (i, j)))
        args.append(residual)
    if next_gain is not None:
        in_specs.append(pl.BlockSpec((1, tn), lambda i, j: (0, j)))
        args.append(next_gain)
    if row_ss is not None:
        in_specs.append(pl.BlockSpec((tm, LANES), lambda i, j: (i, 0)))
        args.append(row_ss)
    out_specs = pl.BlockSpec((tm, tn), lambda i, j: (i, j))
    out_shape = jax.ShapeDtypeStruct((m, n), out_dtype)
    if next_gain is not None:
        out_specs = [out_specs,
                     pl.BlockSpec((tm, tn), lambda i, j: (i, j)),
                     pl.BlockSpec((tm, LANES), lambda i, j: (i, 0))]
        out_shape = [out_shape,
                     jax.ShapeDtypeStruct((m, n), jnp.bfloat16),
                     jax.ShapeDtypeStruct((m, LANES), jnp.float32)]
    return pl.pallas_call(
        functools.partial(_mm_kernel, has_residual=residual is not None,
                          has_row_scale=row_ss is not None,
                          emits_norm=next_gain is not None),
        grid=(m // tm, n // tn),
        in_specs=in_specs,
        out_specs=out_specs,
        out_shape=out_shape,
        compiler_params=_params(("parallel", "arbitrary"), vmem_mib),
        name=name,
    )(*args)


def _rotary_mix(t, lane):
    half = ROT_DIM // 2
    return jnp.where(lane < half, pltpu.roll(t, HEAD_DIM - half, 1), pltpu.roll(t, half, 1))


def _attn_kernel(sink_ref, q_ref, k_ref, v_ref, cq_ref, sq_ref, ck_ref, sk_ref, bias_ref,
                 o_ref, kpad_ref, vpad_ref, *, seq):
    g = pl.program_id(1)
    nb = seq // BLOCK
    zeros = jnp.zeros((BLOCK, HEAD_DIM), jnp.bfloat16)
    kpad_ref[0:BLOCK, :] = zeros
    kpad_ref[BLOCK + seq:, :] = zeros
    vpad_ref[0:BLOCK, :] = zeros
    vpad_ref[BLOCK + seq:, :] = zeros
    lane_full = lax.broadcasted_iota(jnp.int32, (seq, HEAD_DIM), 1)
    k = k_ref[...]
    k_rot = k * ck_ref[...] + _rotary_mix(k, lane_full) * sk_ref[...]
    kpad_ref[BLOCK:BLOCK + seq, :] = k_rot.astype(jnp.bfloat16)
    vpad_ref[BLOCK:BLOCK + seq, :] = v_ref[...].astype(jnp.bfloat16)

    rows = GROUP * BLOCK
    row_head = lax.broadcasted_iota(jnp.int32, (rows, 1), 0) // BLOCK
    sink = jnp.zeros((rows, 1), jnp.float32)
    for h in range(GROUP):
        sink = jnp.where(row_head == h, sink_ref[g * GROUP + h], sink)

    def body(n, carry):
        r0 = pl.multiple_of(n * BLOCK, BLOCK)
        lane_blk = lax.broadcasted_iota(jnp.int32, (BLOCK, HEAD_DIM), 1)
        cq = cq_ref[pl.ds(r0, BLOCK), :]
        sq = sq_ref[pl.ds(r0, BLOCK), :]
        heads = []
        for h in range(GROUP):
            qh = q_ref[pl.ds(r0, BLOCK), h * HEAD_DIM:(h + 1) * HEAD_DIM]
            heads.append((qh * cq + _rotary_mix(qh, lane_blk) * sq).astype(jnp.bfloat16))
        q = jnp.concatenate(heads, axis=0)
        kw = kpad_ref[pl.ds(r0, 3 * BLOCK), :]
        vw = vpad_ref[pl.ds(r0, 3 * BLOCK), :]
        s = lax.dot_general(q, kw, (((1,), (1,)), ((), ())),
                            preferred_element_type=jnp.float32)
        which = jnp.where(n == 0, 0, jnp.where(n == nb - 1, 2, 1))
        s = s + jnp.concatenate([bias_ref[which]] * GROUP, axis=0)
        m = jnp.maximum(jnp.max(s, axis=-1, keepdims=True), sink)
        p = jnp.exp(s - m)
        denom = jnp.sum(p, axis=-1, keepdims=True) + jnp.exp(sink - m)
        o = jnp.dot(p.astype(jnp.bfloat16), vw, preferred_element_type=jnp.float32)
        o = o / denom
        for h in range(GROUP):
            o_ref[pl.ds(r0, BLOCK), h * HEAD_DIM:(h + 1) * HEAD_DIM] = (
                o[h * BLOCK:(h + 1) * BLOCK, :].astype(o_ref.dtype))
        return carry

    lax.fori_loop(0, nb, body, 0, unroll=2)


def _window_mask_bias():
    qi = np.arange(BLOCK)[:, None]
    kj = np.arange(3 * BLOCK)[None, :]
    band = (kj >= qi) & (kj <= qi + 2 * BLOCK)
    first = band & (kj >= BLOCK)
    last = band & (kj < 2 * BLOCK)
    return np.where(np.stack([first, band, last]), 0.0, MASK_NEG).astype(np.float32)


def _attention(proj, sink_l, tables, batch, seq):
    cq, sq, ck, sk = tables
    m = proj.shape[0]
    qw = GROUP * HEAD_DIM
    assert seq // BLOCK >= 2
    bias = jnp.asarray(_window_mask_bias())
    tab_spec = pl.BlockSpec((seq, HEAD_DIM), lambda b, g: (0, 0))
    return pl.pallas_call(
        functools.partial(_attn_kernel, seq=seq),
        grid=(batch, N_KV_HEADS),
        in_specs=[pl.BlockSpec(memory_space=pltpu.SMEM),
                  pl.BlockSpec((seq, qw), lambda b, g: (b, Q_OFF // qw + g)),
                  pl.BlockSpec((seq, HEAD_DIM), lambda b, g: (b, K_OFF // HEAD_DIM + g)),
                  pl.BlockSpec((seq, HEAD_DIM), lambda b, g: (b, V_OFF // HEAD_DIM + g)),
                  tab_spec, tab_spec, tab_spec, tab_spec,
                  pl.BlockSpec(bias.shape, lambda b, g: (0, 0, 0))],
        out_specs=pl.BlockSpec((seq, qw), lambda b, g: (b, g)),
        out_shape=jax.ShapeDtypeStruct((m, ATTN_WIDTH), jnp.bfloat16),
        scratch_shapes=[pltpu.VMEM((seq + 2 * BLOCK, HEAD_DIM), jnp.bfloat16),
                        pltpu.VMEM((seq + 2 * BLOCK, HEAD_DIM), jnp.bfloat16)],
        compiler_params=_params(("parallel", "arbitrary"), 40),
        name="window_gqa",
    )(sink_l, proj, proj, proj, cq, sq, ck, sk, bias)


def _pool_kernel(u_ref, w_ref, scale_ref, o_ref, pad_ref, *, seq):
    g = pl.program_id(1)
    zeros = jnp.zeros((POOL_HALO, POOL_GROUP_DIM), jnp.float32)
    pad_ref[0:POOL_HALO, :] = zeros
    pad_ref[POOL_HALO + seq:, :] = zeros
    pad_ref[POOL_HALO:POOL_HALO + seq, :] = u_ref[...]
    wmat = w_ref[...].astype(jnp.bfloat16)
    scale = scale_ref[...]

    def run(win):
        half = win // 2
        ext = POOL_CHUNK + 2 * POOL_HALO

        def body(c, carry):
            t0 = pl.multiple_of(c * POOL_CHUNK, POOL_CHUNK)
            xe = pad_ref[pl.ds(t0, ext), :]
            a = xe
            step = 1
            while step < win:
                a = a + pltpu.roll(a, step, 0)
                step *= 2
            if half > 1:
                a = pltpu.roll(a, ext - (half - 1), 0)
            wsum = a[POOL_HALO:POOL_HALO + POOL_CHUNK, :]
            x = xe[POOL_HALO:POOL_HALO + POOL_CHUNK, :]
            t = t0 + lax.broadcasted_iota(jnp.int32, (POOL_CHUNK, 1), 0)
            lo = jnp.maximum(t - half, 0)
            hi = jnp.minimum(t + half - 1, seq - 1)
            cnt = (hi - lo + 1).astype(jnp.float32)
            pooled = wsum / cnt - x
            mixed = jnp.dot(pooled.astype(jnp.bfloat16), wmat,
                            preferred_element_type=jnp.float32)
            o_ref[pl.ds(t0, POOL_CHUNK), :] = (mixed * scale).astype(o_ref.dtype)
            return carry

        lax.fori_loop(0, seq // POOL_CHUNK, body, 0)

    for gi, win in enumerate(POOL_WINDOWS):
        pl.when(g == gi)(functools.partial(run, win))


def _pool(proj, pool_w, pool_scale3, layer, batch, seq):
    m = proj.shape[0]
    c = POOL_GROUP_DIM
    return pl.pallas_call(
        functools.partial(_pool_kernel, seq=seq),
        grid=(batch, N_POOL_GROUPS),
        in_specs=[pl.BlockSpec((seq, c), lambda b, g: (b, U_OFF // c + g)),
                  pl.BlockSpec((None, None, c, c), lambda b, g: (layer, g, 0, 0)),
                  pl.BlockSpec((None, 1, c), lambda b, g: (layer, 0, g))],
        out_specs=pl.BlockSpec((seq, c), lambda b, g: (b, g)),
        out_shape=jax.ShapeDtypeStruct((m, POOL_WIDTH), jnp.bfloat16),
        scratch_shapes=[pltpu.VMEM((seq + 2 * POOL_HALO, c), jnp.float32)],
        compiler_params=_params(("parallel", "arbitrary"), 40),
        name="multiscale_pool",
    )(proj, pool_w, pool_scale3)


def _merge_kernel(a_ref, p_ref, wa_ref, wp_ref, ga_ref, gb_ref, o_ref):
    wa = wa_ref[...].astype(jnp.bfloat16)
    wp = wp_ref[...].astype(jnp.bfloat16)
    for r, n in _row_chunks(a_ref.shape[0], MERGE_SUB_M):
        ya = jnp.dot(a_ref[r:r + n, :], wa, preferred_element_type=jnp.float32)
        yp = jnp.dot(p_ref[r:r + n, :], wp, preferred_element_type=jnp.float32)
        merged = (jax.nn.sigmoid(ga_ref[r:r + n, :]) * ya
                  + jax.nn.sigmoid(gb_ref[r:r + n, :]) * yp)
        o_ref[r:r + n, :] = merged.astype(o_ref.dtype)


def _merge(o_attn, mixed, wba, wbp, proj, layer, tm=2048, tn=256):
    m = o_attn.shape[0]
    return pl.pallas_call(
        _merge_kernel,
        grid=(m // tm, D_MODEL // tn),
        in_specs=[_panel_spec(tm, ATTN_WIDTH),
                  _panel_spec(tm, POOL_WIDTH),
                  pl.BlockSpec((None, ATTN_WIDTH, tn), lambda i, j: (layer, 0, j)),
                  pl.BlockSpec((None, POOL_WIDTH, tn), lambda i, j: (layer, 0, j)),
                  pl.BlockSpec((tm, tn), lambda i, j: (i, GA_OFF // tn + j)),
                  pl.BlockSpec((tm, tn), lambda i, j: (i, GB_OFF // tn + j))],
        out_specs=pl.BlockSpec((tm, tn), lambda i, j: (i, j)),
        out_shape=jax.ShapeDtypeStruct((m, D_MODEL), jnp.bfloat16),
        compiler_params=_params(("parallel", "arbitrary"), 48),
        name="branch_merge",
    )(o_attn, mixed, wba, wbp, proj, proj)


def _gate_up_kernel(x_ref, inv_ref, wg_ref, wu_ref, wd_ref, o_ref, wd_bf16_ref):
    wd_bf16_ref[...] = wd_ref[...].astype(jnp.bfloat16)
    wg = wg_ref[...].astype(jnp.bfloat16)
    wu = wu_ref[...].astype(jnp.bfloat16)
    for r, n in _row_chunks(x_ref.shape[0]):
        x = x_ref[r:r + n, :]
        inv = _row_factor(inv_ref, r, n, wg.shape[1])
        gate = jnp.dot(x, wg, preferred_element_type=jnp.float32) * inv
        up = jnp.dot(x, wu, preferred_element_type=jnp.float32) * inv
        o_ref[r:r + n, :] = (jax.nn.silu(gate) * up).astype(o_ref.dtype)


def _gate_up(xg, row_ss, w_gate_up, w_down, layer, tm=2048, tn=256):
    m, k = xg.shape
    nt = D_FF // tn
    steps = (m // tm) * nt
    slab = D_FF // steps
    assert slab * steps == D_FF and slab % 16 == 0
    return pl.pallas_call(
        _gate_up_kernel,
        grid=(m // tm, nt),
        in_specs=[_panel_spec(tm, k),
                  pl.BlockSpec((tm, LANES), lambda i, j: (i, 0)),
                  pl.BlockSpec((None, k, tn), lambda i, j: (layer, 0, j)),
                  pl.BlockSpec((None, k, tn), lambda i, j: (layer, 0, nt + j)),
                  pl.BlockSpec((None, slab, D_MODEL), lambda i, j: (layer, i * nt + j, 0))],
        out_specs=[pl.BlockSpec((tm, tn), lambda i, j: (i, j)),
                   pl.BlockSpec((slab, D_MODEL), lambda i, j: (i * nt + j, 0))],
        out_shape=[jax.ShapeDtypeStruct((m, D_FF), jnp.bfloat16),
                   jax.ShapeDtypeStruct((D_FF, D_MODEL), jnp.bfloat16)],
        compiler_params=_params(("arbitrary", "arbitrary"), 48),
        name="swiglu_gate_up",
    )(xg, row_ss, w_gate_up, w_gate_up, w_down)


def _rotary_tables(seq):
    pos = jnp.arange(seq, dtype=jnp.float32)
    inv_freq = 1.0 / jnp.power(jnp.float32(ROPE_THETA),
                               jnp.arange(0, ROT_DIM, 2, dtype=jnp.float32) / ROT_DIM)
    ang = pos[:, None] * inv_freq[None, :]
    cos, sin = jnp.cos(ang), jnp.sin(ang)
    half = ROT_DIM // 2
    ones = jnp.ones((seq, HEAD_DIM - ROT_DIM), jnp.float32)
    zeros = jnp.zeros((seq, HEAD_DIM - ROT_DIM), jnp.float32)
    c_full = jnp.concatenate([cos, cos, ones], axis=-1)
    s_full = jnp.concatenate([-sin, sin, zeros], axis=-1)
    assert half * 2 == ROT_DIM
    scale = HEAD_DIM ** -0.5
    return c_full * scale, s_full * scale, c_full, s_full


def kernel(x, norm1_g, w_in, attn_sink, pool_w, pool_scale, w_branch_attn, w_branch_pool,
           w_out, norm2_g, w_gate_up, w_down, final_norm_g):
    batch, seq, d = x.shape
    m = batch * seq
    bf16 = jnp.bfloat16
    tables = _rotary_tables(seq)
    pool_scale3 = pool_scale.reshape(DEPTH, 1, POOL_WIDTH)

    xs = x.reshape(m, d)
    h = _rmsnorm(xs, norm1_g[0].reshape(1, d), bf16)
    row_ss = None
    for l in range(DEPTH):
        proj = _matmul(h, w_in, l, jnp.float32, 2048, 512, 56, "in_proj", row_ss=row_ss)
        o_attn = _attention(proj, attn_sink[l], tables, batch, seq)
        mixed = _pool(proj, pool_w, pool_scale3, l, batch, seq)
        merged = _merge(o_attn, mixed, w_branch_attn, w_branch_pool, proj, l)
        xs, xg, row_ss = _matmul(merged, w_out, l, jnp.float32, 2048, 256, 52, "out_proj",
                                 residual=xs, next_gain=norm2_g[l].reshape(1, d))
        act, w_down_b = _gate_up(xg, row_ss, w_gate_up, w_down, l)
        if l + 1 < DEPTH:
            xs, h, row_ss = _matmul(act, w_down_b[None], 0, jnp.float32, 512, 512, 56,
                                    "ffn_down", residual=xs, panel_buffers=2,
                                    next_gain=norm1_g[l + 1].reshape(1, d))
        else:
            xs = _matmul(act, w_down_b[None], 0, jnp.float32, 512, 512, 56, "ffn_down",
                         residual=xs, panel_buffers=2)
    out = _rmsnorm(xs, final_norm_g.reshape(1, d), jnp.float32)
    return out.reshape(batch, seq, d)
```

```python
import functools

import jax
import jax.numpy as jnp
import numpy as np
from jax import lax
from jax.experimental import pallas as pl
from jax.experimental.pallas import tpu as pltpu

D_MODEL = 4096
DEPTH = 2
HEAD_DIM = 128
N_HEADS = 16
N_KV_HEADS = 4
GROUP = N_HEADS // N_KV_HEADS
ATTN_WIDTH = N_HEADS * HEAD_DIM
KV_WIDTH = N_KV_HEADS * HEAD_DIM
BLOCK = 128
ROPE_THETA = 500000.0
ROT_DIM = HEAD_DIM // 4
POOL_WIDTH = D_MODEL // 2
POOL_WINDOWS = (2, 4, 8, 16)
N_POOL_GROUPS = len(POOL_WINDOWS)
POOL_GROUP_DIM = POOL_WIDTH // N_POOL_GROUPS
Q_OFF = 0
K_OFF = ATTN_WIDTH
V_OFF = K_OFF + KV_WIDTH
U_OFF = V_OFF + KV_WIDTH
GA_OFF = U_OFF + POOL_WIDTH
GB_OFF = GA_OFF + D_MODEL
IN_WIDTH = GB_OFF + D_MODEL
D_FF = 11008
RMS_EPS = 1e-6
MASK_NEG = -1e30

POOL_HALO = 8
POOL_CHUNK = 128
LANES = 128
SUB_M = 1024
TAIL_M = 256
MERGE_SUB_M = 256
MIB = 1024 * 1024


def _params(semantics, vmem_mib):
    return pltpu.CompilerParams(dimension_semantics=semantics,
                                vmem_limit_bytes=vmem_mib * MIB)


def _rmsnorm_kernel(x_ref, g_ref, o_ref):
    x = x_ref[...]
    ms = jnp.mean(x * x, axis=-1, keepdims=True)
    y = x * lax.rsqrt(ms + RMS_EPS)
    o_ref[...] = (y * g_ref[...]).astype(o_ref.dtype)


def _rmsnorm(x2d, g_row, out_dtype, tr=256):
    m, d = x2d.shape
    return pl.pallas_call(
        _rmsnorm_kernel,
        grid=(m // tr,),
        in_specs=[pl.BlockSpec((tr, d), lambda i: (i, 0)),
                  pl.BlockSpec((1, d), lambda i: (0, 0))],
        out_specs=pl.BlockSpec((tr, d), lambda i: (i, 0)),
        out_shape=jax.ShapeDtypeStruct((m, d), out_dtype),
        compiler_params=_params(("parallel",), 40),
        name="rmsnorm",
    )(x2d, g_row)


def _row_chunks(tm, sub=None):
    if sub is None:
        sizes, left = [], tm
        while left > 0:
            size = min(SUB_M, left)
            while size > TAIL_M and size * 2 > left:
                size //= 2
            sizes.append(size)
            left -= size
    else:
        sizes = [min(tm, sub)] * (tm // min(tm, sub))
    starts = np.cumsum([0] + sizes[:-1])
    assert sum(sizes) == tm
    return [(int(r), int(n)) for r, n in zip(starts, sizes)]


def _row_factor(inv_ref, r, n, width):
    inv = inv_ref[r:r + n, :]
    return jnp.concatenate([inv] * (width // LANES), axis=1)


def _lane_group_sum(y2):
    part = y2[:, 0:LANES]
    for c in range(LANES, y2.shape[1], LANES):
        part = part + y2[:, c:c + LANES]
    return part


def _rotary_heads(y, cos_ref, sin_ref, r, n):
    c = cos_ref[r:r + n, :]
    s = sin_ref[r:r + n, :]
    lane = lax.broadcasted_iota(jnp.int32, (n, HEAD_DIM), 1)
    heads = []
    for h in range(0, y.shape[1], HEAD_DIM):
        t = y[:, h:h + HEAD_DIM]
        heads.append(t * c + _rotary_mix(t, lane) * s)
    return jnp.concatenate(heads, axis=1)


def _mm_kernel(*refs, has_residual, has_row_scale, emits_norm, epilogue):
    refs = list(refs)
    x_ref, w_ref = refs[0], refs[1]
    pos = 2
    r_ref = g_ref = ss_in_ref = cos_ref = sin_ref = None
    if has_residual:
        r_ref = refs[pos]; pos += 1
    if emits_norm:
        g_ref = refs[pos]; pos += 1
    if has_row_scale:
        ss_in_ref = refs[pos]; pos += 1
    if epilogue == "rotary":
        cos_ref, sin_ref = refs[pos], refs[pos + 1]; pos += 2
    o_ref = refs[pos]; pos += 1
    if emits_norm:
        xg_ref, ss_ref = refs[pos], refs[pos + 1]

        @pl.when(pl.program_id(1) == 0)
        def _():
            ss_ref[...] = jnp.zeros(ss_ref.shape, ss_ref.dtype)

    w = w_ref[...].astype(jnp.bfloat16)
    for r, n in _row_chunks(x_ref.shape[0]):
        y = jnp.dot(x_ref[r:r + n, :], w, preferred_element_type=jnp.float32)
        if has_row_scale:
            y = y * _row_factor(ss_in_ref, r, n, y.shape[1])
        if has_residual:
            y = r_ref[r:r + n, :] + y
        if epilogue == "sigmoid":
            y = jax.nn.sigmoid(y)
        elif epilogue == "rotary":
            y = _rotary_heads(y, cos_ref, sin_ref, r, n)
        o_ref[r:r + n, :] = y.astype(o_ref.dtype)
        if emits_norm:
            xg_ref[r:r + n, :] = (y * g_ref[...]).astype(xg_ref.dtype)
            ss_ref[r:r + n, :] += _lane_group_sum(y * y)

    if emits_norm:
        @pl.when(pl.program_id(1) == pl.num_programs(1) - 1)
        def _():
            total = jnp.sum(ss_ref[...], axis=-1, keepdims=True)
            inv = lax.rsqrt(total * (1.0 / D_MODEL) + RMS_EPS)
            ss_ref[...] = jnp.broadcast_to(inv, ss_ref.shape)


def _panel_spec(tm, k, buffers=1):
    return pl.BlockSpec((tm, k), lambda i, j: (i, 0), pipeline_mode=pl.Buffered(buffers))


def _matmul(x, w_stack, layer, out_dtype, tm, tn, vmem_mib, name, residual=None,
            panel_buffers=1, row_ss=None, next_gain=None, cols=None, epilogue=None,
            rotary=None):
    m, k = x.shape
    col0, n = cols if cols is not None else (0, w_stack.shape[-1])
    assert col0 % tn == 0 and n % tn == 0
    jt0 = col0 // tn
    in_specs = [_panel_spec(tm, k, panel_buffers),
                pl.BlockSpec((None, k, tn), lambda i, j: (layer, 0, jt0 + j))]
    args = [x, w_stack]
    if residual is not None:
        in_specs.append(pl.BlockSpec((tm, tn), lambda i, j: (i, j)))
        args.append(residual)
    if next_gain is not None:
        in_specs.append(pl.BlockSpec((1, tn), lambda i, j: (0, j)))
        args.append(next_gain)
    if row_ss is not None:
        in_specs.append(pl.BlockSpec((tm, LANES), lambda i, j: (i, 0)))
        args.append(row_ss)
    if rotary is not None:
        assert epilogue == "rotary" and tn == KV_WIDTH
        cos3, sin3, seq = rotary
        assert seq % tm == 0
        q_tiles = ATTN_WIDTH // tn

        def table_index(i, j):
            which = jnp.clip(j - (q_tiles - 1), 0, 2)
            return (which, i % (seq // tm), 0)

        in_specs += [pl.BlockSpec((None, tm, HEAD_DIM), table_index)] * 2
        args += [cos3, sin3]
    out_specs = pl.BlockSpec((tm, tn), lambda i, j: (i, j))
    out_shape = jax.ShapeDtypeStruct((m, n), out_dtype)
    if next_gain is not None:
        out_specs = [out_specs,
                     pl.BlockSpec((tm, tn), lambda i, j: (i, j)),
                     pl.BlockSpec((tm, LANES), lambda i, j: (i, 0))]
        out_shape = [out_shape,
                     jax.ShapeDtypeStruct((m, n), jnp.bfloat16),
                     jax.ShapeDtypeStruct((m, LANES), jnp.float32)]
    return pl.pallas_call(
        functools.partial(_mm_kernel, has_residual=residual is not None,
                          has_row_scale=row_ss is not None,
                          emits_norm=next_gain is not None, epilogue=epilogue),
        grid=(m // tm, n // tn),
        in_specs=in_specs,
        out_specs=out_specs,
        out_shape=out_shape,
        compiler_params=_params(("parallel", "arbitrary"), vmem_mib),
        name=name,
    )(*args)


def _rotary_mix(t, lane):
    half = ROT_DIM // 2
    return jnp.where(lane < half, pltpu.roll(t, HEAD_DIM - half, 1), pltpu.roll(t, half, 1))


def _attn_kernel(sink_ref, q_ref, k_ref, v_ref, bias_ref, o_ref, kpad_ref, vpad_ref, *, seq):
    g = pl.program_id(1)
    nb = seq // BLOCK
    zeros = jnp.zeros((BLOCK, HEAD_DIM), jnp.bfloat16)
    kpad_ref[0:BLOCK, :] = zeros
    kpad_ref[BLOCK + seq:, :] = zeros
    vpad_ref[0:BLOCK, :] = zeros
    vpad_ref[BLOCK + seq:, :] = zeros
    kpad_ref[BLOCK:BLOCK + seq, :] = k_ref[...]
    vpad_ref[BLOCK:BLOCK + seq, :] = v_ref[...]

    rows = GROUP * BLOCK
    row_head = lax.broadcasted_iota(jnp.int32, (rows, 1), 0) // BLOCK
    sink = jnp.zeros((rows, 1), jnp.float32)
    for h in range(GROUP):
        sink = jnp.where(row_head == h, sink_ref[g * GROUP + h], sink)

    def body(n, carry):
        r0 = pl.multiple_of(n * BLOCK, BLOCK)
        q = jnp.concatenate(
            [q_ref[pl.ds(r0, BLOCK), h * HEAD_DIM:(h + 1) * HEAD_DIM] for h in range(GROUP)],
            axis=0)
        kw = kpad_ref[pl.ds(r0, 3 * BLOCK), :]
        vw = vpad_ref[pl.ds(r0, 3 * BLOCK), :]
        s = lax.dot_general(q, kw, (((1,), (1,)), ((), ())),
                            preferred_element_type=jnp.float32)
        which = jnp.where(n == 0, 0, jnp.where(n == nb - 1, 2, 1))
        s = s + jnp.concatenate([bias_ref[which]] * GROUP, axis=0)
        m = jnp.maximum(jnp.max(s, axis=-1, keepdims=True), sink)
        p = jnp.exp(s - m)
        denom = jnp.sum(p, axis=-1, keepdims=True) + jnp.exp(sink - m)
        o = jnp.dot(p.astype(jnp.bfloat16), vw, preferred_element_type=jnp.float32)
        o = o / denom
        for h in range(GROUP):
            o_ref[pl.ds(r0, BLOCK), h * HEAD_DIM:(h + 1) * HEAD_DIM] = (
                o[h * BLOCK:(h + 1) * BLOCK, :].astype(o_ref.dtype))
        return carry

    lax.fori_loop(0, nb, body, 0, unroll=2)


def _window_mask_bias():
    qi = np.arange(BLOCK)[:, None]
    kj = np.arange(3 * BLOCK)[None, :]
    band = (kj >= qi) & (kj <= qi + 2 * BLOCK)
    first = band & (kj >= BLOCK)
    last = band & (kj < 2 * BLOCK)
    return np.where(np.stack([first, band, last]), 0.0, MASK_NEG).astype(np.float32)


def _attention(qkv, sink_l, batch, seq):
    m = qkv.shape[0]
    qw = GROUP * HEAD_DIM
    assert seq // BLOCK >= 2
    bias = jnp.asarray(_window_mask_bias())
    return pl.pallas_call(
        functools.partial(_attn_kernel, seq=seq),
        grid=(batch, N_KV_HEADS),
        in_specs=[pl.BlockSpec(memory_space=pltpu.SMEM),
                  pl.BlockSpec((seq, qw), lambda b, g: (b, Q_OFF // qw + g)),
                  pl.BlockSpec((seq, HEAD_DIM), lambda b, g: (b, K_OFF // HEAD_DIM + g)),
                  pl.BlockSpec((seq, HEAD_DIM), lambda b, g: (b, V_OFF // HEAD_DIM + g)),
                  pl.BlockSpec(bias.shape, lambda b, g: (0, 0, 0))],
        out_specs=pl.BlockSpec((seq, qw), lambda b, g: (b, g)),
        out_shape=jax.ShapeDtypeStruct((m, ATTN_WIDTH), jnp.bfloat16),
        scratch_shapes=[pltpu.VMEM((seq + 2 * BLOCK, HEAD_DIM), jnp.bfloat16),
                        pltpu.VMEM((seq + 2 * BLOCK, HEAD_DIM), jnp.bfloat16)],
        compiler_params=_params(("parallel", "arbitrary"), 40),
        name="window_gqa",
    )(sink_l, qkv, qkv, qkv, bias)


def _pool_kernel(u_ref, w_ref, scale_ref, o_ref, pad_ref, *, seq):
    g = pl.program_id(1)
    zeros = jnp.zeros((POOL_HALO, POOL_GROUP_DIM), jnp.float32)
    pad_ref[0:POOL_HALO, :] = zeros
    pad_ref[POOL_HALO + seq:, :] = zeros
    pad_ref[POOL_HALO:POOL_HALO + seq, :] = u_ref[...]
    wmat = w_ref[...].astype(jnp.bfloat16)
    scale = scale_ref[...]

    def run(win):
        half = win // 2
        ext = POOL_CHUNK + 2 * POOL_HALO

        def body(c, carry):
            t0 = pl.multiple_of(c * POOL_CHUNK, POOL_CHUNK)
            xe = pad_ref[pl.ds(t0, ext), :]
            a = xe
            step = 1
            while step < win:
                a = a + pltpu.roll(a, step, 0)
                step *= 2
            if half > 1:
                a = pltpu.roll(a, ext - (half - 1), 0)
            wsum = a[POOL_HALO:POOL_HALO + POOL_CHUNK, :]
            x = xe[POOL_HALO:POOL_HALO + POOL_CHUNK, :]
            t = t0 + lax.broadcasted_iota(jnp.int32, (POOL_CHUNK, 1), 0)
            lo = jnp.maximum(t - half, 0)
            hi = jnp.minimum(t + half - 1, seq - 1)
            cnt = (hi - lo + 1).astype(jnp.float32)
            pooled = wsum / cnt - x
            mixed = jnp.dot(pooled.astype(jnp.bfloat16), wmat,
                            preferred_element_type=jnp.float32)
            o_ref[pl.ds(t0, POOL_CHUNK), :] = (mixed * scale).astype(o_ref.dtype)
            return carry

        lax.fori_loop(0, seq // POOL_CHUNK, body, 0)

    for gi, win in enumerate(POOL_WINDOWS):
        pl.when(g == gi)(functools.partial(run, win))


def _pool(u, pool_w, pool_scale3, layer, batch, seq):
    m = u.shape[0]
    c = POOL_GROUP_DIM
    return pl.pallas_call(
        functools.partial(_pool_kernel, seq=seq),
        grid=(batch, N_POOL_GROUPS),
        in_specs=[pl.BlockSpec((seq, c), lambda b, g: (b, g)),
                  pl.BlockSpec((None, None, c, c), lambda b, g: (layer, g, 0, 0)),
                  pl.BlockSpec((None, 1, c), lambda b, g: (layer, 0, g))],
        out_specs=pl.BlockSpec((seq, c), lambda b, g: (b, g)),
        out_shape=jax.ShapeDtypeStruct((m, POOL_WIDTH), jnp.bfloat16),
        scratch_shapes=[pltpu.VMEM((seq + 2 * POOL_HALO, c), jnp.float32)],
        compiler_params=_params(("parallel", "arbitrary"), 40),
        name="multiscale_pool",
    )(u, pool_w, pool_scale3)


def _merge_kernel(a_ref, p_ref, wa_ref, wp_ref, ga_ref, gb_ref, o_ref):
    wa = wa_ref[...].astype(jnp.bfloat16)
    wp = wp_ref[...].astype(jnp.bfloat16)
    for r, n in _row_chunks(a_ref.shape[0], MERGE_SUB_M):
        ya = jnp.dot(a_ref[r:r + n, :], wa, preferred_element_type=jnp.float32)
        yp = jnp.dot(p_ref[r:r + n, :], wp, preferred_element_type=jnp.float32)
        merged = (ga_ref[r:r + n, :].astype(jnp.float32) * ya
                  + gb_ref[r:r + n, :].astype(jnp.float32) * yp)
        o_ref[r:r + n, :] = merged.astype(o_ref.dtype)


def _merge(o_attn, mixed, wba, wbp, gates, layer, tm=2048, tn=512):
    m = o_attn.shape[0]
    return pl.pallas_call(
        _merge_kernel,
        grid=(m // tm, D_MODEL // tn),
        in_specs=[_panel_spec(tm, ATTN_WIDTH),
                  _panel_spec(tm, POOL_WIDTH),
                  pl.BlockSpec((None, ATTN_WIDTH, tn), lambda i, j: (layer, 0, j)),
                  pl.BlockSpec((None, POOL_WIDTH, tn), lambda i, j: (layer, 0, j)),
                  pl.BlockSpec((tm, tn), lambda i, j: (i, j)),
                  pl.BlockSpec((tm, tn), lambda i, j: (i, D_MODEL // tn + j))],
        out_specs=pl.BlockSpec((tm, tn), lambda i, j: (i, j)),
        out_shape=jax.ShapeDtypeStruct((m, D_MODEL), jnp.bfloat16),
        compiler_params=_params(("parallel", "arbitrary"), 56),
        name="branch_merge",
    )(o_attn, mixed, wba, wbp, gates, gates)


def _gate_up_kernel(x_ref, inv_ref, wg_ref, wu_ref, wd_ref, o_ref, wd_bf16_ref):
    wd_bf16_ref[...] = wd_ref[...].astype(jnp.bfloat16)
    wg = wg_ref[...].astype(jnp.bfloat16)
    wu = wu_ref[...].astype(jnp.bfloat16)
    for r, n in _row_chunks(x_ref.shape[0]):
        x = x_ref[r:r + n, :]
        inv = _row_factor(inv_ref, r, n, wg.shape[1])
        gate = jnp.dot(x, wg, preferred_element_type=jnp.float32) * inv
        up = jnp.dot(x, wu, preferred_element_type=jnp.float32) * inv
        o_ref[r:r + n, :] = (jax.nn.silu(gate) * up).astype(o_ref.dtype)


def _gate_up(xg, row_ss, w_gate_up, w_down, layer, tm=2048, tn=256):
    m, k = xg.shape
    nt = D_FF // tn
    steps = (m // tm) * nt
    slab = D_FF // steps
    assert slab * steps == D_FF and slab % 16 == 0
    return pl.pallas_call(
        _gate_up_kernel,
        grid=(m // tm, nt),
        in_specs=[_panel_spec(tm, k),
                  pl.BlockSpec((tm, LANES), lambda i, j: (i, 0)),
                  pl.BlockSpec((None, k, tn), lambda i, j: (layer, 0, j)),
                  pl.BlockSpec((None, k, tn), lambda i, j: (layer, 0, nt + j)),
                  pl.BlockSpec((None, slab, D_MODEL), lambda i, j: (layer, i * nt + j, 0))],
        out_specs=[pl.BlockSpec((tm, tn), lambda i, j: (i, j)),
                   pl.BlockSpec((slab, D_MODEL), lambda i, j: (i * nt + j, 0))],
        out_shape=[jax.ShapeDtypeStruct((m, D_FF), jnp.bfloat16),
                   jax.ShapeDtypeStruct((D_FF, D_MODEL), jnp.bfloat16)],
        compiler_params=_params(("arbitrary", "arbitrary"), 48),
        name="swiglu_gate_up",
    )(xg, row_ss, w_gate_up, w_gate_up, w_down)


def _rotary_tables(seq):
    pos = jnp.arange(seq, dtype=jnp.float32)
    inv_freq = 1.0 / jnp.power(jnp.float32(ROPE_THETA),
                               jnp.arange(0, ROT_DIM, 2, dtype=jnp.float32) / ROT_DIM)
    ang = pos[:, None] * inv_freq[None, :]
    cos, sin = jnp.cos(ang), jnp.sin(ang)
    ones = jnp.ones((seq, HEAD_DIM - ROT_DIM), jnp.float32)
    zeros = jnp.zeros((seq, HEAD_DIM - ROT_DIM), jnp.float32)
    c_full = jnp.concatenate([cos, cos, ones], axis=-1)
    s_full = jnp.concatenate([-sin, sin, zeros], axis=-1)
    scale = HEAD_DIM ** -0.5
    cos3 = jnp.stack([c_full * scale, c_full, jnp.ones_like(c_full)])
    sin3 = jnp.stack([s_full * scale, s_full, jnp.zeros_like(s_full)])
    return cos3, sin3


def kernel(x, norm1_g, w_in, attn_sink, pool_w, pool_scale, w_branch_attn, w_branch_pool,
           w_out, norm2_g, w_gate_up, w_down, final_norm_g):
    batch, seq, d = x.shape
    m = batch * seq
    bf16 = jnp.bfloat16
    cos3, sin3 = _rotary_tables(seq)
    pool_scale3 = pool_scale.reshape(DEPTH, 1, POOL_WIDTH)

    xs = x.reshape(m, d)
    h = _rmsnorm(xs, norm1_g[0].reshape(1, d), bf16)
    row_ss = None
    for l in range(DEPTH):
        qkv = _matmul(h, w_in, l, bf16, seq, 512, 56, "in_proj_qkv", row_ss=row_ss,
                      cols=(Q_OFF, U_OFF - Q_OFF), epilogue="rotary",
                      rotary=(cos3, sin3, seq))
        u = _matmul(h, w_in, l, jnp.float32, 2048, 512, 56, "in_proj_pool", row_ss=row_ss,
                    cols=(U_OFF, POOL_WIDTH))
        gates = _matmul(h, w_in, l, bf16, 2048, 512, 56, "in_proj_gates", row_ss=row_ss,
                        cols=(GA_OFF, 2 * D_MODEL), epilogue="sigmoid")
        o_attn = _attention(qkv, attn_sink[l], batch, seq)
        mixed = _pool(u, pool_w, pool_scale3, l, batch, seq)
        merged = _merge(o_attn, mixed, w_branch_attn, w_branch_pool, gates, l)
        xs, xg, row_ss = _matmul(merged, w_out, l, jnp.float32, 2048, 256, 52, "out_proj",
                                 residual=xs, next_gain=norm2_g[l].reshape(1, d))
        act, w_down_b = _gate_up(xg, row_ss, w_gate_up, w_down, l)
        if l + 1 < DEPTH:
            xs, h, row_ss = _matmul(act, w_down_b[None], 0, jnp.float32, 512, 512, 56,
                                    "ffn_down", residual=xs, panel_buffers=2,
                                    next_gain=norm1_g[l + 1].reshape(1, d))
        else:
            xs = _matmul(act, w_down_b[None], 0, jnp.float32, 512, 512, 56, "ffn_down",
                         residual=xs, panel_buffers=2)
    out = _rmsnorm(xs, final_norm_g.reshape(1, d), jnp.float32)
    return out.reshape(batch, seq, d)
```

```python
import functools

import jax
import jax.numpy as jnp
import numpy as np
from jax import lax
from jax.experimental import pallas as pl
from jax.experimental.pallas import tpu as pltpu

D_MODEL = 4096
DEPTH = 2
HEAD_DIM = 128
N_HEADS = 16
N_KV_HEADS = 4
GROUP = N_HEADS // N_KV_HEADS
ATTN_WIDTH = N_HEADS * HEAD_DIM
KV_WIDTH = N_KV_HEADS * HEAD_DIM
BLOCK = 128
ROPE_THETA = 500000.0
ROT_DIM = HEAD_DIM // 4
POOL_WIDTH = D_MODEL // 2
POOL_WINDOWS = (2, 4, 8, 16)
N_POOL_GROUPS = len(POOL_WINDOWS)
POOL_GROUP_DIM = POOL_WIDTH // N_POOL_GROUPS
Q_OFF = 0
K_OFF = ATTN_WIDTH
V_OFF = K_OFF + KV_WIDTH
U_OFF = V_OFF + KV_WIDTH
GA_OFF = U_OFF + POOL_WIDTH
GB_OFF = GA_OFF + D_MODEL
IN_WIDTH = GB_OFF + D_MODEL
D_FF = 11008
RMS_EPS = 1e-6
MASK_NEG = -1e30

POOL_HALO = 8
POOL_CHUNK = 128
LANES = 128
SUB_M = 1024
TAIL_M = 256
MERGE_SUB_M = 256
MIB = 1024 * 1024


def _params(semantics, vmem_mib):
    return pltpu.CompilerParams(dimension_semantics=semantics,
                                vmem_limit_bytes=vmem_mib * MIB)


def _rmsnorm_kernel(x_ref, g_ref, o_ref):
    x = x_ref[...]
    ms = jnp.mean(x * x, axis=-1, keepdims=True)
    y = x * lax.rsqrt(ms + RMS_EPS)
    o_ref[...] = (y * g_ref[...]).astype(o_ref.dtype)


def _rmsnorm(x2d, g_row, out_dtype, tr=256):
    m, d = x2d.shape
    return pl.pallas_call(
        _rmsnorm_kernel,
        grid=(m // tr,),
        in_specs=[pl.BlockSpec((tr, d), lambda i: (i, 0)),
                  pl.BlockSpec((1, d), lambda i: (0, 0))],
        out_specs=pl.BlockSpec((tr, d), lambda i: (i, 0)),
        out_shape=jax.ShapeDtypeStruct((m, d), out_dtype),
        compiler_params=_params(("parallel",), 40),
        name="rmsnorm",
    )(x2d, g_row)


def _row_chunks(tm, sub=None):
    if sub is None:
        sizes, left = [], tm
        while left > 0:
            size = min(SUB_M, left)
            while size > TAIL_M and size * 2 > left:
                size //= 2
            sizes.append(size)
            left -= size
    else:
        sizes = [min(tm, sub)] * (tm // min(tm, sub))
    starts = np.cumsum([0] + sizes[:-1])
    assert sum(sizes) == tm
    return [(int(r), int(n)) for r, n in zip(starts, sizes)]


def _row_factor(inv_ref, r, n, width):
    inv = inv_ref[r:r + n, :]
    return jnp.concatenate([inv] * (width // LANES), axis=1)


def _lane_group_sum(y2):
    part = y2[:, 0:LANES]
    for c in range(LANES, y2.shape[1], LANES):
        part = part + y2[:, c:c + LANES]
    return part


def _mm_kernel(*refs, has_residual, has_row_scale, emits_norm):
    refs = list(refs)
    x_ref, w_ref = refs[0], refs[1]
    pos = 2
    r_ref = g_ref = inv_in_ref = None
    if has_residual:
        r_ref = refs[pos]; pos += 1
    if emits_norm:
        g_ref = refs[pos]; pos += 1
    if has_row_scale:
        inv_in_ref = refs[pos]; pos += 1
    o_ref = refs[pos]; pos += 1
    if emits_norm:
        xg_ref, ss_ref = refs[pos], refs[pos + 1]

        @pl.when(pl.program_id(1) == 0)
        def _():
            ss_ref[...] = jnp.zeros(ss_ref.shape, ss_ref.dtype)

    w = w_ref[...].astype(jnp.bfloat16)
    for r, n in _row_chunks(x_ref.shape[0]):
        y = jnp.dot(x_ref[r:r + n, :], w, preferred_element_type=jnp.float32)
        if has_row_scale:
            y = y * _row_factor(inv_in_ref, r, n, y.shape[1])
        if has_residual:
            y = r_ref[r:r + n, :] + y
        o_ref[r:r + n, :] = y.astype(o_ref.dtype)
        if emits_norm:
            xg_ref[r:r + n, :] = (y * g_ref[...]).astype(xg_ref.dtype)
            ss_ref[r:r + n, :] += _lane_group_sum(y * y)

    if emits_norm:
        @pl.when(pl.program_id(1) == pl.num_programs(1) - 1)
        def _():
            total = jnp.sum(ss_ref[...], axis=-1, keepdims=True)
            inv = lax.rsqrt(total * (1.0 / D_MODEL) + RMS_EPS)
            ss_ref[...] = jnp.broadcast_to(inv, ss_ref.shape)


def _panel_spec(tm, k, buffers=1):
    return pl.BlockSpec((tm, k), lambda i, j: (i, 0), pipeline_mode=pl.Buffered(buffers))


def _matmul(x, w_stack, layer, out_dtype, tm, tn, vmem_mib, name, residual=None,
            panel_buffers=1, row_inv=None, next_gain=None):
    m, k = x.shape
    n = w_stack.shape[-1]
    in_specs = [_panel_spec(tm, k, panel_buffers),
                pl.BlockSpec((None, k, tn), lambda i, j: (layer, 0, j))]
    args = [x, w_stack]
    if residual is not None:
        in_specs.append(pl.BlockSpec((tm, tn), lambda i, j: (i, j)))
        args.append(residual)
    if next_gain is not None:
        in_specs.append(pl.BlockSpec((1, tn), lambda i, j: (0, j)))
        args.append(next_gain)
    if row_inv is not None:
        in_specs.append(pl.BlockSpec((tm, LANES), lambda i, j: (i, 0)))
        args.append(row_inv)
    out_specs = pl.BlockSpec((tm, tn), lambda i, j: (i, j))
    out_shape = jax.ShapeDtypeStruct((m, n), out_dtype)
    if next_gain is not None:
        out_specs = [out_specs,
                     pl.BlockSpec((tm, tn), lambda i, j: (i, j)),
                     pl.BlockSpec((tm, LANES), lambda i, j: (i, 0))]
        out_shape = [out_shape,
                     jax.ShapeDtypeStruct((m, n), jnp.bfloat16),
                     jax.ShapeDtypeStruct((m, LANES), jnp.float32)]
    return pl.pallas_call(
        functools.partial(_mm_kernel, has_residual=residual is not None,
                          has_row_scale=row_inv is not None,
                          emits_norm=next_gain is not None),
        grid=(m // tm, n // tn),
        in_specs=in_specs,
        out_specs=out_specs,
        out_shape=out_shape,
        compiler_params=_params(("parallel", "arbitrary"), vmem_mib),
        name=name,
    )(*args)


def _rotary_mix(t, lane):
    half = ROT_DIM // 2
    return jnp.where(lane < half, pltpu.roll(t, HEAD_DIM - half, 1), pltpu.roll(t, half, 1))


def _attn_kernel(sink_ref, q_ref, k_ref, v_ref, cq_ref, sq_ref, ck_ref, sk_ref, bias_ref,
                 o_ref, kpad_ref, vpad_ref, *, seq):
    g = pl.program_id(1)
    nb = seq // BLOCK
    zeros = jnp.zeros((BLOCK, HEAD_DIM), jnp.bfloat16)
    kpad_ref[0:BLOCK, :] = zeros
    kpad_ref[BLOCK + seq:, :] = zeros
    vpad_ref[0:BLOCK, :] = zeros
    vpad_ref[BLOCK + seq:, :] = zeros
    lane_full = lax.broadcasted_iota(jnp.int32, (seq, HEAD_DIM), 1)
    k = k_ref[...]
    k_rot = k * ck_ref[...] + _rotary_mix(k, lane_full) * sk_ref[...]
    kpad_ref[BLOCK:BLOCK + seq, :] = k_rot.astype(jnp.bfloat16)
    vpad_ref[BLOCK:BLOCK + seq, :] = v_ref[...].astype(jnp.bfloat16)

    rows = GROUP * BLOCK
    row_head = lax.broadcasted_iota(jnp.int32, (rows, 1), 0) // BLOCK
    sink = jnp.zeros((rows, 1), jnp.float32)
    for h in range(GROUP):
        sink = jnp.where(row_head == h, sink_ref[g * GROUP + h], sink)

    def body(n, carry):
        r0 = pl.multiple_of(n * BLOCK, BLOCK)
        lane_blk = lax.broadcasted_iota(jnp.int32, (BLOCK, HEAD_DIM), 1)
        cq = cq_ref[pl.ds(r0, BLOCK), :]
        sq = sq_ref[pl.ds(r0, BLOCK), :]
        heads = []
        for h in range(GROUP):
            qh = q_ref[pl.ds(r0, BLOCK), h * HEAD_DIM:(h + 1) * HEAD_DIM]
            heads.append((qh * cq + _rotary_mix(qh, lane_blk) * sq).astype(jnp.bfloat16))
        q = jnp.concatenate(heads, axis=0)
        kw = kpad_ref[pl.ds(r0, 3 * BLOCK), :]
        vw = vpad_ref[pl.ds(r0, 3 * BLOCK), :]
        s = lax.dot_general(q, kw, (((1,), (1,)), ((), ())),
                            preferred_element_type=jnp.float32)
        which = jnp.where(n == 0, 0, jnp.where(n == nb - 1, 2, 1))
        s = s + jnp.concatenate([bias_ref[which]] * GROUP, axis=0)
        m = jnp.maximum(jnp.max(s, axis=-1, keepdims=True), sink)
        p = jnp.exp(s - m)
        denom = jnp.sum(p, axis=-1, keepdims=True) + jnp.exp(sink - m)
        o = jnp.dot(p.astype(jnp.bfloat16), vw, preferred_element_type=jnp.float32)
        o = o / denom
        for h in range(GROUP):
            o_ref[pl.ds(r0, BLOCK), h * HEAD_DIM:(h + 1) * HEAD_DIM] = (
                o[h * BLOCK:(h + 1) * BLOCK, :].astype(o_ref.dtype))
        return carry

    lax.fori_loop(0, nb, body, 0, unroll=2)


def _window_mask_bias():
    qi = np.arange(BLOCK)[:, None]
    kj = np.arange(3 * BLOCK)[None, :]
    band = (kj >= qi) & (kj <= qi + 2 * BLOCK)
    first = band & (kj >= BLOCK)
    last = band & (kj < 2 * BLOCK)
    return np.where(np.stack([first, band, last]), 0.0, MASK_NEG).astype(np.float32)


def _attention(proj, sink_l, tables, batch, seq):
    cq, sq, ck, sk = tables
    m = proj.shape[0]
    qw = GROUP * HEAD_DIM
    assert seq // BLOCK >= 2
    bias = jnp.asarray(_window_mask_bias())
    tab_spec = pl.BlockSpec((seq, HEAD_DIM), lambda b, g: (0, 0))
    return pl.pallas_call(
        functools.partial(_attn_kernel, seq=seq),
        grid=(batch, N_KV_HEADS),
        in_specs=[pl.BlockSpec(memory_space=pltpu.SMEM),
                  pl.BlockSpec((seq, qw), lambda b, g: (b, Q_OFF // qw + g)),
                  pl.BlockSpec((seq, HEAD_DIM), lambda b, g: (b, K_OFF // HEAD_DIM + g)),
                  pl.BlockSpec((seq, HEAD_DIM), lambda b, g: (b, V_OFF // HEAD_DIM + g)),
                  tab_spec, tab_spec, tab_spec, tab_spec,
                  pl.BlockSpec(bias.shape, lambda b, g: (0, 0, 0))],
        out_specs=pl.BlockSpec((seq, qw), lambda b, g: (b, g)),
        out_shape=jax.ShapeDtypeStruct((m, ATTN_WIDTH), jnp.bfloat16),
        scratch_shapes=[pltpu.VMEM((seq + 2 * BLOCK, HEAD_DIM), jnp.bfloat16),
                        pltpu.VMEM((seq + 2 * BLOCK, HEAD_DIM), jnp.bfloat16)],
        compiler_params=_params(("parallel", "arbitrary"), 40),
        name="window_gqa",
    )(sink_l, proj, proj, proj, cq, sq, ck, sk, bias)


def _pool_kernel(u_ref, w_ref, scale_ref, o_ref, pad_ref, *, seq):
    g = pl.program_id(1)
    zeros = jnp.zeros((POOL_HALO, POOL_GROUP_DIM), jnp.float32)
    pad_ref[0:POOL_HALO, :] = zeros
    pad_ref[POOL_HALO + seq:, :] = zeros
    pad_ref[POOL_HALO:POOL_HALO + seq, :] = u_ref[...]
    wmat = w_ref[...].astype(jnp.bfloat16)
    scale = scale_ref[...]

    def run(win):
        half = win // 2
        ext = POOL_CHUNK + 2 * POOL_HALO

        def body(c, carry):
            t0 = pl.multiple_of(c * POOL_CHUNK, POOL_CHUNK)
            xe = pad_ref[pl.ds(t0, ext), :]
            a = xe
            step = 1
            while step < half:
                a = a + pltpu.roll(a, ext - step, 0)
                step *= 2
            a = a + pltpu.roll(a, half, 0)
            wsum = a[POOL_HALO:POOL_HALO + POOL_CHUNK, :]
            x = xe[POOL_HALO:POOL_HALO + POOL_CHUNK, :]
            t = t0 + lax.broadcasted_iota(jnp.int32, (POOL_CHUNK, 1), 0)
            lo = jnp.maximum(t - half, 0)
            hi = jnp.minimum(t + half - 1, seq - 1)
            cnt = (hi - lo + 1).astype(jnp.float32)
            pooled = wsum / cnt - x
            mixed = jnp.dot(pooled.astype(jnp.bfloat16), wmat,
                            preferred_element_type=jnp.float32)
            o_ref[pl.ds(t0, POOL_CHUNK), :] = (mixed * scale).astype(o_ref.dtype)
            return carry

        lax.fori_loop(0, seq // POOL_CHUNK, body, 0, unroll=2)

    for gi, win in enumerate(POOL_WINDOWS):
        pl.when(g == gi)(functools.partial(run, win))


def _pool(proj, pool_w, pool_scale3, layer, batch, seq):
    m = proj.shape[0]
    c = POOL_GROUP_DIM
    return pl.pallas_call(
        functools.partial(_pool_kernel, seq=seq),
        grid=(batch, N_POOL_GROUPS),
        in_specs=[pl.BlockSpec((seq, c), lambda b, g: (b, U_OFF // c + g)),
                  pl.BlockSpec((None, None, c, c), lambda b, g: (layer, g, 0, 0)),
                  pl.BlockSpec((None, 1, c), lambda b, g: (layer, 0, g))],
        out_specs=pl.BlockSpec((seq, c), lambda b, g: (b, g)),
        out_shape=jax.ShapeDtypeStruct((m, POOL_WIDTH), jnp.bfloat16),
        scratch_shapes=[pltpu.VMEM((seq + 2 * POOL_HALO, c), jnp.float32)],
        compiler_params=_params(("parallel", "arbitrary"), 40),
        name="multiscale_pool",
    )(proj, pool_w, pool_scale3)


def _merge_kernel(a_ref, p_ref, wa_ref, wp_ref, ga_ref, gb_ref, o_ref):
    wa = wa_ref[...].astype(jnp.bfloat16)
    wp = wp_ref[...].astype(jnp.bfloat16)
    for r, n in _row_chunks(a_ref.shape[0], MERGE_SUB_M):
        ya = jnp.dot(a_ref[r:r + n, :], wa, preferred_element_type=jnp.float32)
        yp = jnp.dot(p_ref[r:r + n, :], wp, preferred_element_type=jnp.float32)
        merged = (jax.nn.sigmoid(ga_ref[r:r + n, :]) * ya
                  + jax.nn.sigmoid(gb_ref[r:r + n, :]) * yp)
        o_ref[r:r + n, :] = merged.astype(o_ref.dtype)


def _merge(o_attn, mixed, wba, wbp, proj, layer, tm=2048, tn=256):
    m = o_attn.shape[0]
    return pl.pallas_call(
        _merge_kernel,
        grid=(m // tm, D_MODEL // tn),
        in_specs=[_panel_spec(tm, ATTN_WIDTH),
                  _panel_spec(tm, POOL_WIDTH),
                  pl.BlockSpec((None, ATTN_WIDTH, tn), lambda i, j: (layer, 0, j)),
                  pl.BlockSpec((None, POOL_WIDTH, tn), lambda i, j: (layer, 0, j)),
                  pl.BlockSpec((tm, tn), lambda i, j: (i, GA_OFF // tn + j)),
                  pl.BlockSpec((tm, tn), lambda i, j: (i, GB_OFF // tn + j))],
        out_specs=pl.BlockSpec((tm, tn), lambda i, j: (i, j)),
        out_shape=jax.ShapeDtypeStruct((m, D_MODEL), jnp.bfloat16),
        compiler_params=_params(("parallel", "arbitrary"), 48),
        name="branch_merge",
    )(o_attn, mixed, wba, wbp, proj, proj)


def _gate_up_kernel(x_ref, inv_ref, wg_ref, wu_ref, wd_ref, o_ref, wd_bf16_ref):
    wd_bf16_ref[...] = wd_ref[...].astype(jnp.bfloat16)
    wg = wg_ref[...].astype(jnp.bfloat16)
    wu = wu_ref[...].astype(jnp.bfloat16)
    for r, n in _row_chunks(x_ref.shape[0]):
        x = x_ref[r:r + n, :]
        inv = _row_factor(inv_ref, r, n, wg.shape[1])
        gate = jnp.dot(x, wg, preferred_element_type=jnp.float32) * inv
        up = jnp.dot(x, wu, preferred_element_type=jnp.float32) * inv
        o_ref[r:r + n, :] = (jax.nn.silu(gate) * up).astype(o_ref.dtype)


def _gate_up(xg, row_inv, w_gate_up, w_down, layer, tm=2048, tn=256):
    m, k = xg.shape
    nt = D_FF // tn
    steps = (m // tm) * nt
    slab = D_FF // steps
    assert slab * steps == D_FF and slab % 16 == 0
    return pl.pallas_call(
        _gate_up_kernel,
        grid=(m // tm, nt),
        in_specs=[_panel_spec(tm, k),
                  pl.BlockSpec((tm, LANES), lambda i, j: (i, 0)),
                  pl.BlockSpec((None, k, tn), lambda i, j: (layer, 0, j)),
                  pl.BlockSpec((None, k, tn), lambda i, j: (layer, 0, nt + j)),
                  pl.BlockSpec((None, slab, D_MODEL), lambda i, j: (layer, i * nt + j, 0))],
        out_specs=[pl.BlockSpec((tm, tn), lambda i, j: (i, j)),
                   pl.BlockSpec((slab, D_MODEL), lambda i, j: (i * nt + j, 0))],
        out_shape=[jax.ShapeDtypeStruct((m, D_FF), jnp.bfloat16),
                   jax.ShapeDtypeStruct((D_FF, D_MODEL), jnp.bfloat16)],
        compiler_params=_params(("arbitrary", "arbitrary"), 48),
        name="swiglu_gate_up",
    )(xg, row_inv, w_gate_up, w_gate_up, w_down)


def _rotary_tables(seq):
    pos = jnp.arange(seq, dtype=jnp.float32)
    inv_freq = 1.0 / jnp.power(jnp.float32(ROPE_THETA),
                               jnp.arange(0, ROT_DIM, 2, dtype=jnp.float32) / ROT_DIM)
    ang = pos[:, None] * inv_freq[None, :]
    cos, sin = jnp.cos(ang), jnp.sin(ang)
    ones = jnp.ones((seq, HEAD_DIM - ROT_DIM), jnp.float32)
    zeros = jnp.zeros((seq, HEAD_DIM - ROT_DIM), jnp.float32)
    c_full = jnp.concatenate([cos, cos, ones], axis=-1)
    s_full = jnp.concatenate([-sin, sin, zeros], axis=-1)
    scale = HEAD_DIM ** -0.5
    return c_full * scale, s_full * scale, c_full, s_full


def kernel(x, norm1_g, w_in, attn_sink, pool_w, pool_scale, w_branch_attn, w_branch_pool,
           w_out, norm2_g, w_gate_up, w_down, final_norm_g):
    batch, seq, d = x.shape
    m = batch * seq
    bf16 = jnp.bfloat16
    tables = _rotary_tables(seq)
    pool_scale3 = pool_scale.reshape(DEPTH, 1, POOL_WIDTH)

    xs = x.reshape(m, d)
    h = _rmsnorm(xs, norm1_g[0].reshape(1, d), bf16)
    row_inv = None
    for l in range(DEPTH):
        proj = _matmul(h, w_in, l, jnp.float32, 2048, 512, 56, "in_proj", row_inv=row_inv)
        o_attn = _attention(proj, attn_sink[l], tables, batch, seq)
        mixed = _pool(proj, pool_w, pool_scale3, l, batch, seq)
        merged = _merge(o_attn, mixed, w_branch_attn, w_branch_pool, proj, l)
        xs, xg, row_inv = _matmul(merged, w_out, l, jnp.float32, 2048, 256, 52, "out_proj",
                                  residual=xs, next_gain=norm2_g[l].reshape(1, d))
        act, w_down_b = _gate_up(xg, row_inv, w_gate_up, w_down, l)
        if l + 1 < DEPTH:
            xs, h, row_inv = _matmul(act, w_down_b[None], 0, jnp.float32, 512, 512, 56,
                                     "ffn_down", residual=xs, panel_buffers=2,
                                     next_gain=norm1_g[l + 1].reshape(1, d))
        else:
            xs = _matmul(act, w_down_b[None], 0, jnp.float32, 512, 512, 56, "ffn_down",
                         residual=xs, panel_buffers=2)
    out = _rmsnorm(xs, final_norm_g.reshape(1, d), jnp.float32)
    return out.reshape(batch, seq, d)
```

```python
import functools
from typing import NamedTuple

import jax
import jax.numpy as jnp
import numpy as np
from jax import lax
from jax.experimental import pallas as pl
from jax.experimental.pallas import tpu as pltpu

D_MODEL = 4096
DEPTH = 2
HEAD_DIM = 128
N_HEADS = 16
N_KV_HEADS = 4
GROUP = N_HEADS // N_KV_HEADS
ATTN_WIDTH = N_HEADS * HEAD_DIM
KV_WIDTH = N_KV_HEADS * HEAD_DIM
BLOCK = 128
ROPE_THETA = 500000.0
ROT_DIM = HEAD_DIM // 4
POOL_WIDTH = D_MODEL // 2
POOL_WINDOWS = (2, 4, 8, 16)
N_POOL_GROUPS = len(POOL_WINDOWS)
POOL_GROUP_DIM = POOL_WIDTH // N_POOL_GROUPS
Q_OFF = 0
K_OFF = ATTN_WIDTH
V_OFF = K_OFF + KV_WIDTH
U_OFF = V_OFF + KV_WIDTH
GA_OFF = U_OFF + POOL_WIDTH
GB_OFF = GA_OFF + D_MODEL
IN_WIDTH = GB_OFF + D_MODEL
D_FF = 11008
RMS_EPS = 1e-6
MASK_NEG = -1e30

POOL_HALO = 8
POOL_CHUNK = 128
LANES = 128
SUB_M = 1024
TAIL_M = 256
MERGE_SUB_M = 256
MIB = 1024 * 1024
VMEM_BYTES = 64 * MIB
VMEM_COMPILER_RESERVE = 8 * MIB
VMEM_LIMIT_MATMUL = VMEM_BYTES - VMEM_COMPILER_RESERVE
VMEM_LIMIT_SMALL = 40 * MIB


class _Tiles(NamedTuple):
    tm: int
    tn: int
    panel_buffers: int


TILES = {
    "in_proj": _Tiles(2048, 512, 1),
    "merge": _Tiles(2048, 256, 2),
    "out_proj": _Tiles(2048, 256, 2),
    "gate_up": _Tiles(2048, 256, 1),
    "ffn_down": _Tiles(512, 512, 2),
}


def _params(semantics, vmem_bytes):
    return pltpu.CompilerParams(dimension_semantics=semantics, vmem_limit_bytes=vmem_bytes)


def _rmsnorm_kernel(x_ref, g_ref, o_ref):
    x = x_ref[...]
    ms = jnp.mean(x * x, axis=-1, keepdims=True)
    y = x * lax.rsqrt(ms + RMS_EPS)
    o_ref[...] = (y * g_ref[...]).astype(o_ref.dtype)


def _rmsnorm(x2d, g_row, out_dtype, tr=256):
    m, d = x2d.shape
    return pl.pallas_call(
        _rmsnorm_kernel,
        grid=(m // tr,),
        in_specs=[pl.BlockSpec((tr, d), lambda i: (i, 0)),
                  pl.BlockSpec((1, d), lambda i: (0, 0))],
        out_specs=pl.BlockSpec((tr, d), lambda i: (i, 0)),
        out_shape=jax.ShapeDtypeStruct((m, d), out_dtype),
        compiler_params=_params(("parallel",), VMEM_LIMIT_SMALL),
        name="rmsnorm",
    )(x2d, g_row)


def _row_chunks(tm, sub=None):
    if sub is None:
        sizes, left = [], tm
        while left > 0:
            size = min(SUB_M, left)
            while size > TAIL_M and size * 2 > left:
                size //= 2
            sizes.append(size)
            left -= size
    else:
        sizes = [min(tm, sub)] * (tm // min(tm, sub))
    starts = np.cumsum([0] + sizes[:-1])
    assert sum(sizes) == tm
    return [(int(r), int(n)) for r, n in zip(starts, sizes)]


def _row_factor(inv_ref, r, n, width):
    inv = inv_ref[r:r + n, :]
    return jnp.concatenate([inv] * (width // LANES), axis=1)


def _lane_group_sum(y2):
    part = y2[:, 0:LANES]
    for c in range(LANES, y2.shape[1], LANES):
        part = part + y2[:, c:c + LANES]
    return part


def _mm_kernel(*refs, has_residual, has_row_scale, emits_norm):
    refs = list(refs)
    x_ref, w_ref = refs[0], refs[1]
    pos = 2
    r_ref = g_ref = inv_in_ref = None
    if has_residual:
        r_ref = refs[pos]; pos += 1
    if emits_norm:
        g_ref = refs[pos]; pos += 1
    if has_row_scale:
        inv_in_ref = refs[pos]; pos += 1
    o_ref = refs[pos]; pos += 1
    if emits_norm:
        xg_ref, ss_ref = refs[pos], refs[pos + 1]

        @pl.when(pl.program_id(1) == 0)
        def _():
            ss_ref[...] = jnp.zeros(ss_ref.shape, ss_ref.dtype)

    w = w_ref[...].astype(jnp.bfloat16)
    for r, n in _row_chunks(x_ref.shape[0]):
        y = jnp.dot(x_ref[r:r + n, :], w, preferred_element_type=jnp.float32)
        if has_row_scale:
            y = y * _row_factor(inv_in_ref, r, n, y.shape[1])
        if has_residual:
            y = r_ref[r:r + n, :] + y
        o_ref[r:r + n, :] = y.astype(o_ref.dtype)
        if emits_norm:
            xg_ref[r:r + n, :] = (y * g_ref[...]).astype(xg_ref.dtype)
            ss_ref[r:r + n, :] += _lane_group_sum(y * y)

    if emits_norm:
        @pl.when(pl.program_id(1) == pl.num_programs(1) - 1)
        def _():
            total = jnp.sum(ss_ref[...], axis=-1, keepdims=True)
            inv = lax.rsqrt(total * (1.0 / D_MODEL) + RMS_EPS)
            ss_ref[...] = jnp.broadcast_to(inv, ss_ref.shape)


def _panel_spec(tm, k, buffers=1):
    return pl.BlockSpec((tm, k), lambda i, j: (i, 0), pipeline_mode=pl.Buffered(buffers))


def _matmul(x, w_stack, layer, out_dtype, name, residual=None, row_inv=None, next_gain=None):
    m, k = x.shape
    n = w_stack.shape[-1]
    tm, tn, panel_buffers = TILES[name]
    in_specs = [_panel_spec(tm, k, panel_buffers),
                pl.BlockSpec((None, k, tn), lambda i, j: (layer, 0, j))]
    args = [x, w_stack]
    if residual is not None:
        in_specs.append(pl.BlockSpec((tm, tn), lambda i, j: (i, j)))
        args.append(residual)
    if next_gain is not None:
        in_specs.append(pl.BlockSpec((1, tn), lambda i, j: (0, j)))
        args.append(next_gain)
    if row_inv is not None:
        in_specs.append(pl.BlockSpec((tm, LANES), lambda i, j: (i, 0)))
        args.append(row_inv)
    out_specs = pl.BlockSpec((tm, tn), lambda i, j: (i, j))
    out_shape = jax.ShapeDtypeStruct((m, n), out_dtype)
    if next_gain is not None:
        out_specs = [out_specs,
                     pl.BlockSpec((tm, tn), lambda i, j: (i, j)),
                     pl.BlockSpec((tm, LANES), lambda i, j: (i, 0))]
        out_shape = [out_shape,
                     jax.ShapeDtypeStruct((m, n), jnp.bfloat16),
                     jax.ShapeDtypeStruct((m, LANES), jnp.float32)]
    return pl.pallas_call(
        functools.partial(_mm_kernel, has_residual=residual is not None,
                          has_row_scale=row_inv is not None,
                          emits_norm=next_gain is not None),
        grid=(m // tm, n // tn),
        in_specs=in_specs,
        out_specs=out_specs,
        out_shape=out_shape,
        compiler_params=_params(("parallel", "arbitrary"), VMEM_LIMIT_MATMUL),
        name=name,
    )(*args)


def _rotary_mix(t, lane):
    half = ROT_DIM // 2
    return jnp.where(lane < half, pltpu.roll(t, HEAD_DIM - half, 1), pltpu.roll(t, half, 1))


def _attn_kernel(sink_ref, q_ref, k_ref, v_ref, cq_ref, sq_ref, ck_ref, sk_ref, bias_ref,
                 o_ref, kpad_ref, vpad_ref, *, seq):
    g = pl.program_id(1)
    nb = seq // BLOCK
    zeros = jnp.zeros((BLOCK, HEAD_DIM), jnp.bfloat16)
    kpad_ref[0:BLOCK, :] = zeros
    kpad_ref[BLOCK + seq:, :] = zeros
    vpad_ref[0:BLOCK, :] = zeros
    vpad_ref[BLOCK + seq:, :] = zeros
    lane_full = lax.broadcasted_iota(jnp.int32, (seq, HEAD_DIM), 1)
    k = k_ref[...]
    k_rot = k * ck_ref[...] + _rotary_mix(k, lane_full) * sk_ref[...]
    kpad_ref[BLOCK:BLOCK + seq, :] = k_rot.astype(jnp.bfloat16)
    vpad_ref[BLOCK:BLOCK + seq, :] = v_ref[...].astype(jnp.bfloat16)

    rows = GROUP * BLOCK
    row_head = lax.broadcasted_iota(jnp.int32, (rows, 1), 0) // BLOCK
    sink = jnp.zeros((rows, 1), jnp.float32)
    for h in range(GROUP):
        sink = jnp.where(row_head == h, sink_ref[g * GROUP + h], sink)

    def body(n, carry):
        r0 = pl.multiple_of(n * BLOCK, BLOCK)
        lane_blk = lax.broadcasted_iota(jnp.int32, (BLOCK, HEAD_DIM), 1)
        cq = cq_ref[pl.ds(r0, BLOCK), :]
        sq = sq_ref[pl.ds(r0, BLOCK), :]
        heads = []
        for h in range(GROUP):
            qh = q_ref[pl.ds(r0, BLOCK), h * HEAD_DIM:(h + 1) * HEAD_DIM]
            heads.append((qh * cq + _rotary_mix(qh, lane_blk) * sq).astype(jnp.bfloat16))
        q = jnp.concatenate(heads, axis=0)
        kw = kpad_ref[pl.ds(r0, 3 * BLOCK), :]
        vw = vpad_ref[pl.ds(r0, 3 * BLOCK), :]
        s = lax.dot_general(q, kw, (((1,), (1,)), ((), ())),
                            preferred_element_type=jnp.float32)
        which = jnp.where(n == 0, 0, jnp.where(n == nb - 1, 2, 1))
        s = s + jnp.concatenate([bias_ref[which]] * GROUP, axis=0)
        m = jnp.maximum(jnp.max(s, axis=-1, keepdims=True), sink)
        p = jnp.exp(s - m)
        denom = jnp.sum(p, axis=-1, keepdims=True) + jnp.exp(sink - m)
        o = jnp.dot(p.astype(jnp.bfloat16), vw, preferred_element_type=jnp.float32)
        o = o / denom
        for h in range(GROUP):
            o_ref[pl.ds(r0, BLOCK), h * HEAD_DIM:(h + 1) * HEAD_DIM] = (
                o[h * BLOCK:(h + 1) * BLOCK, :].astype(o_ref.dtype))
        return carry

    lax.fori_loop(0, nb, body, 0, unroll=2)


def _window_mask_bias():
    qi = np.arange(BLOCK)[:, None]
    kj = np.arange(3 * BLOCK)[None, :]
    band = (kj >= qi) & (kj <= qi + 2 * BLOCK)
    first = band & (kj >= BLOCK)
    last = band & (kj < 2 * BLOCK)
    return np.where(np.stack([first, band, last]), 0.0, MASK_NEG).astype(np.float32)


def _attention(proj, sink_l, tables, batch, seq):
    cq, sq, ck, sk = tables
    m = proj.shape[0]
    qw = GROUP * HEAD_DIM
    assert seq // BLOCK >= 2
    bias = jnp.asarray(_window_mask_bias())
    tab_spec = pl.BlockSpec((seq, HEAD_DIM), lambda b, g: (0, 0))
    return pl.pallas_call(
        functools.partial(_attn_kernel, seq=seq),
        grid=(batch, N_KV_HEADS),
        in_specs=[pl.BlockSpec(memory_space=pltpu.SMEM),
                  pl.BlockSpec((seq, qw), lambda b, g: (b, Q_OFF // qw + g)),
                  pl.BlockSpec((seq, HEAD_DIM), lambda b, g: (b, K_OFF // HEAD_DIM + g)),
                  pl.BlockSpec((seq, HEAD_DIM), lambda b, g: (b, V_OFF // HEAD_DIM + g)),
                  tab_spec, tab_spec, tab_spec, tab_spec,
                  pl.BlockSpec(bias.shape, lambda b, g: (0, 0, 0))],
        out_specs=pl.BlockSpec((seq, qw), lambda b, g: (b, g)),
        out_shape=jax.ShapeDtypeStruct((m, ATTN_WIDTH), jnp.bfloat16),
        scratch_shapes=[pltpu.VMEM((seq + 2 * BLOCK, HEAD_DIM), jnp.bfloat16),
                        pltpu.VMEM((seq + 2 * BLOCK, HEAD_DIM), jnp.bfloat16)],
        compiler_params=_params(("parallel", "arbitrary"), VMEM_LIMIT_SMALL),
        name="window_gqa",
    )(sink_l, proj, proj, proj, cq, sq, ck, sk, bias)


def _pool_kernel(u_ref, w_ref, scale_ref, o_ref, pad_ref, *, seq):
    g = pl.program_id(1)
    zeros = jnp.zeros((POOL_HALO, POOL_GROUP_DIM), jnp.float32)
    pad_ref[0:POOL_HALO, :] = zeros
    pad_ref[POOL_HALO + seq:, :] = zeros
    pad_ref[POOL_HALO:POOL_HALO + seq, :] = u_ref[...]
    wmat = w_ref[...].astype(jnp.bfloat16)
    scale = scale_ref[...]

    def run(win):
        half = win // 2
        ext = POOL_CHUNK + 2 * POOL_HALO

        def body(c, carry):
            t0 = pl.multiple_of(c * POOL_CHUNK, POOL_CHUNK)
            xe = pad_ref[pl.ds(t0, ext), :]
            a = xe
            step = 1
            while step < half:
                a = a + pltpu.roll(a, ext - step, 0)
                step *= 2
            a = a + pltpu.roll(a, half, 0)
            wsum = a[POOL_HALO:POOL_HALO + POOL_CHUNK, :]
            x = xe[POOL_HALO:POOL_HALO + POOL_CHUNK, :]
            t = t0 + lax.broadcasted_iota(jnp.int32, (POOL_CHUNK, 1), 0)
            lo = jnp.maximum(t - half, 0)
            hi = jnp.minimum(t + half - 1, seq - 1)
            cnt = (hi - lo + 1).astype(jnp.float32)
            pooled = wsum / cnt - x
            mixed = jnp.dot(pooled.astype(jnp.bfloat16), wmat,
                            preferred_element_type=jnp.float32)
            o_ref[pl.ds(t0, POOL_CHUNK), :] = (mixed * scale).astype(o_ref.dtype)
            return carry

        lax.fori_loop(0, seq // POOL_CHUNK, body, 0, unroll=2)

    for gi, win in enumerate(POOL_WINDOWS):
        pl.when(g == gi)(functools.partial(run, win))


def _pool(proj, pool_w, pool_scale3, layer, batch, seq):
    m = proj.shape[0]
    c = POOL_GROUP_DIM
    return pl.pallas_call(
        functools.partial(_pool_kernel, seq=seq),
        grid=(batch, N_POOL_GROUPS),
        in_specs=[pl.BlockSpec((seq, c), lambda b, g: (b, U_OFF // c + g)),
                  pl.BlockSpec((None, None, c, c), lambda b, g: (layer, g, 0, 0)),
                  pl.BlockSpec((None, 1, c), lambda b, g: (layer, 0, g))],
        out_specs=pl.BlockSpec((seq, c), lambda b, g: (b, g)),
        out_shape=jax.ShapeDtypeStruct((m, POOL_WIDTH), jnp.bfloat16),
        scratch_shapes=[pltpu.VMEM((seq + 2 * POOL_HALO, c), jnp.float32)],
        compiler_params=_params(("parallel", "arbitrary"), VMEM_LIMIT_SMALL),
        name="multiscale_pool",
    )(proj, pool_w, pool_scale3)


def _merge_kernel(a_ref, p_ref, wa_ref, wp_ref, ga_ref, gb_ref, o_ref):
    wa = wa_ref[...].astype(jnp.bfloat16)
    wp = wp_ref[...].astype(jnp.bfloat16)
    for r, n in _row_chunks(a_ref.shape[0], MERGE_SUB_M):
        ya = jnp.dot(a_ref[r:r + n, :], wa, preferred_element_type=jnp.float32)
        yp = jnp.dot(p_ref[r:r + n, :], wp, preferred_element_type=jnp.float32)
        merged = (jax.nn.sigmoid(ga_ref[r:r + n, :]) * ya
                  + jax.nn.sigmoid(gb_ref[r:r + n, :]) * yp)
        o_ref[r:r + n, :] = merged.astype(o_ref.dtype)


def _merge(o_attn, mixed, wba, wbp, proj, layer):
    m = o_attn.shape[0]
    tm, tn, panel_buffers = TILES["merge"]
    return pl.pallas_call(
        _merge_kernel,
        grid=(m // tm, D_MODEL // tn),
        in_specs=[_panel_spec(tm, ATTN_WIDTH, panel_buffers),
                  _panel_spec(tm, POOL_WIDTH, panel_buffers),
                  pl.BlockSpec((None, ATTN_WIDTH, tn), lambda i, j: (layer, 0, j)),
                  pl.BlockSpec((None, POOL_WIDTH, tn), lambda i, j: (layer, 0, j)),
                  pl.BlockSpec((tm, tn), lambda i, j: (i, GA_OFF // tn + j)),
                  pl.BlockSpec((tm, tn), lambda i, j: (i, GB_OFF // tn + j))],
        out_specs=pl.BlockSpec((tm, tn), lambda i, j: (i, j)),
        out_shape=jax.ShapeDtypeStruct((m, D_MODEL), jnp.bfloat16),
        compiler_params=_params(("parallel", "arbitrary"), VMEM_LIMIT_MATMUL),
        name="branch_merge",
    )(o_attn, mixed, wba, wbp, proj, proj)


def _gate_up_kernel(x_ref, inv_ref, wg_ref, wu_ref, wd_ref, o_ref, wd_bf16_ref):
    wd_bf16_ref[...] = wd_ref[...].astype(jnp.bfloat16)
    wg = wg_ref[...].astype(jnp.bfloat16)
    wu = wu_ref[...].astype(jnp.bfloat16)
    for r, n in _row_chunks(x_ref.shape[0]):
        x = x_ref[r:r + n, :]
        inv = _row_factor(inv_ref, r, n, wg.shape[1])
        gate = jnp.dot(x, wg, preferred_element_type=jnp.float32) * inv
        up = jnp.dot(x, wu, preferred_element_type=jnp.float32) * inv
        o_ref[r:r + n, :] = (jax.nn.silu(gate) * up).astype(o_ref.dtype)


def _gate_up(xg, row_inv, w_gate_up, w_down, layer):
    m, k = xg.shape
    tm, tn, panel_buffers = TILES["gate_up"]
    nt = D_FF // tn
    steps = (m // tm) * nt
    slab = D_FF // steps
    assert slab * steps == D_FF and slab % 16 == 0
    return pl.pallas_call(
        _gate_up_kernel,
        grid=(m // tm, nt),
        in_specs=[_panel_spec(tm, k, panel_buffers),
                  pl.BlockSpec((tm, LANES), lambda i, j: (i, 0)),
                  pl.BlockSpec((None, k, tn), lambda i, j: (layer, 0, j)),
                  pl.BlockSpec((None, k, tn), lambda i, j: (layer, 0, nt + j)),
                  pl.BlockSpec((None, slab, D_MODEL), lambda i, j: (layer, i * nt + j, 0))],
        out_specs=[pl.BlockSpec((tm, tn), lambda i, j: (i, j)),
                   pl.BlockSpec((slab, D_MODEL), lambda i, j: (i * nt + j, 0))],
        out_shape=[jax.ShapeDtypeStruct((m, D_FF), jnp.bfloat16),
                   jax.ShapeDtypeStruct((D_FF, D_MODEL), jnp.bfloat16)],
        compiler_params=_params(("arbitrary", "arbitrary"), VMEM_LIMIT_MATMUL),
        name="swiglu_gate_up",
    )(xg, row_inv, w_gate_up, w_gate_up, w_down)


def _rotary_tables(seq):
    pos = jnp.arange(seq, dtype=jnp.float32)
    inv_freq = 1.0 / jnp.power(jnp.float32(ROPE_THETA),
                               jnp.arange(0, ROT_DIM, 2, dtype=jnp.float32) / ROT_DIM)
    ang = pos[:, None] * inv_freq[None, :]
    cos, sin = jnp.cos(ang), jnp.sin(ang)
    ones = jnp.ones((seq, HEAD_DIM - ROT_DIM), jnp.float32)
    zeros = jnp.zeros((seq, HEAD_DIM - ROT_DIM), jnp.float32)
    c_full = jnp.concatenate([cos, cos, ones], axis=-1)
    s_full = jnp.concatenate([-sin, sin, zeros], axis=-1)
    scale = HEAD_DIM ** -0.5
    return c_full * scale, s_full * scale, c_full, s_full


def kernel(x, norm1_g, w_in, attn_sink, pool_w, pool_scale, w_branch_attn, w_branch_pool,
           w_out, norm2_g, w_gate_up, w_down, final_norm_g):
    batch, seq, d = x.shape
    m = batch * seq
    bf16 = jnp.bfloat16
    tables = _rotary_tables(seq)
    pool_scale3 = pool_scale.reshape(DEPTH, 1, POOL_WIDTH)

    xs = x.reshape(m, d)
    h = _rmsnorm(xs, norm1_g[0].reshape(1, d), bf16)
    row_inv = None
    for l in range(DEPTH):
        proj = _matmul(h, w_in, l, jnp.float32, "in_proj", row_inv=row_inv)
        o_attn = _attention(proj, attn_sink[l], tables, batch, seq)
        mixed = _pool(proj, pool_w, pool_scale3, l, batch, seq)
        merged = _merge(o_attn, mixed, w_branch_attn, w_branch_pool, proj, l)
        xs, xg, row_inv = _matmul(merged, w_out, l, jnp.float32, "out_proj", residual=xs,
                                  next_gain=norm2_g[l].reshape(1, d))
        act, w_down_b = _gate_up(xg, row_inv, w_gate_up, w_down, l)
        if l + 1 < DEPTH:
            xs, h, row_inv = _matmul(act, w_down_b[None], 0, jnp.float32, "ffn_down",
                                     residual=xs, next_gain=norm1_g[l + 1].reshape(1, d))
        else:
            xs = _matmul(act, w_down_b[None], 0, jnp.float32, "ffn_down", residual=xs)
    out = _rmsnorm(xs, final_norm_g.reshape(1, d), jnp.float32)
    return out.reshape(batch, seq, d)
```

```python
import functools
from typing import NamedTuple

import jax
import jax.numpy as jnp
import numpy as np
from jax import lax
from jax.experimental import pallas as pl
from jax.experimental.pallas import tpu as pltpu

D_MODEL = 4096
DEPTH = 2
HEAD_DIM = 128
N_HEADS = 16
N_KV_HEADS = 4
GROUP = N_HEADS // N_KV_HEADS
ATTN_WIDTH = N_HEADS * HEAD_DIM
KV_WIDTH = N_KV_HEADS * HEAD_DIM
BLOCK = 128
ROPE_THETA = 500000.0
ROT_DIM = HEAD_DIM // 4
POOL_WIDTH = D_MODEL // 2
POOL_WINDOWS = (2, 4, 8, 16)
N_POOL_GROUPS = len(POOL_WINDOWS)
POOL_GROUP_DIM = POOL_WIDTH // N_POOL_GROUPS
Q_OFF = 0
K_OFF = ATTN_WIDTH
V_OFF = K_OFF + KV_WIDTH
U_OFF = V_OFF + KV_WIDTH
GA_OFF = U_OFF + POOL_WIDTH
GB_OFF = GA_OFF + D_MODEL
IN_WIDTH = GB_OFF + D_MODEL
D_FF = 11008
RMS_EPS = 1e-6
MASK_NEG = -1e30

POOL_HALO = 8
POOL_CHUNK = 128
LANES = 128
SUB_M = 1024
TAIL_M = 256
MERGE_SUB_M = 256
MIB = 1024 * 1024
VMEM_BYTES = 64 * MIB
VMEM_COMPILER_RESERVE = 8 * MIB
VMEM_LIMIT_MATMUL = VMEM_BYTES - VMEM_COMPILER_RESERVE
VMEM_LIMIT_SMALL = 40 * MIB


class _Tiles(NamedTuple):
    tm: int
    tn: int
    panel_buffers: int


TILES = {
    "in_proj": _Tiles(2048, 512, 1),
    "merge": _Tiles(2048, 256, 2),
    "out_proj": _Tiles(2048, 256, 2),
    "gate_up": _Tiles(2048, 256, 1),
    "ffn_down": _Tiles(512, 512, 2),
}


def _params(semantics, vmem_bytes):
    return pltpu.CompilerParams(dimension_semantics=semantics, vmem_limit_bytes=vmem_bytes)


def _rmsnorm_kernel(x_ref, g_ref, o_ref):
    x = x_ref[...]
    ms = jnp.mean(x * x, axis=-1, keepdims=True)
    y = x * lax.rsqrt(ms + RMS_EPS)
    o_ref[...] = (y * g_ref[...]).astype(o_ref.dtype)


def _rmsnorm(x2d, g_row, out_dtype, tr=256):
    m, d = x2d.shape
    return pl.pallas_call(
        _rmsnorm_kernel,
        grid=(m // tr,),
        in_specs=[pl.BlockSpec((tr, d), lambda i: (i, 0)),
                  pl.BlockSpec((1, d), lambda i: (0, 0))],
        out_specs=pl.BlockSpec((tr, d), lambda i: (i, 0)),
        out_shape=jax.ShapeDtypeStruct((m, d), out_dtype),
        compiler_params=_params(("parallel",), VMEM_LIMIT_SMALL),
        name="rmsnorm",
    )(x2d, g_row)


def _row_chunks(tm, sub=None):
    if sub is None:
        sizes, left = [], tm
        while left > 0:
            size = min(SUB_M, left)
            while size > TAIL_M and size * 2 > left:
                size //= 2
            sizes.append(size)
            left -= size
    else:
        sizes = [min(tm, sub)] * (tm // min(tm, sub))
    starts = np.cumsum([0] + sizes[:-1])
    assert sum(sizes) == tm
    return [(int(r), int(n)) for r, n in zip(starts, sizes)]


def _row_factor(inv_ref, r, n, width):
    inv = inv_ref[r:r + n, :]
    return jnp.concatenate([inv] * (width // LANES), axis=1)


def _lane_group_sum(y2):
    part = y2[:, 0:LANES]
    for c in range(LANES, y2.shape[1], LANES):
        part = part + y2[:, c:c + LANES]
    return part


def _mm_kernel(*refs, has_residual, has_row_scale, emits_norm):
    refs = list(refs)
    x_ref, w_ref = refs[0], refs[1]
    pos = 2
    r_ref = g_ref = inv_in_ref = None
    if has_residual:
        r_ref = refs[pos]; pos += 1
    if emits_norm:
        g_ref = refs[pos]; pos += 1
    if has_row_scale:
        inv_in_ref = refs[pos]; pos += 1
    o_ref = refs[pos]; pos += 1
    if emits_norm:
        xg_ref, ss_ref = refs[pos], refs[pos + 1]

        @pl.when(pl.program_id(1) == 0)
        def _():
            ss_ref[...] = jnp.zeros(ss_ref.shape, ss_ref.dtype)

    w = w_ref[...].astype(jnp.bfloat16)
    for r, n in _row_chunks(x_ref.shape[0]):
        y = jnp.dot(x_ref[r:r + n, :], w, preferred_element_type=jnp.float32)
        if has_row_scale:
            y = y * _row_factor(inv_in_ref, r, n, y.shape[1])
        if has_residual:
            y = r_ref[r:r + n, :] + y
        o_ref[r:r + n, :] = y.astype(o_ref.dtype)
        if emits_norm:
            xg_ref[r:r + n, :] = (y * g_ref[...]).astype(xg_ref.dtype)
            ss_ref[r:r + n, :] += _lane_group_sum(y * y)

    if emits_norm:
        @pl.when(pl.program_id(1) == pl.num_programs(1) - 1)
        def _():
            total = jnp.sum(ss_ref[...], axis=-1, keepdims=True)
            inv = lax.rsqrt(total * (1.0 / D_MODEL) + RMS_EPS)
            ss_ref[...] = jnp.broadcast_to(inv, ss_ref.shape)


def _panel_spec(tm, k, buffers=1):
    return pl.BlockSpec((tm, k), lambda i, j: (i, 0), pipeline_mode=pl.Buffered(buffers))


def _matmul(x, w_stack, layer, out_dtype, name, residual=None, row_inv=None, next_gain=None,
            w_tiled=False):
    m, k = x.shape
    tm, tn, panel_buffers = TILES[name]
    if w_tiled:
        assert w_stack.shape[1:] == (k, tn)
        n = w_stack.shape[0] * tn
        w_spec = pl.BlockSpec((None, k, tn), lambda i, j: (j, 0, 0))
    else:
        n = w_stack.shape[-1]
        w_spec = pl.BlockSpec((None, k, tn), lambda i, j: (layer, 0, j))
    in_specs = [_panel_spec(tm, k, panel_buffers), w_spec]
    args = [x, w_stack]
    if residual is not None:
        in_specs.append(pl.BlockSpec((tm, tn), lambda i, j: (i, j)))
        args.append(residual)
    if next_gain is not None:
        in_specs.append(pl.BlockSpec((1, tn), lambda i, j: (0, j)))
        args.append(next_gain)
    if row_inv is not None:
        in_specs.append(pl.BlockSpec((tm, LANES), lambda i, j: (i, 0)))
        args.append(row_inv)
    out_specs = pl.BlockSpec((tm, tn), lambda i, j: (i, j))
    out_shape = jax.ShapeDtypeStruct((m, n), out_dtype)
    if next_gain is not None:
        out_specs = [out_specs,
                     pl.BlockSpec((tm, tn), lambda i, j: (i, j)),
                     pl.BlockSpec((tm, LANES), lambda i, j: (i, 0))]
        out_shape = [out_shape,
                     jax.ShapeDtypeStruct((m, n), jnp.bfloat16),
                     jax.ShapeDtypeStruct((m, LANES), jnp.float32)]
    return pl.pallas_call(
        functools.partial(_mm_kernel, has_residual=residual is not None,
                          has_row_scale=row_inv is not None,
                          emits_norm=next_gain is not None),
        grid=(m // tm, n // tn),
        in_specs=in_specs,
        out_specs=out_specs,
        out_shape=out_shape,
        compiler_params=_params(("parallel", "arbitrary"), VMEM_LIMIT_MATMUL),
        name=name,
    )(*args)


def _rotary_mix(t, lane):
    half = ROT_DIM // 2
    return jnp.where(lane < half, pltpu.roll(t, HEAD_DIM - half, 1), pltpu.roll(t, half, 1))


def _attn_kernel(sink_ref, q_ref, k_ref, v_ref, cq_ref, sq_ref, ck_ref, sk_ref, bias_ref,
                 o_ref, kpad_ref, vpad_ref, *, seq):
    g = pl.program_id(1)
    nb = seq // BLOCK
    zeros = jnp.zeros((BLOCK, HEAD_DIM), jnp.bfloat16)
    kpad_ref[0:BLOCK, :] = zeros
    kpad_ref[BLOCK + seq:, :] = zeros
    vpad_ref[0:BLOCK, :] = zeros
    vpad_ref[BLOCK + seq:, :] = zeros
    lane_full = lax.broadcasted_iota(jnp.int32, (seq, HEAD_DIM), 1)
    k = k_ref[...]
    k_rot = k * ck_ref[...] + _rotary_mix(k, lane_full) * sk_ref[...]
    kpad_ref[BLOCK:BLOCK + seq, :] = k_rot.astype(jnp.bfloat16)
    vpad_ref[BLOCK:BLOCK + seq, :] = v_ref[...].astype(jnp.bfloat16)

    rows = GROUP * BLOCK
    row_head = lax.broadcasted_iota(jnp.int32, (rows, 1), 0) // BLOCK
    sink = jnp.zeros((rows, 1), jnp.float32)
    for h in range(GROUP):
        sink = jnp.where(row_head == h, sink_ref[g * GROUP + h], sink)

    def body(n, carry):
        r0 = pl.multiple_of(n * BLOCK, BLOCK)
        lane_blk = lax.broadcasted_iota(jnp.int32, (BLOCK, HEAD_DIM), 1)
        cq = cq_ref[pl.ds(r0, BLOCK), :]
        sq = sq_ref[pl.ds(r0, BLOCK), :]
        heads = []
        for h in range(GROUP):
            qh = q_ref[pl.ds(r0, BLOCK), h * HEAD_DIM:(h + 1) * HEAD_DIM]
            heads.append((qh * cq + _rotary_mix(qh, lane_blk) * sq).astype(jnp.bfloat16))
        q = jnp.concatenate(heads, axis=0)
        kw = kpad_ref[pl.ds(r0, 3 * BLOCK), :]
        vw = vpad_ref[pl.ds(r0, 3 * BLOCK), :]
        s = lax.dot_general(q, kw, (((1,), (1,)), ((), ())),
                            preferred_element_type=jnp.float32)
        which = jnp.where(n == 0, 0, jnp.where(n == nb - 1, 2, 1))
        s = s + jnp.concatenate([bias_ref[which]] * GROUP, axis=0)
        m = jnp.maximum(jnp.max(s, axis=-1, keepdims=True), sink)
        p = jnp.exp(s - m)
        denom = jnp.sum(p, axis=-1, keepdims=True) + jnp.exp(sink - m)
        o = jnp.dot(p.astype(jnp.bfloat16), vw, preferred_element_type=jnp.float32)
        o = o / denom
        for h in range(GROUP):
            o_ref[pl.ds(r0, BLOCK), h * HEAD_DIM:(h + 1) * HEAD_DIM] = (
                o[h * BLOCK:(h + 1) * BLOCK, :].astype(o_ref.dtype))
        return carry

    lax.fori_loop(0, nb, body, 0, unroll=2)


def _window_mask_bias():
    qi = np.arange(BLOCK)[:, None]
    kj = np.arange(3 * BLOCK)[None, :]
    band = (kj >= qi) & (kj <= qi + 2 * BLOCK)
    first = band & (kj >= BLOCK)
    last = band & (kj < 2 * BLOCK)
    return np.where(np.stack([first, band, last]), 0.0, MASK_NEG).astype(np.float32)


def _attention(proj, sink_l, tables, batch, seq):
    cq, sq, ck, sk = tables
    m = proj.shape[0]
    qw = GROUP * HEAD_DIM
    assert seq // BLOCK >= 2
    bias = jnp.asarray(_window_mask_bias())
    tab_spec = pl.BlockSpec((seq, HEAD_DIM), lambda b, g: (0, 0))
    return pl.pallas_call(
        functools.partial(_attn_kernel, seq=seq),
        grid=(batch, N_KV_HEADS),
        in_specs=[pl.BlockSpec(memory_space=pltpu.SMEM),
                  pl.BlockSpec((seq, qw), lambda b, g: (b, Q_OFF // qw + g)),
                  pl.BlockSpec((seq, HEAD_DIM), lambda b, g: (b, K_OFF // HEAD_DIM + g)),
                  pl.BlockSpec((seq, HEAD_DIM), lambda b, g: (b, V_OFF // HEAD_DIM + g)),
                  tab_spec, tab_spec, tab_spec, tab_spec,
                  pl.BlockSpec(bias.shape, lambda b, g: (0, 0, 0))],
        out_specs=pl.BlockSpec((seq, qw), lambda b, g: (b, g)),
        out_shape=jax.ShapeDtypeStruct((m, ATTN_WIDTH), jnp.bfloat16),
        scratch_shapes=[pltpu.VMEM((seq + 2 * BLOCK, HEAD_DIM), jnp.bfloat16),
                        pltpu.VMEM((seq + 2 * BLOCK, HEAD_DIM), jnp.bfloat16)],
        compiler_params=_params(("parallel", "arbitrary"), VMEM_LIMIT_SMALL),
        name="window_gqa",
    )(sink_l, proj, proj, proj, cq, sq, ck, sk, bias)


def _pool_kernel(u_ref, w_ref, scale_ref, o_ref, pad_ref, *, seq):
    g = pl.program_id(1)
    zeros = jnp.zeros((POOL_HALO, POOL_GROUP_DIM), jnp.float32)
    pad_ref[0:POOL_HALO, :] = zeros
    pad_ref[POOL_HALO + seq:, :] = zeros
    pad_ref[POOL_HALO:POOL_HALO + seq, :] = u_ref[...]
    wmat = w_ref[...].astype(jnp.bfloat16)
    scale = scale_ref[...]

    def run(win):
        half = win // 2
        ext = POOL_CHUNK + 2 * POOL_HALO

        def body(c, carry):
            t0 = pl.multiple_of(c * POOL_CHUNK, POOL_CHUNK)
            xe = pad_ref[pl.ds(t0, ext), :]
            a = xe
            step = 1
            while step < half:
                a = a + pltpu.roll(a, ext - step, 0)
                step *= 2
            a = a + pltpu.roll(a, half, 0)
            wsum = a[POOL_HALO:POOL_HALO + POOL_CHUNK, :]
            x = xe[POOL_HALO:POOL_HALO + POOL_CHUNK, :]
            t = t0 + lax.broadcasted_iota(jnp.int32, (POOL_CHUNK, 1), 0)
            lo = jnp.maximum(t - half, 0)
            hi = jnp.minimum(t + half - 1, seq - 1)
            cnt = (hi - lo + 1).astype(jnp.float32)
            pooled = wsum / cnt - x
            mixed = jnp.dot(pooled.astype(jnp.bfloat16), wmat,
                            preferred_element_type=jnp.float32)
            o_ref[pl.ds(t0, POOL_CHUNK), :] = (mixed * scale).astype(o_ref.dtype)
            return carry

        lax.fori_loop(0, seq // POOL_CHUNK, body, 0, unroll=2)

    for gi, win in enumerate(POOL_WINDOWS):
        pl.when(g == gi)(functools.partial(run, win))


def _pool(proj, pool_w, pool_scale3, layer, batch, seq):
    m = proj.shape[0]
    c = POOL_GROUP_DIM
    return pl.pallas_call(
        functools.partial(_pool_kernel, seq=seq),
        grid=(batch, N_POOL_GROUPS),
        in_specs=[pl.BlockSpec((seq, c), lambda b, g: (b, U_OFF // c + g)),
                  pl.BlockSpec((None, None, c, c), lambda b, g: (layer, g, 0, 0)),
                  pl.BlockSpec((None, 1, c), lambda b, g: (layer, 0, g))],
        out_specs=pl.BlockSpec((seq, c), lambda b, g: (b, g)),
        out_shape=jax.ShapeDtypeStruct((m, POOL_WIDTH), jnp.bfloat16),
        scratch_shapes=[pltpu.VMEM((seq + 2 * POOL_HALO, c), jnp.float32)],
        compiler_params=_params(("parallel", "arbitrary"), VMEM_LIMIT_SMALL),
        name="multiscale_pool",
    )(proj, pool_w, pool_scale3)


def _merge_kernel(a_ref, p_ref, wa_ref, wp_ref, ga_ref, gb_ref, o_ref):
    wa = wa_ref[...].astype(jnp.bfloat16)
    wp = wp_ref[...].astype(jnp.bfloat16)
    for r, n in _row_chunks(a_ref.shape[0], MERGE_SUB_M):
        ya = jnp.dot(a_ref[r:r + n, :], wa, preferred_element_type=jnp.float32)
        yp = jnp.dot(p_ref[r:r + n, :], wp, preferred_element_type=jnp.float32)
        merged = (jax.nn.sigmoid(ga_ref[r:r + n, :]) * ya
                  + jax.nn.sigmoid(gb_ref[r:r + n, :]) * yp)
        o_ref[r:r + n, :] = merged.astype(o_ref.dtype)


def _merge(o_attn, mixed, wba, wbp, proj, layer):
    m = o_attn.shape[0]
    tm, tn, panel_buffers = TILES["merge"]
    return pl.pallas_call(
        _merge_kernel,
        grid=(m // tm, D_MODEL // tn),
        in_specs=[_panel_spec(tm, ATTN_WIDTH, panel_buffers),
                  _panel_spec(tm, POOL_WIDTH, panel_buffers),
                  pl.BlockSpec((None, ATTN_WIDTH, tn), lambda i, j: (layer, 0, j)),
                  pl.BlockSpec((None, POOL_WIDTH, tn), lambda i, j: (layer, 0, j)),
                  pl.BlockSpec((tm, tn), lambda i, j: (i, GA_OFF // tn + j)),
                  pl.BlockSpec((tm, tn), lambda i, j: (i, GB_OFF // tn + j))],
        out_specs=pl.BlockSpec((tm, tn), lambda i, j: (i, j)),
        out_shape=jax.ShapeDtypeStruct((m, D_MODEL), jnp.bfloat16),
        compiler_params=_params(("parallel", "arbitrary"), VMEM_LIMIT_MATMUL),
        name="branch_merge",
    )(o_attn, mixed, wba, wbp, proj, proj)


def _gate_up_kernel(x_ref, inv_ref, wg_ref, wu_ref, wd_ref, o_ref, wd_bf16_ref):
    wd = wd_ref[...].astype(jnp.bfloat16)
    tile = wd_bf16_ref.shape[-1]
    for t in range(wd_bf16_ref.shape[0]):
        wd_bf16_ref[t] = wd[:, t * tile:(t + 1) * tile]
    wg = wg_ref[...].astype(jnp.bfloat16)
    wu = wu_ref[...].astype(jnp.bfloat16)
    for r, n in _row_chunks(x_ref.shape[0]):
        x = x_ref[r:r + n, :]
        inv = _row_factor(inv_ref, r, n, wg.shape[1])
        gate = jnp.dot(x, wg, preferred_element_type=jnp.float32) * inv
        up = jnp.dot(x, wu, preferred_element_type=jnp.float32) * inv
        o_ref[r:r + n, :] = (jax.nn.silu(gate) * up).astype(o_ref.dtype)


def _gate_up(xg, row_inv, w_gate_up, w_down, layer):
    m, k = xg.shape
    tm, tn, panel_buffers = TILES["gate_up"]
    nt = D_FF // tn
    steps = (m // tm) * nt
    slab = D_FF // steps
    assert slab * steps == D_FF and slab % 16 == 0
    down_tn = TILES["ffn_down"].tn
    down_tiles = D_MODEL // down_tn
    return pl.pallas_call(
        _gate_up_kernel,
        grid=(m // tm, nt),
        in_specs=[_panel_spec(tm, k, panel_buffers),
                  pl.BlockSpec((tm, LANES), lambda i, j: (i, 0)),
                  pl.BlockSpec((None, k, tn), lambda i, j: (layer, 0, j)),
                  pl.BlockSpec((None, k, tn), lambda i, j: (layer, 0, nt + j)),
                  pl.BlockSpec((None, slab, D_MODEL), lambda i, j: (layer, i * nt + j, 0))],
        out_specs=[pl.BlockSpec((tm, tn), lambda i, j: (i, j)),
                   pl.BlockSpec((down_tiles, slab, down_tn), lambda i, j: (0, i * nt + j, 0))],
        out_shape=[jax.ShapeDtypeStruct((m, D_FF), jnp.bfloat16),
                   jax.ShapeDtypeStruct((down_tiles, D_FF, down_tn), jnp.bfloat16)],
        compiler_params=_params(("arbitrary", "arbitrary"), VMEM_LIMIT_MATMUL),
        name="swiglu_gate_up",
    )(xg, row_inv, w_gate_up, w_gate_up, w_down)


def _rotary_tables(seq):
    pos = jnp.arange(seq, dtype=jnp.float32)
    inv_freq = 1.0 / jnp.power(jnp.float32(ROPE_THETA),
                               jnp.arange(0, ROT_DIM, 2, dtype=jnp.float32) / ROT_DIM)
    ang = pos[:, None] * inv_freq[None, :]
    cos, sin = jnp.cos(ang), jnp.sin(ang)
    ones = jnp.ones((seq, HEAD_DIM - ROT_DIM), jnp.float32)
    zeros = jnp.zeros((seq, HEAD_DIM - ROT_DIM), jnp.float32)
    c_full = jnp.concatenate([cos, cos, ones], axis=-1)
    s_full = jnp.concatenate([-sin, sin, zeros], axis=-1)
    scale = HEAD_DIM ** -0.5
    return c_full * scale, s_full * scale, c_full, s_full


def kernel(x, norm1_g, w_in, attn_sink, pool_w, pool_scale, w_branch_attn, w_branch_pool,
           w_out, norm2_g, w_gate_up, w_down, final_norm_g):
    batch, seq, d = x.shape
    m = batch * seq
    bf16 = jnp.bfloat16
    tables = _rotary_tables(seq)
    pool_scale3 = pool_scale.reshape(DEPTH, 1, POOL_WIDTH)

    xs = x.reshape(m, d)
    h = _rmsnorm(xs, norm1_g[0].reshape(1, d), bf16)
    row_inv = None
    for l in range(DEPTH):
        proj = _matmul(h, w_in, l, jnp.float32, "in_proj", row_inv=row_inv)
        o_attn = _attention(proj, attn_sink[l], tables, batch, seq)
        mixed = _pool(proj, pool_w, pool_scale3, l, batch, seq)
        merged = _merge(o_attn, mixed, w_branch_attn, w_branch_pool, proj, l)
        xs, xg, row_inv = _matmul(merged, w_out, l, jnp.float32, "out_proj", residual=xs,
                                  next_gain=norm2_g[l].reshape(1, d))
        act, w_down_b = _gate_up(xg, row_inv, w_gate_up, w_down, l)
        if l + 1 < DEPTH:
            xs, h, row_inv = _matmul(act, w_down_b, 0, jnp.float32, "ffn_down", residual=xs,
                                     next_gain=norm1_g[l + 1].reshape(1, d), w_tiled=True)
        else:
            xs = _matmul(act, w_down_b, 0, jnp.float32, "ffn_down", residual=xs,
                         w_tiled=True)
    out = _rmsnorm(xs, final_norm_g.reshape(1, d), jnp.float32)
    return out.reshape(batch, seq, d)
```

```python
import functools
from typing import NamedTuple

import jax
import jax.numpy as jnp
import numpy as np
from jax import lax
from jax.experimental import pallas as pl
from jax.experimental.pallas import tpu as pltpu

D_MODEL = 4096
DEPTH = 2
HEAD_DIM = 128
N_HEADS = 16
N_KV_HEADS = 4
GROUP = N_HEADS // N_KV_HEADS
ATTN_WIDTH = N_HEADS * HEAD_DIM
KV_WIDTH = N_KV_HEADS * HEAD_DIM
BLOCK = 128
ROPE_THETA = 500000.0
ROT_DIM = HEAD_DIM // 4
POOL_WIDTH = D_MODEL // 2
POOL_WINDOWS = (2, 4, 8, 16)
N_POOL_GROUPS = len(POOL_WINDOWS)
POOL_GROUP_DIM = POOL_WIDTH // N_POOL_GROUPS
Q_OFF = 0
K_OFF = ATTN_WIDTH
V_OFF = K_OFF + KV_WIDTH
U_OFF = V_OFF + KV_WIDTH
GA_OFF = U_OFF + POOL_WIDTH
GB_OFF = GA_OFF + D_MODEL
IN_WIDTH = GB_OFF + D_MODEL
D_FF = 11008
RMS_EPS = 1e-6
MASK_NEG = -1e30

POOL_HALO = 8
POOL_CHUNK = 128
LANES = 128
SUB_M = 1024
TAIL_M = 256
MERGE_SUB_M = 256
MIB = 1024 * 1024
VMEM_BYTES = 64 * MIB
VMEM_COMPILER_RESERVE = 8 * MIB
VMEM_LIMIT_MATMUL = VMEM_BYTES - VMEM_COMPILER_RESERVE
VMEM_LIMIT_SMALL = 40 * MIB


class _Tiles(NamedTuple):
    tm: int
    tn: int
    panel_buffers: int


TILES = {
    "in_proj": _Tiles(2048, 512, 1),
    "merge": _Tiles(2048, 256, 2),
    "out_proj": _Tiles(2048, 256, 2),
    "gate_up": _Tiles(2048, 256, 1),
    "ffn_down": _Tiles(512, 512, 2),
}


def _params(semantics, vmem_bytes):
    return pltpu.CompilerParams(dimension_semantics=semantics, vmem_limit_bytes=vmem_bytes)


def _rmsnorm_kernel(x_ref, g_ref, o_ref):
    x = x_ref[...]
    ms = jnp.mean(x * x, axis=-1, keepdims=True)
    y = x * lax.rsqrt(ms + RMS_EPS)
    o_ref[...] = (y * g_ref[...]).astype(o_ref.dtype)


def _rmsnorm(x2d, g_row, out_dtype, tr=512):
    m, d = x2d.shape
    return pl.pallas_call(
        _rmsnorm_kernel,
        grid=(m // tr,),
        in_specs=[pl.BlockSpec((tr, d), lambda i: (i, 0)),
                  pl.BlockSpec((1, d), lambda i: (0, 0))],
        out_specs=pl.BlockSpec((tr, d), lambda i: (i, 0)),
        out_shape=jax.ShapeDtypeStruct((m, d), out_dtype),
        compiler_params=_params(("parallel",), VMEM_LIMIT_SMALL),
        name="rmsnorm",
    )(x2d, g_row)


def _row_chunks(tm, sub=None):
    if sub is None:
        sizes, left = [], tm
        while left > 0:
            size = min(SUB_M, left)
            while size > TAIL_M and size * 2 > left:
                size //= 2
            sizes.append(size)
            left -= size
    else:
        sizes = [min(tm, sub)] * (tm // min(tm, sub))
    starts = np.cumsum([0] + sizes[:-1])
    assert sum(sizes) == tm
    return [(int(r), int(n)) for r, n in zip(starts, sizes)]


def _row_factor(inv_ref, r, n, width):
    inv = inv_ref[r:r + n, :]
    return jnp.concatenate([inv] * (width // LANES), axis=1)


def _lane_group_sum(y2):
    part = y2[:, 0:LANES]
    for c in range(LANES, y2.shape[1], LANES):
        part = part + y2[:, c:c + LANES]
    return part


def _mm_kernel(*refs, has_residual, has_row_scale, emits_norm):
    refs = list(refs)
    x_ref, w_ref = refs[0], refs[1]
    pos = 2
    r_ref = g_ref = inv_in_ref = None
    if has_residual:
        r_ref = refs[pos]; pos += 1
    if emits_norm:
        g_ref = refs[pos]; pos += 1
    if has_row_scale:
        inv_in_ref = refs[pos]; pos += 1
    o_ref = refs[pos]; pos += 1
    if emits_norm:
        xg_ref, ss_ref = refs[pos], refs[pos + 1]

        @pl.when(pl.program_id(1) == 0)
        def _():
            ss_ref[...] = jnp.zeros(ss_ref.shape, ss_ref.dtype)

    w = w_ref[...].astype(jnp.bfloat16)
    for r, n in _row_chunks(x_ref.shape[0]):
        y = jnp.dot(x_ref[r:r + n, :], w, preferred_element_type=jnp.float32)
        if has_row_scale:
            y = y * _row_factor(inv_in_ref, r, n, y.shape[1])
        if has_residual:
            y = r_ref[r:r + n, :] + y
        o_ref[r:r + n, :] = y.astype(o_ref.dtype)
        if emits_norm:
            xg_ref[r:r + n, :] = (y * g_ref[pl.program_id(1)]).astype(xg_ref.dtype)
            ss_ref[r:r + n, :] += _lane_group_sum(y * y)

    if emits_norm:
        @pl.when(pl.program_id(1) == pl.num_programs(1) - 1)
        def _():
            total = jnp.sum(ss_ref[...], axis=-1, keepdims=True)
            inv = lax.rsqrt(total * (1.0 / D_MODEL) + RMS_EPS)
            ss_ref[...] = jnp.broadcast_to(inv, ss_ref.shape)


def _panel_spec(tm, k, buffers=1):
    return pl.BlockSpec((tm, k), lambda i, j: (i, 0), pipeline_mode=pl.Buffered(buffers))


def _matmul(x, w_stack, layer, out_dtype, name, residual=None, row_inv=None, next_gain=None):
    m, k = x.shape
    n = w_stack.shape[-1]
    tm, tn, panel_buffers = TILES[name]
    in_specs = [_panel_spec(tm, k, panel_buffers),
                pl.BlockSpec((None, k, tn), lambda i, j: (layer, 0, j))]
    args = [x, w_stack]
    if residual is not None:
        in_specs.append(pl.BlockSpec((tm, tn), lambda i, j: (i, j)))
        args.append(residual)
    if next_gain is not None:
        in_specs.append(pl.BlockSpec((n // tn, 1, tn), lambda i, j: (0, 0, 0)))
        args.append(next_gain.reshape(n // tn, 1, tn))
    if row_inv is not None:
        in_specs.append(pl.BlockSpec((tm, LANES), lambda i, j: (i, 0)))
        args.append(row_inv)
    out_specs = pl.BlockSpec((tm, tn), lambda i, j: (i, j))
    out_shape = jax.ShapeDtypeStruct((m, n), out_dtype)
    if next_gain is not None:
        out_specs = [out_specs,
                     pl.BlockSpec((tm, tn), lambda i, j: (i, j)),
                     pl.BlockSpec((tm, LANES), lambda i, j: (i, 0))]
        out_shape = [out_shape,
                     jax.ShapeDtypeStruct((m, n), jnp.bfloat16),
                     jax.ShapeDtypeStruct((m, LANES), jnp.float32)]
    return pl.pallas_call(
        functools.partial(_mm_kernel, has_residual=residual is not None,
                          has_row_scale=row_inv is not None,
                          emits_norm=next_gain is not None),
        grid=(m // tm, n // tn),
        in_specs=in_specs,
        out_specs=out_specs,
        out_shape=out_shape,
        compiler_params=_params(("parallel", "arbitrary"), VMEM_LIMIT_MATMUL),
        name=name,
    )(*args)


def _rotary_mix(t, lane):
    half = ROT_DIM // 2
    return jnp.where(lane < half, pltpu.roll(t, HEAD_DIM - half, 1), pltpu.roll(t, half, 1))


def _attn_kernel(sink_ref, q_ref, k_ref, v_ref, cq_ref, sq_ref, ck_ref, sk_ref, bias_ref,
                 o_ref, kpad_ref, vpad_ref, *, seq):
    g = pl.program_id(1)
    nb = seq // BLOCK
    zeros = jnp.zeros((BLOCK, HEAD_DIM), jnp.bfloat16)
    kpad_ref[0:BLOCK, :] = zeros
    kpad_ref[BLOCK + seq:, :] = zeros
    vpad_ref[0:BLOCK, :] = zeros
    vpad_ref[BLOCK + seq:, :] = zeros
    lane_full = lax.broadcasted_iota(jnp.int32, (seq, HEAD_DIM), 1)
    k = k_ref[...]
    k_rot = k * ck_ref[...] + _rotary_mix(k, lane_full) * sk_ref[...]
    kpad_ref[BLOCK:BLOCK + seq, :] = k_rot.astype(jnp.bfloat16)
    vpad_ref[BLOCK:BLOCK + seq, :] = v_ref[...].astype(jnp.bfloat16)

    rows = GROUP * BLOCK
    row_head = lax.broadcasted_iota(jnp.int32, (rows, 1), 0) // BLOCK
    sink = jnp.zeros((rows, 1), jnp.float32)
    for h in range(GROUP):
        sink = jnp.where(row_head == h, sink_ref[g * GROUP + h], sink)

    def body(n, carry):
        r0 = pl.multiple_of(n * BLOCK, BLOCK)
        lane_blk = lax.broadcasted_iota(jnp.int32, (BLOCK, HEAD_DIM), 1)
        cq = cq_ref[pl.ds(r0, BLOCK), :]
        sq = sq_ref[pl.ds(r0, BLOCK), :]
        heads = []
        for h in range(GROUP):
            qh = q_ref[pl.ds(r0, BLOCK), h * HEAD_DIM:(h + 1) * HEAD_DIM]
            heads.append((qh * cq + _rotary_mix(qh, lane_blk) * sq).astype(jnp.bfloat16))
        q = jnp.concatenate(heads, axis=0)
        kw = kpad_ref[pl.ds(r0, 3 * BLOCK), :]
        vw = vpad_ref[pl.ds(r0, 3 * BLOCK), :]
        s = lax.dot_general(q, kw, (((1,), (1,)), ((), ())),
                            preferred_element_type=jnp.float32)
        which = jnp.where(n == 0, 0, jnp.where(n == nb - 1, 2, 1))
        s = s + jnp.concatenate([bias_ref[which]] * GROUP, axis=0)
        m = jnp.maximum(jnp.max(s, axis=-1, keepdims=True), sink)
        p = jnp.exp(s - m)
        denom = jnp.sum(p, axis=-1, keepdims=True) + jnp.exp(sink - m)
        o = jnp.dot(p.astype(jnp.bfloat16), vw, preferred_element_type=jnp.float32)
        o = o / denom
        for h in range(GROUP):
            o_ref[pl.ds(r0, BLOCK), h * HEAD_DIM:(h + 1) * HEAD_DIM] = (
                o[h * BLOCK:(h + 1) * BLOCK, :].astype(o_ref.dtype))
        return carry

    lax.fori_loop(0, nb, body, 0, unroll=2)


def _window_mask_bias():
    qi = np.arange(BLOCK)[:, None]
    kj = np.arange(3 * BLOCK)[None, :]
    band = (kj >= qi) & (kj <= qi + 2 * BLOCK)
    first = band & (kj >= BLOCK)
    last = band & (kj < 2 * BLOCK)
    return np.where(np.stack([first, band, last]), 0.0, MASK_NEG).astype(np.float32)


def _attention(proj, sink_l, tables, batch, seq):
    cq, sq, ck, sk = tables
    m = proj.shape[0]
    qw = GROUP * HEAD_DIM
    assert seq // BLOCK >= 2
    bias = jnp.asarray(_window_mask_bias())
    tab_spec = pl.BlockSpec((seq, HEAD_DIM), lambda b, g: (0, 0))
    return pl.pallas_call(
        functools.partial(_attn_kernel, seq=seq),
        grid=(batch, N_KV_HEADS),
        in_specs=[pl.BlockSpec(memory_space=pltpu.SMEM),
                  pl.BlockSpec((seq, qw), lambda b, g: (b, Q_OFF // qw + g)),
                  pl.BlockSpec((seq, HEAD_DIM), lambda b, g: (b, K_OFF // HEAD_DIM + g)),
                  pl.BlockSpec((seq, HEAD_DIM), lambda b, g: (b, V_OFF // HEAD_DIM + g)),
                  tab_spec, tab_spec, tab_spec, tab_spec,
                  pl.BlockSpec(bias.shape, lambda b, g: (0, 0, 0))],
        out_specs=pl.BlockSpec((seq, qw), lambda b, g: (b, g)),
        out_shape=jax.ShapeDtypeStruct((m, ATTN_WIDTH), jnp.bfloat16),
        scratch_shapes=[pltpu.VMEM((seq + 2 * BLOCK, HEAD_DIM), jnp.bfloat16),
                        pltpu.VMEM((seq + 2 * BLOCK, HEAD_DIM), jnp.bfloat16)],
        compiler_params=_params(("parallel", "arbitrary"), VMEM_LIMIT_SMALL),
        name="window_gqa",
    )(sink_l, proj, proj, proj, cq, sq, ck, sk, bias)


def _pool_kernel(u_ref, w_ref, scale_ref, o_ref, pad_ref, *, seq):
    g = pl.program_id(1)
    zeros = jnp.zeros((POOL_HALO, POOL_GROUP_DIM), jnp.float32)
    pad_ref[0:POOL_HALO, :] = zeros
    pad_ref[POOL_HALO + seq:, :] = zeros
    pad_ref[POOL_HALO:POOL_HALO + seq, :] = u_ref[...]
    wmat = w_ref[...].astype(jnp.bfloat16)
    scale = scale_ref[...]

    def run(win):
        half = win // 2
        ext = POOL_CHUNK + 2 * POOL_HALO

        def body(c, carry):
            t0 = pl.multiple_of(c * POOL_CHUNK, POOL_CHUNK)
            xe = pad_ref[pl.ds(t0, ext), :]
            a = xe
            step = 1
            while step < half:
                a = a + pltpu.roll(a, ext - step, 0)
                step *= 2
            a = a + pltpu.roll(a, half, 0)
            wsum = a[POOL_HALO:POOL_HALO + POOL_CHUNK, :]
            x = xe[POOL_HALO:POOL_HALO + POOL_CHUNK, :]
            t = t0 + lax.broadcasted_iota(jnp.int32, (POOL_CHUNK, 1), 0)
            lo = jnp.maximum(t - half, 0)
            hi = jnp.minimum(t + half - 1, seq - 1)
            cnt = (hi - lo + 1).astype(jnp.float32)
            pooled = wsum / cnt - x
            mixed = jnp.dot(pooled.astype(jnp.bfloat16), wmat,
                            preferred_element_type=jnp.float32)
            o_ref[pl.ds(t0, POOL_CHUNK), :] = (mixed * scale).astype(o_ref.dtype)
            return carry

        lax.fori_loop(0, seq // POOL_CHUNK, body, 0, unroll=2)

    for gi, win in enumerate(POOL_WINDOWS):
        pl.when(g == gi)(functools.partial(run, win))


def _pool(proj, pool_w, pool_scale3, layer, batch, seq):
    m = proj.shape[0]
    c = POOL_GROUP_DIM
    return pl.pallas_call(
        functools.partial(_pool_kernel, seq=seq),
        grid=(batch, N_POOL_GROUPS),
        in_specs=[pl.BlockSpec((seq, c), lambda b, g: (b, U_OFF // c + g)),
                  pl.BlockSpec((None, None, c, c), lambda b, g: (layer, g, 0, 0)),
                  pl.BlockSpec((None, 1, c), lambda b, g: (layer, 0, g))],
        out_specs=pl.BlockSpec((seq, c), lambda b, g: (b, g)),
        out_shape=jax.ShapeDtypeStruct((m, POOL_WIDTH), jnp.bfloat16),
        scratch_shapes=[pltpu.VMEM((seq + 2 * POOL_HALO, c), jnp.float32)],
        compiler_params=_params(("parallel", "arbitrary"), VMEM_LIMIT_SMALL),
        name="multiscale_pool",
    )(proj, pool_w, pool_scale3)


def _merge_kernel(a_ref, p_ref, wa_ref, wp_ref, ga_ref, gb_ref, o_ref):
    wa = wa_ref[...].astype(jnp.bfloat16)
    wp = wp_ref[...].astype(jnp.bfloat16)
    for r, n in _row_chunks(a_ref.shape[0], MERGE_SUB_M):
        ya = jnp.dot(a_ref[r:r + n, :], wa, preferred_element_type=jnp.float32)
        yp = jnp.dot(p_ref[r:r + n, :], wp, preferred_element_type=jnp.float32)
        merged = (jax.nn.sigmoid(ga_ref[r:r + n, :]) * ya
                  + jax.nn.sigmoid(gb_ref[r:r + n, :]) * yp)
        o_ref[r:r + n, :] = merged.astype(o_ref.dtype)


def _merge(o_attn, mixed, wba, wbp, proj, layer):
    m = o_attn.shape[0]
    tm, tn, panel_buffers = TILES["merge"]
    return pl.pallas_call(
        _merge_kernel,
        grid=(m // tm, D_MODEL // tn),
        in_specs=[_panel_spec(tm, ATTN_WIDTH, panel_buffers),
                  _panel_spec(tm, POOL_WIDTH, panel_buffers),
                  pl.BlockSpec((None, ATTN_WIDTH, tn), lambda i, j: (layer, 0, j)),
                  pl.BlockSpec((None, POOL_WIDTH, tn), lambda i, j: (layer, 0, j)),
                  pl.BlockSpec((tm, tn), lambda i, j: (i, GA_OFF // tn + j)),
                  pl.BlockSpec((tm, tn), lambda i, j: (i, GB_OFF // tn + j))],
        out_specs=pl.BlockSpec((tm, tn), lambda i, j: (i, j)),
        out_shape=jax.ShapeDtypeStruct((m, D_MODEL), jnp.bfloat16),
        compiler_params=_params(("parallel", "arbitrary"), VMEM_LIMIT_MATMUL),
        name="branch_merge",
    )(o_attn, mixed, wba, wbp, proj, proj)


def _gate_up_kernel(x_ref, inv_ref, wg_ref, wu_ref, wd_ref, o_ref, wd_bf16_ref):
    wd_bf16_ref[...] = wd_ref[...].astype(jnp.bfloat16)
    wg = wg_ref[...].astype(jnp.bfloat16)
    wu = wu_ref[...].astype(jnp.bfloat16)
    for r, n in _row_chunks(x_ref.shape[0]):
        x = x_ref[r:r + n, :]
        inv = _row_factor(inv_ref, r, n, wg.shape[1])
        gate = jnp.dot(x, wg, preferred_element_type=jnp.float32) * inv
        up = jnp.dot(x, wu, preferred_element_type=jnp.float32) * inv
        o_ref[r:r + n, :] = (jax.nn.silu(gate) * up).astype(o_ref.dtype)


def _gate_up(xg, row_inv, w_gate_up, w_down, layer):
    m, k = xg.shape
    tm, tn, panel_buffers = TILES["gate_up"]
    nt = D_FF // tn
    steps = (m // tm) * nt
    slab = D_FF // steps
    assert slab * steps == D_FF and slab % 16 == 0
    return pl.pallas_call(
        _gate_up_kernel,
        grid=(m // tm, nt),
        in_specs=[_panel_spec(tm, k, panel_buffers),
                  pl.BlockSpec((tm, LANES), lambda i, j: (i, 0)),
                  pl.BlockSpec((None, k, tn), lambda i, j: (layer, 0, j)),
                  pl.BlockSpec((None, k, tn), lambda i, j: (layer, 0, nt + j)),
                  pl.BlockSpec((None, slab, D_MODEL), lambda i, j: (layer, i * nt + j, 0))],
        out_specs=[pl.BlockSpec((tm, tn), lambda i, j: (i, j)),
                   pl.BlockSpec((slab, D_MODEL), lambda i, j: (i * nt + j, 0))],
        out_shape=[jax.ShapeDtypeStruct((m, D_FF), jnp.bfloat16),
                   jax.ShapeDtypeStruct((D_FF, D_MODEL), jnp.bfloat16)],
        compiler_params=_params(("arbitrary", "arbitrary"), VMEM_LIMIT_MATMUL),
        name="swiglu_gate_up",
    )(xg, row_inv, w_gate_up, w_gate_up, w_down)


def _rotary_tables(seq):
    pos = jnp.arange(seq, dtype=jnp.float32)
    inv_freq = 1.0 / jnp.power(jnp.float32(ROPE_THETA),
                               jnp.arange(0, ROT_DIM, 2, dtype=jnp.float32) / ROT_DIM)
    ang = pos[:, None] * inv_freq[None, :]
    cos, sin = jnp.cos(ang), jnp.sin(ang)
    ones = jnp.ones((seq, HEAD_DIM - ROT_DIM), jnp.float32)
    zeros = jnp.zeros((seq, HEAD_DIM - ROT_DIM), jnp.float32)
    c_full = jnp.concatenate([cos, cos, ones], axis=-1)
    s_full = jnp.concatenate([-sin, sin, zeros], axis=-1)
    scale = HEAD_DIM ** -0.5
    return c_full * scale, s_full * scale, c_full, s_full


def kernel(x, norm1_g, w_in, attn_sink, pool_w, pool_scale, w_branch_attn, w_branch_pool,
           w_out, norm2_g, w_gate_up, w_down, final_norm_g):
    batch, seq, d = x.shape
    m = batch * seq
    bf16 = jnp.bfloat16
    tables = _rotary_tables(seq)
    pool_scale3 = pool_scale.reshape(DEPTH, 1, POOL_WIDTH)

    xs = x.reshape(m, d)
    h = _rmsnorm(xs, norm1_g[0].reshape(1, d), bf16)
    row_inv = None
    for l in range(DEPTH):
        proj = _matmul(h, w_in, l, jnp.float32, "in_proj", row_inv=row_inv)
        o_attn = _attention(proj, attn_sink[l], tables, batch, seq)
        mixed = _pool(proj, pool_w, pool_scale3, l, batch, seq)
        merged = _merge(o_attn, mixed, w_branch_attn, w_branch_pool, proj, l)
        xs, xg, row_inv = _matmul(merged, w_out, l, jnp.float32, "out_proj", residual=xs,
                                  next_gain=norm2_g[l].reshape(1, d))
        act, w_down_b = _gate_up(xg, row_inv, w_gate_up, w_down, l)
        if l + 1 < DEPTH:
            xs, h, row_inv = _matmul(act, w_down_b[None], 0, jnp.float32, "ffn_down",
                                     residual=xs, next_gain=norm1_g[l + 1].reshape(1, d))
        else:
            xs = _matmul(act, w_down_b[None], 0, jnp.float32, "ffn_down", residual=xs)
    out = _rmsnorm(xs, final_norm_g.reshape(1, d), jnp.float32)
    return out.reshape(batch, seq, d)
```

```python
import functools
from typing import NamedTuple

import jax
import jax.numpy as jnp
import numpy as np
from jax import lax
from jax.experimental import pallas as pl
from jax.experimental.pallas import tpu as pltpu

D_MODEL = 4096
DEPTH = 2
HEAD_DIM = 128
N_HEADS = 16
N_KV_HEADS = 4
GROUP = N_HEADS // N_KV_HEADS
ATTN_WIDTH = N_HEADS * HEAD_DIM
KV_WIDTH = N_KV_HEADS * HEAD_DIM
BLOCK = 128
ROPE_THETA = 500000.0
ROT_DIM = HEAD_DIM // 4
POOL_WIDTH = D_MODEL // 2
POOL_WINDOWS = (2, 4, 8, 16)
N_POOL_GROUPS = len(POOL_WINDOWS)
POOL_GROUP_DIM = POOL_WIDTH // N_POOL_GROUPS
Q_OFF = 0
K_OFF = ATTN_WIDTH
V_OFF = K_OFF + KV_WIDTH
U_OFF = V_OFF + KV_WIDTH
GA_OFF = U_OFF + POOL_WIDTH
GB_OFF = GA_OFF + D_MODEL
IN_WIDTH = GB_OFF + D_MODEL
D_FF = 11008
RMS_EPS = 1e-6
MASK_NEG = -1e30

POOL_HALO = 8
POOL_CHUNK = 128
LANES = 128
SUB_M = 1024
TAIL_M = 256
MERGE_SUB_M = 256
MIB = 1024 * 1024
VMEM_BYTES = 64 * MIB
VMEM_COMPILER_RESERVE = 8 * MIB
VMEM_LIMIT_MATMUL = VMEM_BYTES - VMEM_COMPILER_RESERVE
VMEM_LIMIT_SMALL = 40 * MIB


class _Tiles(NamedTuple):
    tm: int
    tn: int
    panel_buffers: int


TILES = {
    "in_proj": _Tiles(2048, 512, 1),
    "merge": _Tiles(2048, 256, 2),
    "out_proj": _Tiles(2048, 256, 2),
    "gate_up": _Tiles(2048, 256, 1),
    "ffn_down": _Tiles(512, 512, 2),
}


def _params(semantics, vmem_bytes):
    return pltpu.CompilerParams(dimension_semantics=semantics, vmem_limit_bytes=vmem_bytes)


def _rmsnorm_kernel(x_ref, g_ref, o_ref):
    x = x_ref[...]
    ms = jnp.mean(x * x, axis=-1, keepdims=True)
    y = x * lax.rsqrt(ms + RMS_EPS)
    o_ref[...] = (y * g_ref[...]).astype(o_ref.dtype)


def _rmsnorm(x2d, g_row, out_dtype, tr=256):
    m, d = x2d.shape
    return pl.pallas_call(
        _rmsnorm_kernel,
        grid=(m // tr,),
        in_specs=[pl.BlockSpec((tr, d), lambda i: (i, 0)),
                  pl.BlockSpec((1, d), lambda i: (0, 0))],
        out_specs=pl.BlockSpec((tr, d), lambda i: (i, 0)),
        out_shape=jax.ShapeDtypeStruct((m, d), out_dtype),
        compiler_params=_params(("parallel",), VMEM_LIMIT_SMALL),
        name="rmsnorm",
    )(x2d, g_row)


def _row_chunks(tm, sub=None):
    if sub is None:
        sizes, left = [], tm
        while left > 0:
            size = min(SUB_M, left)
            while size > TAIL_M and size * 2 > left:
                size //= 2
            sizes.append(size)
            left -= size
    else:
        sizes = [min(tm, sub)] * (tm // min(tm, sub))
    starts = np.cumsum([0] + sizes[:-1])
    assert sum(sizes) == tm
    return [(int(r), int(n)) for r, n in zip(starts, sizes)]


def _row_factor(inv_ref, r, n, width):
    inv = inv_ref[r:r + n, :]
    return jnp.concatenate([inv] * (width // LANES), axis=1)


def _lane_group_sum(y2):
    part = y2[:, 0:LANES]
    for c in range(LANES, y2.shape[1], LANES):
        part = part + y2[:, c:c + LANES]
    return part


def _mm_kernel(*refs, has_residual, has_row_scale, emits_norm):
    refs = list(refs)
    x_ref, w_ref = refs[0], refs[1]
    pos = 2
    r_ref = g_ref = inv_in_ref = None
    if has_residual:
        r_ref = refs[pos]; pos += 1
    if emits_norm:
        g_ref = refs[pos]; pos += 1
    if has_row_scale:
        inv_in_ref = refs[pos]; pos += 1
    o_ref = refs[pos]; pos += 1
    if emits_norm:
        xg_ref, ss_ref = refs[pos], refs[pos + 1]

        @pl.when(pl.program_id(1) == 0)
        def _():
            ss_ref[...] = jnp.zeros(ss_ref.shape, ss_ref.dtype)

    w = w_ref[...].astype(jnp.bfloat16)
    for r, n in _row_chunks(x_ref.shape[0]):
        y = jnp.dot(x_ref[r:r + n, :], w, preferred_element_type=jnp.float32)
        if has_row_scale:
            y = y * _row_factor(inv_in_ref, r, n, y.shape[1])
        if has_residual:
            y = r_ref[r:r + n, :] + y
        o_ref[r:r + n, :] = y.astype(o_ref.dtype)
        if emits_norm:
            xg_ref[r:r + n, :] = (y * g_ref[...]).astype(xg_ref.dtype)
            ss_ref[r:r + n, :] += _lane_group_sum(y * y)

    if emits_norm:
        @pl.when(pl.program_id(1) == pl.num_programs(1) - 1)
        def _():
            total = jnp.sum(ss_ref[...], axis=-1, keepdims=True)
            inv = lax.rsqrt(total * (1.0 / D_MODEL) + RMS_EPS)
            ss_ref[...] = jnp.broadcast_to(inv, ss_ref.shape)


def _panel_spec(tm, k, buffers=1):
    return pl.BlockSpec((tm, k), lambda i, j: (i, 0), pipeline_mode=pl.Buffered(buffers))


def _matmul(x, w_stack, layer, out_dtype, name, residual=None, row_inv=None, next_gain=None):
    m, k = x.shape
    n = w_stack.shape[-1]
    tm, tn, panel_buffers = TILES[name]
    in_specs = [_panel_spec(tm, k, panel_buffers),
                pl.BlockSpec((None, k, tn), lambda i, j: (layer, 0, j))]
    args = [x, w_stack]
    if residual is not None:
        in_specs.append(pl.BlockSpec((tm, tn), lambda i, j: (i, j)))
        args.append(residual)
    if next_gain is not None:
        in_specs.append(pl.BlockSpec((1, tn), lambda i, j: (0, j)))
        args.append(next_gain)
    if row_inv is not None:
        in_specs.append(pl.BlockSpec((tm, LANES), lambda i, j: (i, 0)))
        args.append(row_inv)
    out_specs = pl.BlockSpec((tm, tn), lambda i, j: (i, j))
    out_shape = jax.ShapeDtypeStruct((m, n), out_dtype)
    if next_gain is not None:
        out_specs = [out_specs,
                     pl.BlockSpec((tm, tn), lambda i, j: (i, j)),
                     pl.BlockSpec((tm, LANES), lambda i, j: (i, 0))]
        out_shape = [out_shape,
                     jax.ShapeDtypeStruct((m, n), jnp.bfloat16),
                     jax.ShapeDtypeStruct((m, LANES), jnp.float32)]
    return pl.pallas_call(
        functools.partial(_mm_kernel, has_residual=residual is not None,
                          has_row_scale=row_inv is not None,
                          emits_norm=next_gain is not None),
        grid=(m // tm, n // tn),
        in_specs=in_specs,
        out_specs=out_specs,
        out_shape=out_shape,
        compiler_params=_params(("parallel", "arbitrary"), VMEM_LIMIT_MATMUL),
        name=name,
    )(*args)


def _rotary_mix(t, lane):
    half = ROT_DIM // 2
    return jnp.where(lane < half, pltpu.roll(t, HEAD_DIM - half, 1), pltpu.roll(t, half, 1))


def _attn_kernel(sink_ref, q_ref, k_ref, v_ref, cq_ref, sq_ref, ck_ref, sk_ref, bias_ref,
                 o_ref, kpad_ref, vpad_ref, *, seq):
    g = pl.program_id(1)
    nb = seq // BLOCK
    zeros = jnp.zeros((BLOCK, HEAD_DIM), jnp.bfloat16)
    kpad_ref[0:BLOCK, :] = zeros
    kpad_ref[BLOCK + seq:, :] = zeros
    vpad_ref[0:BLOCK, :] = zeros
    vpad_ref[BLOCK + seq:, :] = zeros
    lane_full = lax.broadcasted_iota(jnp.int32, (seq, HEAD_DIM), 1)
    k = k_ref[...]
    k_rot = k * ck_ref[...] + _rotary_mix(k, lane_full) * sk_ref[...]
    kpad_ref[BLOCK:BLOCK + seq, :] = k_rot.astype(jnp.bfloat16)
    vpad_ref[BLOCK:BLOCK + seq, :] = v_ref[...].astype(jnp.bfloat16)

    rows = GROUP * BLOCK
    row_head = lax.broadcasted_iota(jnp.int32, (rows, 1), 0) // BLOCK
    sink = jnp.zeros((rows, 1), jnp.float32)
    for h in range(GROUP):
        sink = jnp.where(row_head == h, sink_ref[g * GROUP + h], sink)

    def body(n, carry):
        r0 = pl.multiple_of(n * BLOCK, BLOCK)
        lane_blk = lax.broadcasted_iota(jnp.int32, (BLOCK, HEAD_DIM), 1)
        cq = cq_ref[pl.ds(r0, BLOCK), :]
        sq = sq_ref[pl.ds(r0, BLOCK), :]
        heads = []
        for h in range(GROUP):
            qh = q_ref[pl.ds(r0, BLOCK), h * HEAD_DIM:(h + 1) * HEAD_DIM]
            heads.append((qh * cq + _rotary_mix(qh, lane_blk) * sq).astype(jnp.bfloat16))
        q = jnp.concatenate(heads, axis=0)
        kw = kpad_ref[pl.ds(r0, 3 * BLOCK), :]
        vw = vpad_ref[pl.ds(r0, 3 * BLOCK), :]
        s = lax.dot_general(q, kw, (((1,), (1,)), ((), ())),
                            preferred_element_type=jnp.float32)
        which = jnp.where(n == 0, 0, jnp.where(n == nb - 1, 2, 1))
        s = s + jnp.concatenate([bias_ref[which]] * GROUP, axis=0)
        m = jnp.maximum(jnp.max(s, axis=-1, keepdims=True), sink)
        p = jnp.exp(s - m)
        denom = jnp.sum(p, axis=-1, keepdims=True) + jnp.exp(sink - m)
        o = jnp.dot(p.astype(jnp.bfloat16), vw, preferred_element_type=jnp.float32)
        o = o / denom
        for h in range(GROUP):
            o_ref[pl.ds(r0, BLOCK), h * HEAD_DIM:(h + 1) * HEAD_DIM] = (
                o[h * BLOCK:(h + 1) * BLOCK, :].astype(o_ref.dtype))
        return carry

    lax.fori_loop(0, nb, body, 0, unroll=2)


def _window_mask_bias():
    qi = np.arange(BLOCK)[:, None]
    kj = np.arange(3 * BLOCK)[None, :]
    band = (kj >= qi) & (kj <= qi + 2 * BLOCK)
    first = band & (kj >= BLOCK)
    last = band & (kj < 2 * BLOCK)
    return np.where(np.stack([first, band, last]), 0.0, MASK_NEG).astype(np.float32)


def _attention(proj, sink_l, tables, batch, seq):
    cq, sq, ck, sk = tables
    m = proj.shape[0]
    qw = GROUP * HEAD_DIM
    assert seq // BLOCK >= 2
    bias = jnp.asarray(_window_mask_bias())
    tab_spec = pl.BlockSpec((seq, HEAD_DIM), lambda b, g: (0, 0))
    return pl.pallas_call(
        functools.partial(_attn_kernel, seq=seq),
        grid=(batch, N_KV_HEADS),
        in_specs=[pl.BlockSpec(memory_space=pltpu.SMEM),
                  pl.BlockSpec((seq, qw), lambda b, g: (b, Q_OFF // qw + g)),
                  pl.BlockSpec((seq, HEAD_DIM), lambda b, g: (b, K_OFF // HEAD_DIM + g)),
                  pl.BlockSpec((seq, HEAD_DIM), lambda b, g: (b, V_OFF // HEAD_DIM + g)),
                  tab_spec, tab_spec, tab_spec, tab_spec,
                  pl.BlockSpec(bias.shape, lambda b, g: (0, 0, 0))],
        out_specs=pl.BlockSpec((seq, qw), lambda b, g: (b, g)),
        out_shape=jax.ShapeDtypeStruct((m, ATTN_WIDTH), jnp.bfloat16),
        scratch_shapes=[pltpu.VMEM((seq + 2 * BLOCK, HEAD_DIM), jnp.bfloat16),
                        pltpu.VMEM((seq + 2 * BLOCK, HEAD_DIM), jnp.bfloat16)],
        compiler_params=_params(("parallel", "arbitrary"), VMEM_LIMIT_SMALL),
        name="window_gqa",
    )(sink_l, proj, proj, proj, cq, sq, ck, sk, bias)


def _pool_kernel(u_ref, w_ref, scale_ref, o_ref, pad_ref, *, seq):
    g = pl.program_id(1)
    zeros = jnp.zeros((POOL_HALO, POOL_GROUP_DIM), jnp.float32)
    pad_ref[0:POOL_HALO, :] = zeros
    pad_ref[POOL_HALO + seq:, :] = zeros
    pad_ref[POOL_HALO:POOL_HALO + seq, :] = u_ref[...]
    wmat = w_ref[...].astype(jnp.bfloat16)
    scale = scale_ref[...]

    def run(win):
        half = win // 2
        ext = POOL_CHUNK + 2 * POOL_HALO

        def body(c, carry):
            t0 = pl.multiple_of(c * POOL_CHUNK, POOL_CHUNK)
            xe = pad_ref[pl.ds(t0, ext), :]
            a = xe
            step = 1
            while step < half:
                a = a + pltpu.roll(a, ext - step, 0)
                step *= 2
            a = a + pltpu.roll(a, half, 0)
            wsum = a[POOL_HALO:POOL_HALO + POOL_CHUNK, :]
            x = xe[POOL_HALO:POOL_HALO + POOL_CHUNK, :]
            t = t0 + lax.broadcasted_iota(jnp.int32, (POOL_CHUNK, 1), 0)
            lo = jnp.maximum(t - half, 0)
            hi = jnp.minimum(t + half - 1, seq - 1)
            cnt = (hi - lo + 1).astype(jnp.float32)
            pooled = wsum / cnt - x
            mixed = jnp.dot(pooled.astype(jnp.bfloat16), wmat,
                            preferred_element_type=jnp.float32)
            o_ref[pl.ds(t0, POOL_CHUNK), :] = (mixed * scale).astype(o_ref.dtype)
            return carry

        lax.fori_loop(0, seq // POOL_CHUNK, body, 0, unroll=2)

    for gi, win in enumerate(POOL_WINDOWS):
        pl.when(g == gi)(functools.partial(run, win))


def _pool(proj, pool_w, pool_scale3, layer, batch, seq):
    m = proj.shape[0]
    c = POOL_GROUP_DIM
    return pl.pallas_call(
        functools.partial(_pool_kernel, seq=seq),
        grid=(batch, N_POOL_GROUPS),
        in_specs=[pl.BlockSpec((seq, c), lambda b, g: (b, U_OFF // c + g)),
                  pl.BlockSpec((None, None, c, c), lambda b, g: (layer, g, 0, 0)),
                  pl.BlockSpec((None, 1, c), lambda b, g: (layer, 0, g))],
        out_specs=pl.BlockSpec((seq, c), lambda b, g: (b, g)),
        out_shape=jax.ShapeDtypeStruct((m, POOL_WIDTH), jnp.bfloat16),
        scratch_shapes=[pltpu.VMEM((seq + 2 * POOL_HALO, c), jnp.float32)],
        compiler_params=_params(("parallel", "arbitrary"), VMEM_LIMIT_SMALL),
        name="multiscale_pool",
    )(proj, pool_w, pool_scale3)


def _merge_kernel(a_ref, p_ref, wa_ref, wp_ref, ga_ref, gb_ref, o_ref):
    wa = wa_ref[...].astype(jnp.bfloat16)
    wp = wp_ref[...].astype(jnp.bfloat16)
    for r, n in _row_chunks(a_ref.shape[0], MERGE_SUB_M):
        ya = jnp.dot(a_ref[r:r + n, :], wa, preferred_element_type=jnp.float32)
        yp = jnp.dot(p_ref[r:r + n, :], wp, preferred_element_type=jnp.float32)
        merged = (jax.nn.sigmoid(ga_ref[r:r + n, :]) * ya
                  + jax.nn.sigmoid(gb_ref[r:r + n, :]) * yp)
        o_ref[r:r + n, :] = merged.astype(o_ref.dtype)


def _merge(o_attn, mixed, wba, wbp, proj, layer):
    m = o_attn.shape[0]
    tm, tn, panel_buffers = TILES["merge"]
    grid = (m // tm, D_MODEL // tn)
    panel_mode = pl.Buffered(panel_buffers, use_lookahead=True)
    in_specs = [pl.BlockSpec((tm, ATTN_WIDTH), lambda i, j: (i, 0), pipeline_mode=panel_mode),
                pl.BlockSpec((tm, POOL_WIDTH), lambda i, j: (i, 0), pipeline_mode=panel_mode),
                pl.BlockSpec((None, ATTN_WIDTH, tn), lambda i, j: (layer, 0, j)),
                pl.BlockSpec((None, POOL_WIDTH, tn), lambda i, j: (layer, 0, j)),
                pl.BlockSpec((tm, tn), lambda i, j: (i, GA_OFF // tn + j)),
                pl.BlockSpec((tm, tn), lambda i, j: (i, GB_OFF // tn + j))]
    out_specs = [pl.BlockSpec((tm, tn), lambda i, j: (i, j))]

    def pipelined(*hbm_refs):
        pltpu.emit_pipeline(_merge_kernel, grid=grid, in_specs=in_specs,
                            out_specs=out_specs)(*hbm_refs)

    any_spec = pl.BlockSpec(memory_space=pl.ANY)
    return pl.pallas_call(
        pipelined,
        in_specs=[any_spec] * 6,
        out_specs=any_spec,
        out_shape=jax.ShapeDtypeStruct((m, D_MODEL), jnp.bfloat16),
        compiler_params=pltpu.CompilerParams(vmem_limit_bytes=VMEM_LIMIT_MATMUL),
        name="branch_merge",
    )(o_attn, mixed, wba, wbp, proj, proj)


def _gate_up_kernel(x_ref, inv_ref, wg_ref, wu_ref, wd_ref, o_ref, wd_bf16_ref):
    wd_bf16_ref[...] = wd_ref[...].astype(jnp.bfloat16)
    wg = wg_ref[...].astype(jnp.bfloat16)
    wu = wu_ref[...].astype(jnp.bfloat16)
    for r, n in _row_chunks(x_ref.shape[0]):
        x = x_ref[r:r + n, :]
        inv = _row_factor(inv_ref, r, n, wg.shape[1])
        gate = jnp.dot(x, wg, preferred_element_type=jnp.float32) * inv
        up = jnp.dot(x, wu, preferred_element_type=jnp.float32) * inv
        o_ref[r:r + n, :] = (jax.nn.silu(gate) * up).astype(o_ref.dtype)


def _gate_up(xg, row_inv, w_gate_up, w_down, layer):
    m, k = xg.shape
    tm, tn, panel_buffers = TILES["gate_up"]
    nt = D_FF // tn
    steps = (m // tm) * nt
    slab = D_FF // steps
    assert slab * steps == D_FF and slab % 16 == 0
    return pl.pallas_call(
        _gate_up_kernel,
        grid=(m // tm, nt),
        in_specs=[_panel_spec(tm, k, panel_buffers),
                  pl.BlockSpec((tm, LANES), lambda i, j: (i, 0)),
                  pl.BlockSpec((None, k, tn), lambda i, j: (layer, 0, j)),
                  pl.BlockSpec((None, k, tn), lambda i, j: (layer, 0, nt + j)),
                  pl.BlockSpec((None, slab, D_MODEL), lambda i, j: (layer, i * nt + j, 0))],
        out_specs=[pl.BlockSpec((tm, tn), lambda i, j: (i, j)),
                   pl.BlockSpec((slab, D_MODEL), lambda i, j: (i * nt + j, 0))],
        out_shape=[jax.ShapeDtypeStruct((m, D_FF), jnp.bfloat16),
                   jax.ShapeDtypeStruct((D_FF, D_MODEL), jnp.bfloat16)],
        compiler_params=_params(("arbitrary", "arbitrary"), VMEM_LIMIT_MATMUL),
        name="swiglu_gate_up",
    )(xg, row_inv, w_gate_up, w_gate_up, w_down)


def _rotary_tables(seq):
    pos = jnp.arange(seq, dtype=jnp.float32)
    inv_freq = 1.0 / jnp.power(jnp.float32(ROPE_THETA),
                               jnp.arange(0, ROT_DIM, 2, dtype=jnp.float32) / ROT_DIM)
    ang = pos[:, None] * inv_freq[None, :]
    cos, sin = jnp.cos(ang), jnp.sin(ang)
    ones = jnp.ones((seq, HEAD_DIM - ROT_DIM), jnp.float32)
    zeros = jnp.zeros((seq, HEAD_DIM - ROT_DIM), jnp.float32)
    c_full = jnp.concatenate([cos, cos, ones], axis=-1)
    s_full = jnp.concatenate([-sin, sin, zeros], axis=-1)
    scale = HEAD_DIM ** -0.5
    return c_full * scale, s_full * scale, c_full, s_full


def kernel(x, norm1_g, w_in, attn_sink, pool_w, pool_scale, w_branch_attn, w_branch_pool,
           w_out, norm2_g, w_gate_up, w_down, final_norm_g):
    batch, seq, d = x.shape
    m = batch * seq
    bf16 = jnp.bfloat16
    tables = _rotary_tables(seq)
    pool_scale3 = pool_scale.reshape(DEPTH, 1, POOL_WIDTH)

    xs = x.reshape(m, d)
    h = _rmsnorm(xs, norm1_g[0].reshape(1, d), bf16)
    row_inv = None
    for l in range(DEPTH):
        proj = _matmul(h, w_in, l, jnp.float32, "in_proj", row_inv=row_inv)
        o_attn = _attention(proj, attn_sink[l], tables, batch, seq)
        mixed = _pool(proj, pool_w, pool_scale3, l, batch, seq)
        merged = _merge(o_attn, mixed, w_branch_attn, w_branch_pool, proj, l)
        xs, xg, row_inv = _matmul(merged, w_out, l, jnp.float32, "out_proj", residual=xs,
                                  next_gain=norm2_g[l].reshape(1, d))
        act, w_down_b = _gate_up(xg, row_inv, w_gate_up, w_down, l)
        if l + 1 < DEPTH:
            xs, h, row_inv = _matmul(act, w_down_b[None], 0, jnp.float32, "ffn_down",
                                     residual=xs, next_gain=norm1_g[l + 1].reshape(1, d))
        else:
            xs = _matmul(act, w_down_b[None], 0, jnp.float32, "ffn_down", residual=xs)
    out = _rmsnorm(xs, final_norm_g.reshape(1, d), jnp.float32)
    return out.reshape(batch, seq, d)
```

```python
import functools
from typing import NamedTuple

import jax
import jax.numpy as jnp
import numpy as np
from jax import lax
from jax.experimental import pallas as pl
from jax.experimental.pallas import tpu as pltpu

D_MODEL = 4096
DEPTH = 2
HEAD_DIM = 128
N_HEADS = 16
N_KV_HEADS = 4
GROUP = N_HEADS // N_KV_HEADS
ATTN_WIDTH = N_HEADS * HEAD_DIM
KV_WIDTH = N_KV_HEADS * HEAD_DIM
BLOCK = 128
ROPE_THETA = 500000.0
ROT_DIM = HEAD_DIM // 4
POOL_WIDTH = D_MODEL // 2
POOL_WINDOWS = (2, 4, 8, 16)
N_POOL_GROUPS = len(POOL_WINDOWS)
POOL_GROUP_DIM = POOL_WIDTH // N_POOL_GROUPS
Q_OFF = 0
K_OFF = ATTN_WIDTH
V_OFF = K_OFF + KV_WIDTH
U_OFF = V_OFF + KV_WIDTH
GA_OFF = U_OFF + POOL_WIDTH
GB_OFF = GA_OFF + D_MODEL
IN_WIDTH = GB_OFF + D_MODEL
D_FF = 11008
RMS_EPS = 1e-6
MASK_NEG = -1e30

POOL_HALO = 8
POOL_CHUNK = 128
LANES = 128
SUB_M = 1024
TAIL_M = 128
MERGE_SUB_M = 256
MIB = 1024 * 1024
VMEM_BYTES = 64 * MIB
VMEM_COMPILER_RESERVE = 8 * MIB
VMEM_LIMIT_MATMUL = VMEM_BYTES - VMEM_COMPILER_RESERVE
VMEM_LIMIT_SMALL = 40 * MIB


class _Tiles(NamedTuple):
    tm: int
    tn: int
    panel_buffers: float


TILES = {
    "in_proj": _Tiles(2048, 512, 1.5),
    "merge": _Tiles(2048, 256, 2),
    "out_proj": _Tiles(2048, 256, 2),
    "gate_up": _Tiles(2048, 256, 1.5),
    "ffn_down": _Tiles(512, 512, 2),
}


def _params(semantics, vmem_bytes):
    return pltpu.CompilerParams(dimension_semantics=semantics, vmem_limit_bytes=vmem_bytes)


def _rmsnorm_kernel(x_ref, g_ref, o_ref):
    x = x_ref[...]
    ms = jnp.mean(x * x, axis=-1, keepdims=True)
    y = x * lax.rsqrt(ms + RMS_EPS)
    o_ref[...] = (y * g_ref[...]).astype(o_ref.dtype)


def _rmsnorm(x2d, g_row, out_dtype, tr=256):
    m, d = x2d.shape
    return pl.pallas_call(
        _rmsnorm_kernel,
        grid=(m // tr,),
        in_specs=[pl.BlockSpec((tr, d), lambda i: (i, 0)),
                  pl.BlockSpec((1, d), lambda i: (0, 0))],
        out_specs=pl.BlockSpec((tr, d), lambda i: (i, 0)),
        out_shape=jax.ShapeDtypeStruct((m, d), out_dtype),
        compiler_params=_params(("parallel",), VMEM_LIMIT_SMALL),
        name="rmsnorm",
    )(x2d, g_row)


def _row_chunks(tm, sub=None):
    if sub is None:
        sizes, left = [], tm
        while left > 0:
            size = min(SUB_M, left)
            while size > TAIL_M and size * 2 > left:
                size //= 2
            sizes.append(size)
            left -= size
    else:
        sizes = [min(tm, sub)] * (tm // min(tm, sub))
    starts = np.cumsum([0] + sizes[:-1])
    assert sum(sizes) == tm
    return [(int(r), int(n)) for r, n in zip(starts, sizes)]


def _row_factor(inv_ref, r, n, width):
    inv = inv_ref[r:r + n, :]
    return jnp.concatenate([inv] * (width // LANES), axis=1)


def _lane_group_sum(y2):
    part = y2[:, 0:LANES]
    for c in range(LANES, y2.shape[1], LANES):
        part = part + y2[:, c:c + LANES]
    return part


def _panel_rows(x_refs, r, n):
    part = x_refs[0].shape[0]
    return x_refs[r // part][r % part:r % part + n, :]


def _mm_kernel(*refs, n_panel_refs, has_residual, has_row_scale, emits_norm):
    refs = list(refs)
    x_refs, w_ref = refs[:n_panel_refs], refs[n_panel_refs]
    tm = sum(x.shape[0] for x in x_refs)
    pos = n_panel_refs + 1
    r_ref = g_ref = inv_in_ref = None
    if has_residual:
        r_ref = refs[pos]; pos += 1
    if emits_norm:
        g_ref = refs[pos]; pos += 1
    if has_row_scale:
        inv_in_ref = refs[pos]; pos += 1
    o_ref = refs[pos]; pos += 1
    if emits_norm:
        xg_ref, ss_ref = refs[pos], refs[pos + 1]

        @pl.when(pl.program_id(1) == 0)
        def _():
            ss_ref[...] = jnp.zeros(ss_ref.shape, ss_ref.dtype)

    w = w_ref[...].astype(jnp.bfloat16)
    for r, n in _row_chunks(tm):
        y = jnp.dot(_panel_rows(x_refs, r, n), w, preferred_element_type=jnp.float32)
        if has_row_scale:
            y = y * _row_factor(inv_in_ref, r, n, y.shape[1])
        if has_residual:
            y = r_ref[r:r + n, :] + y
        o_ref[r:r + n, :] = y.astype(o_ref.dtype)
        if emits_norm:
            xg_ref[r:r + n, :] = (y * g_ref[...]).astype(xg_ref.dtype)
            ss_ref[r:r + n, :] += _lane_group_sum(y * y)

    if emits_norm:
        @pl.when(pl.program_id(1) == pl.num_programs(1) - 1)
        def _():
            total = jnp.sum(ss_ref[...], axis=-1, keepdims=True)
            inv = lax.rsqrt(total * (1.0 / D_MODEL) + RMS_EPS)
            ss_ref[...] = jnp.broadcast_to(inv, ss_ref.shape)


def _panel_specs(tm, k, buffers=1):
    if buffers == 1.5:
        half = tm // 2
        return [pl.BlockSpec((half, k), lambda i, j: (2 * i, 0), pipeline_mode=pl.Buffered(2)),
                pl.BlockSpec((half, k), lambda i, j: (2 * i + 1, 0),
                             pipeline_mode=pl.Buffered(1))]
    return [pl.BlockSpec((tm, k), lambda i, j: (i, 0), pipeline_mode=pl.Buffered(buffers))]


def _matmul(x, w_stack, layer, out_dtype, name, residual=None, row_inv=None, next_gain=None):
    m, k = x.shape
    n = w_stack.shape[-1]
    tm, tn, panel_buffers = TILES[name]
    panel_specs = _panel_specs(tm, k, panel_buffers)
    in_specs = panel_specs + [pl.BlockSpec((None, k, tn), lambda i, j: (layer, 0, j))]
    args = [x] * len(panel_specs) + [w_stack]
    if residual is not None:
        in_specs.append(pl.BlockSpec((tm, tn), lambda i, j: (i, j)))
        args.append(residual)
    if next_gain is not None:
        in_specs.append(pl.BlockSpec((1, tn), lambda i, j: (0, j)))
        args.append(next_gain)
    if row_inv is not None:
        in_specs.append(pl.BlockSpec((tm, LANES), lambda i, j: (i, 0)))
        args.append(row_inv)
    out_specs = pl.BlockSpec((tm, tn), lambda i, j: (i, j))
    out_shape = jax.ShapeDtypeStruct((m, n), out_dtype)
    if next_gain is not None:
        out_specs = [out_specs,
                     pl.BlockSpec((tm, tn), lambda i, j: (i, j)),
                     pl.BlockSpec((tm, LANES), lambda i, j: (i, 0))]
        out_shape = [out_shape,
                     jax.ShapeDtypeStruct((m, n), jnp.bfloat16),
                     jax.ShapeDtypeStruct((m, LANES), jnp.float32)]
    return pl.pallas_call(
        functools.partial(_mm_kernel, n_panel_refs=len(panel_specs),
                          has_residual=residual is not None,
                          has_row_scale=row_inv is not None,
                          emits_norm=next_gain is not None),
        grid=(m // tm, n // tn),
        in_specs=in_specs,
        out_specs=out_specs,
        out_shape=out_shape,
        compiler_params=_params(("parallel", "arbitrary"), VMEM_LIMIT_MATMUL),
        name=name,
    )(*args)


def _rotary_mix(t, lane):
    half = ROT_DIM // 2
    return jnp.where(lane < half, pltpu.roll(t, HEAD_DIM - half, 1), pltpu.roll(t, half, 1))


def _attn_kernel(sink_ref, q_ref, k_ref, v_ref, cq_ref, sq_ref, ck_ref, sk_ref, bias_ref,
                 o_ref, kpad_ref, vpad_ref, *, seq):
    g = pl.program_id(1)
    nb = seq // BLOCK
    zeros = jnp.zeros((BLOCK, HEAD_DIM), jnp.bfloat16)
    kpad_ref[0:BLOCK, :] = zeros
    kpad_ref[BLOCK + seq:, :] = zeros
    vpad_ref[0:BLOCK, :] = zeros
    vpad_ref[BLOCK + seq:, :] = zeros
    lane_full = lax.broadcasted_iota(jnp.int32, (seq, HEAD_DIM), 1)
    k = k_ref[...]
    k_rot = k * ck_ref[...] + _rotary_mix(k, lane_full) * sk_ref[...]
    kpad_ref[BLOCK:BLOCK + seq, :] = k_rot.astype(jnp.bfloat16)
    vpad_ref[BLOCK:BLOCK + seq, :] = v_ref[...].astype(jnp.bfloat16)

    rows = GROUP * BLOCK
    row_head = lax.broadcasted_iota(jnp.int32, (rows, 1), 0) // BLOCK
    sink = jnp.zeros((rows, 1), jnp.float32)
    for h in range(GROUP):
        sink = jnp.where(row_head == h, sink_ref[g * GROUP + h], sink)

    def body(n, carry):
        r0 = pl.multiple_of(n * BLOCK, BLOCK)
        lane_blk = lax.broadcasted_iota(jnp.int32, (BLOCK, HEAD_DIM), 1)
        cq = cq_ref[pl.ds(r0, BLOCK), :]
        sq = sq_ref[pl.ds(r0, BLOCK), :]
        heads = []
        for h in range(GROUP):
            qh = q_ref[pl.ds(r0, BLOCK), h * HEAD_DIM:(h + 1) * HEAD_DIM]
            heads.append((qh * cq + _rotary_mix(qh, lane_blk) * sq).astype(jnp.bfloat16))
        q = jnp.concatenate(heads, axis=0)
        kw = kpad_ref[pl.ds(r0, 3 * BLOCK), :]
        vw = vpad_ref[pl.ds(r0, 3 * BLOCK), :]
        s = lax.dot_general(q, kw, (((1,), (1,)), ((), ())),
                            preferred_element_type=jnp.float32)
        which = jnp.where(n == 0, 0, jnp.where(n == nb - 1, 2, 1))
        s = s + jnp.concatenate([bias_ref[which]] * GROUP, axis=0)
        m = jnp.maximum(jnp.max(s, axis=-1, keepdims=True), sink)
        p = jnp.exp(s - m)
        denom = jnp.sum(p, axis=-1, keepdims=True) + jnp.exp(sink - m)
        o = jnp.dot(p.astype(jnp.bfloat16), vw, preferred_element_type=jnp.float32)
        o = o / denom
        for h in range(GROUP):
            o_ref[pl.ds(r0, BLOCK), h * HEAD_DIM:(h + 1) * HEAD_DIM] = (
                o[h * BLOCK:(h + 1) * BLOCK, :].astype(o_ref.dtype))
        return carry

    lax.fori_loop(0, nb, body, 0, unroll=2)


def _window_mask_bias():
    qi = np.arange(BLOCK)[:, None]
    kj = np.arange(3 * BLOCK)[None, :]
    band = (kj >= qi) & (kj <= qi + 2 * BLOCK)
    first = band & (kj >= BLOCK)
    last = band & (kj < 2 * BLOCK)
    return np.where(np.stack([first, band, last]), 0.0, MASK_NEG).astype(np.float32)


def _attention(proj, sink_l, tables, batch, seq):
    cq, sq, ck, sk = tables
    m = proj.shape[0]
    qw = GROUP * HEAD_DIM
    assert seq // BLOCK >= 2
    bias = jnp.asarray(_window_mask_bias())
    tab_spec = pl.BlockSpec((seq, HEAD_DIM), lambda b, g: (0, 0))
    return pl.pallas_call(
        functools.partial(_attn_kernel, seq=seq),
        grid=(batch, N_KV_HEADS),
        in_specs=[pl.BlockSpec(memory_space=pltpu.SMEM),
                  pl.BlockSpec((seq, qw), lambda b, g: (b, Q_OFF // qw + g)),
                  pl.BlockSpec((seq, HEAD_DIM), lambda b, g: (b, K_OFF // HEAD_DIM + g)),
                  pl.BlockSpec((seq, HEAD_DIM), lambda b, g: (b, V_OFF // HEAD_DIM + g)),
                  tab_spec, tab_spec, tab_spec, tab_spec,
                  pl.BlockSpec(bias.shape, lambda b, g: (0, 0, 0))],
        out_specs=pl.BlockSpec((seq, qw), lambda b, g: (b, g)),
        out_shape=jax.ShapeDtypeStruct((m, ATTN_WIDTH), jnp.bfloat16),
        scratch_shapes=[pltpu.VMEM((seq + 2 * BLOCK, HEAD_DIM), jnp.bfloat16),
                        pltpu.VMEM((seq + 2 * BLOCK, HEAD_DIM), jnp.bfloat16)],
        compiler_params=_params(("parallel", "arbitrary"), VMEM_LIMIT_SMALL),
        name="window_gqa",
    )(sink_l, proj, proj, proj, cq, sq, ck, sk, bias)


def _pool_kernel(u_ref, w_ref, scale_ref, o_ref, pad_ref, *, seq):
    g = pl.program_id(1)
    zeros = jnp.zeros((POOL_HALO, POOL_GROUP_DIM), jnp.float32)
    pad_ref[0:POOL_HALO, :] = zeros
    pad_ref[POOL_HALO + seq:, :] = zeros
    pad_ref[POOL_HALO:POOL_HALO + seq, :] = u_ref[...]
    wmat = w_ref[...].astype(jnp.bfloat16)
    scale = scale_ref[...]

    def run(win):
        half = win // 2
        ext = POOL_CHUNK + 2 * POOL_HALO

        def body(c, carry):
            t0 = pl.multiple_of(c * POOL_CHUNK, POOL_CHUNK)
            xe = pad_ref[pl.ds(t0, ext), :]
            a = xe
            step = 1
            while step < half:
                a = a + pltpu.roll(a, ext - step, 0)
                step *= 2
            a = a + pltpu.roll(a, half, 0)
            wsum = a[POOL_HALO:POOL_HALO + POOL_CHUNK, :]
            x = xe[POOL_HALO:POOL_HALO + POOL_CHUNK, :]
            t = t0 + lax.broadcasted_iota(jnp.int32, (POOL_CHUNK, 1), 0)
            lo = jnp.maximum(t - half, 0)
            hi = jnp.minimum(t + half - 1, seq - 1)
            cnt = (hi - lo + 1).astype(jnp.float32)
            pooled = wsum / cnt - x
            mixed = jnp.dot(pooled.astype(jnp.bfloat16), wmat,
                            preferred_element_type=jnp.float32)
            o_ref[pl.ds(t0, POOL_CHUNK), :] = (mixed * scale).astype(o_ref.dtype)
            return carry

        lax.fori_loop(0, seq // POOL_CHUNK, body, 0, unroll=2)

    for gi, win in enumerate(POOL_WINDOWS):
        pl.when(g == gi)(functools.partial(run, win))


def _pool(proj, pool_w, pool_scale3, layer, batch, seq):
    m = proj.shape[0]
    c = POOL_GROUP_DIM
    return pl.pallas_call(
        functools.partial(_pool_kernel, seq=seq),
        grid=(batch, N_POOL_GROUPS),
        in_specs=[pl.BlockSpec((seq, c), lambda b, g: (b, U_OFF // c + g)),
                  pl.BlockSpec((None, None, c, c), lambda b, g: (layer, g, 0, 0)),
                  pl.BlockSpec((None, 1, c), lambda b, g: (layer, 0, g))],
        out_specs=pl.BlockSpec((seq, c), lambda b, g: (b, g)),
        out_shape=jax.ShapeDtypeStruct((m, POOL_WIDTH), jnp.bfloat16),
        scratch_shapes=[pltpu.VMEM((seq + 2 * POOL_HALO, c), jnp.float32)],
        compiler_params=_params(("parallel", "arbitrary"), VMEM_LIMIT_SMALL),
        name="multiscale_pool",
    )(proj, pool_w, pool_scale3)


def _merge_kernel(a_ref, p_ref, wa_ref, wp_ref, ga_ref, gb_ref, o_ref):
    wa = wa_ref[...].astype(jnp.bfloat16)
    wp = wp_ref[...].astype(jnp.bfloat16)
    for r, n in _row_chunks(a_ref.shape[0], MERGE_SUB_M):
        ya = jnp.dot(a_ref[r:r + n, :], wa, preferred_element_type=jnp.float32)
        yp = jnp.dot(p_ref[r:r + n, :], wp, preferred_element_type=jnp.float32)
        merged = (jax.nn.sigmoid(ga_ref[r:r + n, :]) * ya
                  + jax.nn.sigmoid(gb_ref[r:r + n, :]) * yp)
        o_ref[r:r + n, :] = merged.astype(o_ref.dtype)


def _merge(o_attn, mixed, wba, wbp, proj, layer):
    m = o_attn.shape[0]
    tm, tn, panel_buffers = TILES["merge"]
    return pl.pallas_call(
        _merge_kernel,
        grid=(m // tm, D_MODEL // tn),
        in_specs=_panel_specs(tm, ATTN_WIDTH, panel_buffers) + _panel_specs(
            tm, POOL_WIDTH, panel_buffers) + [
                  pl.BlockSpec((None, ATTN_WIDTH, tn), lambda i, j: (layer, 0, j)),
                  pl.BlockSpec((None, POOL_WIDTH, tn), lambda i, j: (layer, 0, j)),
                  pl.BlockSpec((tm, tn), lambda i, j: (i, GA_OFF // tn + j)),
                  pl.BlockSpec((tm, tn), lambda i, j: (i, GB_OFF // tn + j))],
        out_specs=pl.BlockSpec((tm, tn), lambda i, j: (i, j)),
        out_shape=jax.ShapeDtypeStruct((m, D_MODEL), jnp.bfloat16),
        compiler_params=_params(("parallel", "arbitrary"), VMEM_LIMIT_MATMUL),
        name="branch_merge",
    )(o_attn, mixed, wba, wbp, proj, proj)


def _gate_up_kernel(*refs):
    *x_refs, inv_ref, wg_ref, wu_ref, wd_ref, o_ref, wd_bf16_ref = refs
    wd_bf16_ref[...] = wd_ref[...].astype(jnp.bfloat16)
    wg = wg_ref[...].astype(jnp.bfloat16)
    wu = wu_ref[...].astype(jnp.bfloat16)
    for r, n in _row_chunks(o_ref.shape[0]):
        x = _panel_rows(x_refs, r, n)
        inv = _row_factor(inv_ref, r, n, wg.shape[1])
        gate = jnp.dot(x, wg, preferred_element_type=jnp.float32) * inv
        up = jnp.dot(x, wu, preferred_element_type=jnp.float32) * inv
        o_ref[r:r + n, :] = (jax.nn.silu(gate) * up).astype(o_ref.dtype)


def _gate_up(xg, row_inv, w_gate_up, w_down, layer):
    m, k = xg.shape
    tm, tn, panel_buffers = TILES["gate_up"]
    nt = D_FF // tn
    steps = (m // tm) * nt
    slab = D_FF // steps
    assert slab * steps == D_FF and slab % 16 == 0
    panel_specs = _panel_specs(tm, k, panel_buffers)
    return pl.pallas_call(
        _gate_up_kernel,
        grid=(m // tm, nt),
        in_specs=panel_specs + [
                  pl.BlockSpec((tm, LANES), lambda i, j: (i, 0)),
                  pl.BlockSpec((None, k, tn), lambda i, j: (layer, 0, j)),
                  pl.BlockSpec((None, k, tn), lambda i, j: (layer, 0, nt + j)),
                  pl.BlockSpec((None, slab, D_MODEL), lambda i, j: (layer, i * nt + j, 0))],
        out_specs=[pl.BlockSpec((tm, tn), lambda i, j: (i, j)),
                   pl.BlockSpec((slab, D_MODEL), lambda i, j: (i * nt + j, 0))],
        out_shape=[jax.ShapeDtypeStruct((m, D_FF), jnp.bfloat16),
                   jax.ShapeDtypeStruct((D_FF, D_MODEL), jnp.bfloat16)],
        compiler_params=_params(("arbitrary", "arbitrary"), VMEM_LIMIT_MATMUL),
        name="swiglu_gate_up",
    )(*([xg] * len(panel_specs)), row_inv, w_gate_up, w_gate_up, w_down)


def _rotary_tables(seq):
    pos = jnp.arange(seq, dtype=jnp.float32)
    inv_freq = 1.0 / jnp.power(jnp.float32(ROPE_THETA),
                               jnp.arange(0, ROT_DIM, 2, dtype=jnp.float32) / ROT_DIM)
    ang = pos[:, None] * inv_freq[None, :]
    cos, sin = jnp.cos(ang), jnp.sin(ang)
    ones = jnp.ones((seq, HEAD_DIM - ROT_DIM), jnp.float32)
    zeros = jnp.zeros((seq, HEAD_DIM - ROT_DIM), jnp.float32)
    c_full = jnp.concatenate([cos, cos, ones], axis=-1)
    s_full = jnp.concatenate([-sin, sin, zeros], axis=-1)
    scale = HEAD_DIM ** -0.5
    return c_full * scale, s_full * scale, c_full, s_full


def kernel(x, norm1_g, w_in, attn_sink, pool_w, pool_scale, w_branch_attn, w_branch_pool,
           w_out, norm2_g, w_gate_up, w_down, final_norm_g):
    batch, seq, d = x.shape
    m = batch * seq
    bf16 = jnp.bfloat16
    tables = _rotary_tables(seq)
    pool_scale3 = pool_scale.reshape(DEPTH, 1, POOL_WIDTH)

    xs = x.reshape(m, d)
    h = _rmsnorm(xs, norm1_g[0].reshape(1, d), bf16)
    row_inv = None
    for l in range(DEPTH):
        proj = _matmul(h, w_in, l, jnp.float32, "in_proj", row_inv=row_inv)
        o_attn = _attention(proj, attn_sink[l], tables, batch, seq)
        mixed = _pool(proj, pool_w, pool_scale3, l, batch, seq)
        merged = _merge(o_attn, mixed, w_branch_attn, w_branch_pool, proj, l)
        xs, xg, row_inv = _matmul(merged, w_out, l, jnp.float32, "out_proj", residual=xs,
                                  next_gain=norm2_g[l].reshape(1, d))
        act, w_down_b = _gate_up(xg, row_inv, w_gate_up, w_down, l)
        if l + 1 < DEPTH:
            xs, h, row_inv = _matmul(act, w_down_b[None], 0, jnp.float32, "ffn_down",
                                     residual=xs, next_gain=norm1_g[l + 1].reshape(1, d))
        else:
            xs = _matmul(act, w_down_b[None], 0, jnp.float32, "ffn_down", residual=xs)
    out = _rmsnorm(xs, final_norm_g.reshape(1, d), jnp.float32)
    return out.reshape(batch, seq, d)
```

```python
import functools
from typing import NamedTuple

import jax
import jax.numpy as jnp
import numpy as np
from jax import lax
from jax.experimental import pallas as pl
from jax.experimental.pallas import tpu as pltpu

D_MODEL = 4096
DEPTH = 2
HEAD_DIM = 128
N_HEADS = 16
N_KV_HEADS = 4
GROUP = N_HEADS // N_KV_HEADS
ATTN_WIDTH = N_HEADS * HEAD_DIM
KV_WIDTH = N_KV_HEADS * HEAD_DIM
BLOCK = 128
ROPE_THETA = 500000.0
ROT_DIM = HEAD_DIM // 4
POOL_WIDTH = D_MODEL // 2
POOL_WINDOWS = (2, 4, 8, 16)
N_POOL_GROUPS = len(POOL_WINDOWS)
POOL_GROUP_DIM = POOL_WIDTH // N_POOL_GROUPS
Q_OFF = 0
K_OFF = ATTN_WIDTH
V_OFF = K_OFF + KV_WIDTH
U_OFF = V_OFF + KV_WIDTH
GA_OFF = U_OFF + POOL_WIDTH
GB_OFF = GA_OFF + D_MODEL
IN_WIDTH = GB_OFF + D_MODEL
D_FF = 11008
RMS_EPS = 1e-6
MASK_NEG = -1e30

POOL_HALO = 8
POOL_CHUNK = 128
LANES = 128
SUB_M = 1024
TAIL_M = 256
MERGE_SUB_M = 512
MIB = 1024 * 1024
VMEM_BYTES = 64 * MIB
VMEM_COMPILER_RESERVE = 8 * MIB
VMEM_LIMIT_MATMUL = VMEM_BYTES - VMEM_COMPILER_RESERVE
VMEM_LIMIT_SMALL = 40 * MIB


class _Tiles(NamedTuple):
    tm: int
    tn: int
    panel_buffers: int


TILES = {
    "in_proj": _Tiles(2048, 512, 1),
    "merge": _Tiles(2048, 256, 2),
    "out_proj": _Tiles(2048, 256, 2),
    "gate_up": _Tiles(2048, 256, 1),
    "ffn_down": _Tiles(512, 512, 2),
}


def _params(semantics, vmem_bytes):
    return pltpu.CompilerParams(dimension_semantics=semantics, vmem_limit_bytes=vmem_bytes)


def _rmsnorm_kernel(x_ref, g_ref, o_ref):
    x = x_ref[...]
    ms = jnp.mean(x * x, axis=-1, keepdims=True)
    y = x * lax.rsqrt(ms + RMS_EPS)
    o_ref[...] = (y * g_ref[...]).astype(o_ref.dtype)


def _rmsnorm(x2d, g_row, out_dtype, tr=256):
    m, d = x2d.shape
    return pl.pallas_call(
        _rmsnorm_kernel,
        grid=(m // tr,),
        in_specs=[pl.BlockSpec((tr, d), lambda i: (i, 0)),
                  pl.BlockSpec((1, d), lambda i: (0, 0))],
        out_specs=pl.BlockSpec((tr, d), lambda i: (i, 0)),
        out_shape=jax.ShapeDtypeStruct((m, d), out_dtype),
        compiler_params=_params(("parallel",), VMEM_LIMIT_SMALL),
        name="rmsnorm",
    )(x2d, g_row)


def _row_chunks(tm, sub=None):
    if sub is None:
        sizes, left = [], tm
        while left > 0:
            size = min(SUB_M, left)
            while size > TAIL_M and size * 2 > left:
                size //= 2
            sizes.append(size)
            left -= size
    else:
        sizes = [min(tm, sub)] * (tm // min(tm, sub))
    starts = np.cumsum([0] + sizes[:-1])
    assert sum(sizes) == tm
    return [(int(r), int(n)) for r, n in zip(starts, sizes)]


def _row_factor(inv_ref, r, n, width):
    inv = inv_ref[r:r + n, :]
    return jnp.concatenate([inv] * (width // LANES), axis=1)


def _lane_group_sum(y2):
    part = y2[:, 0:LANES]
    for c in range(LANES, y2.shape[1], LANES):
        part = part + y2[:, c:c + LANES]
    return part


def _mm_kernel(*refs, has_residual, has_row_scale, emits_norm):
    refs = list(refs)
    x_ref, w_ref = refs[0], refs[1]
    pos = 2
    r_ref = g_ref = inv_in_ref = None
    if has_residual:
        r_ref = refs[pos]; pos += 1
    if emits_norm:
        g_ref = refs[pos]; pos += 1
    if has_row_scale:
        inv_in_ref = refs[pos]; pos += 1
    o_ref = refs[pos]; pos += 1
    if emits_norm:
        xg_ref, ss_ref = refs[pos], refs[pos + 1]

        @pl.when(pl.program_id(1) == 0)
        def _():
            ss_ref[...] = jnp.zeros(ss_ref.shape, ss_ref.dtype)

    w = w_ref[...].astype(jnp.bfloat16)
    for r, n in _row_chunks(x_ref.shape[0]):
        y = jnp.dot(x_ref[r:r + n, :], w, preferred_element_type=jnp.float32)
        if has_row_scale:
            y = y * _row_factor(inv_in_ref, r, n, y.shape[1])
        if has_residual:
            y = r_ref[r:r + n, :] + y
        o_ref[r:r + n, :] = y.astype(o_ref.dtype)
        if emits_norm:
            xg_ref[r:r + n, :] = (y * g_ref[...]).astype(xg_ref.dtype)
            ss_ref[r:r + n, :] += _lane_group_sum(y * y)

    if emits_norm:
        @pl.when(pl.program_id(1) == pl.num_programs(1) - 1)
        def _():
            total = jnp.sum(ss_ref[...], axis=-1, keepdims=True)
            inv = lax.rsqrt(total * (1.0 / D_MODEL) + RMS_EPS)
            ss_ref[...] = jnp.broadcast_to(inv, ss_ref.shape)


def _panel_spec(tm, k, buffers=1):
    return pl.BlockSpec((tm, k), lambda i, j: (i, 0), pipeline_mode=pl.Buffered(buffers))


def _matmul(x, w_stack, layer, out_dtype, name, residual=None, row_inv=None, next_gain=None):
    m, k = x.shape
    n = w_stack.shape[-1]
    tm, tn, panel_buffers = TILES[name]
    in_specs = [_panel_spec(tm, k, panel_buffers),
                pl.BlockSpec((None, k, tn), lambda i, j: (layer, 0, j))]
    args = [x, w_stack]
    if residual is not None:
        in_specs.append(pl.BlockSpec((tm, tn), lambda i, j: (i, j)))
        args.append(residual)
    if next_gain is not None:
        in_specs.append(pl.BlockSpec((1, tn), lambda i, j: (0, j)))
        args.append(next_gain)
    if row_inv is not None:
        in_specs.append(pl.BlockSpec((tm, LANES), lambda i, j: (i, 0)))
        args.append(row_inv)
    out_specs = pl.BlockSpec((tm, tn), lambda i, j: (i, j))
    out_shape = jax.ShapeDtypeStruct((m, n), out_dtype)
    if next_gain is not None:
        out_specs = [out_specs,
                     pl.BlockSpec((tm, tn), lambda i, j: (i, j)),
                     pl.BlockSpec((tm, LANES), lambda i, j: (i, 0))]
        out_shape = [out_shape,
                     jax.ShapeDtypeStruct((m, n), jnp.bfloat16),
                     jax.ShapeDtypeStruct((m, LANES), jnp.float32)]
    return pl.pallas_call(
        functools.partial(_mm_kernel, has_residual=residual is not None,
                          has_row_scale=row_inv is not None,
                          emits_norm=next_gain is not None),
        grid=(m // tm, n // tn),
        in_specs=in_specs,
        out_specs=out_specs,
        out_shape=out_shape,
        compiler_params=_params(("parallel", "arbitrary"), VMEM_LIMIT_MATMUL),
        name=name,
    )(*args)


def _rotary_mix(t, lane):
    half = ROT_DIM // 2
    return jnp.where(lane < half, pltpu.roll(t, HEAD_DIM - half, 1), pltpu.roll(t, half, 1))


def _attn_kernel(sink_ref, q_ref, k_ref, v_ref, cq_ref, sq_ref, ck_ref, sk_ref, bias_ref,
                 o_ref, kpad_ref, vpad_ref, *, seq):
    g = pl.program_id(1)
    nb = seq // BLOCK
    zeros = jnp.zeros((BLOCK, HEAD_DIM), jnp.bfloat16)
    kpad_ref[0:BLOCK, :] = zeros
    kpad_ref[BLOCK + seq:, :] = zeros
    vpad_ref[0:BLOCK, :] = zeros
    vpad_ref[BLOCK + seq:, :] = zeros
    lane_full = lax.broadcasted_iota(jnp.int32, (seq, HEAD_DIM), 1)
    k = k_ref[...]
    k_rot = k * ck_ref[...] + _rotary_mix(k, lane_full) * sk_ref[...]
    kpad_ref[BLOCK:BLOCK + seq, :] = k_rot.astype(jnp.bfloat16)
    vpad_ref[BLOCK:BLOCK + seq, :] = v_ref[...].astype(jnp.bfloat16)

    rows = GROUP * BLOCK
    row_head = lax.broadcasted_iota(jnp.int32, (rows, 1), 0) // BLOCK
    sink = jnp.zeros((rows, 1), jnp.float32)
    for h in range(GROUP):
        sink = jnp.where(row_head == h, sink_ref[g * GROUP + h], sink)

    def body(n, carry):
        r0 = pl.multiple_of(n * BLOCK, BLOCK)
        lane_blk = lax.broadcasted_iota(jnp.int32, (BLOCK, HEAD_DIM), 1)
        cq = cq_ref[pl.ds(r0, BLOCK), :]
        sq = sq_ref[pl.ds(r0, BLOCK), :]
        heads = []
        for h in range(GROUP):
            qh = q_ref[pl.ds(r0, BLOCK), h * HEAD_DIM:(h + 1) * HEAD_DIM]
            heads.append((qh * cq + _rotary_mix(qh, lane_blk) * sq).astype(jnp.bfloat16))
        q = jnp.concatenate(heads, axis=0)
        kw = kpad_ref[pl.ds(r0, 3 * BLOCK), :]
        vw = vpad_ref[pl.ds(r0, 3 * BLOCK), :]
        s = lax.dot_general(q, kw, (((1,), (1,)), ((), ())),
                            preferred_element_type=jnp.float32)
        which = jnp.where(n == 0, 0, jnp.where(n == nb - 1, 2, 1))
        s = s + jnp.concatenate([bias_ref[which]] * GROUP, axis=0)
        m = jnp.maximum(jnp.max(s, axis=-1, keepdims=True), sink)
        p = jnp.exp(s - m)
        denom = jnp.sum(p, axis=-1, keepdims=True) + jnp.exp(sink - m)
        o = jnp.dot(p.astype(jnp.bfloat16), vw, preferred_element_type=jnp.float32)
        o = o / denom
        for h in range(GROUP):
            o_ref[pl.ds(r0, BLOCK), h * HEAD_DIM:(h + 1) * HEAD_DIM] = (
                o[h * BLOCK:(h + 1) * BLOCK, :].astype(o_ref.dtype))
        return carry

    lax.fori_loop(0, nb, body, 0, unroll=2)


def _window_mask_bias():
    qi = np.arange(BLOCK)[:, None]
    kj = np.arange(3 * BLOCK)[None, :]
    band = (kj >= qi) & (kj <= qi + 2 * BLOCK)
    first = band & (kj >= BLOCK)
    last = band & (kj < 2 * BLOCK)
    return np.where(np.stack([first, band, last]), 0.0, MASK_NEG).astype(np.float32)


def _attention(proj, sink_l, tables, batch, seq):
    cq, sq, ck, sk = tables
    m = proj.shape[0]
    qw = GROUP * HEAD_DIM
    assert seq // BLOCK >= 2
    bias = jnp.asarray(_window_mask_bias())
    tab_spec = pl.BlockSpec((seq, HEAD_DIM), lambda b, g: (0, 0))
    return pl.pallas_call(
        functools.partial(_attn_kernel, seq=seq),
        grid=(batch, N_KV_HEADS),
        in_specs=[pl.BlockSpec(memory_space=pltpu.SMEM),
                  pl.BlockSpec((seq, qw), lambda b, g: (b, Q_OFF // qw + g)),
                  pl.BlockSpec((seq, HEAD_DIM), lambda b, g: (b, K_OFF // HEAD_DIM + g)),
                  pl.BlockSpec((seq, HEAD_DIM), lambda b, g: (b, V_OFF // HEAD_DIM + g)),
                  tab_spec, tab_spec, tab_spec, tab_spec,
                  pl.BlockSpec(bias.shape, lambda b, g: (0, 0, 0))],
        out_specs=pl.BlockSpec((seq, qw), lambda b, g: (b, g)),
        out_shape=jax.ShapeDtypeStruct((m, ATTN_WIDTH), jnp.bfloat16),
        scratch_shapes=[pltpu.VMEM((seq + 2 * BLOCK, HEAD_DIM), jnp.bfloat16),
                        pltpu.VMEM((seq + 2 * BLOCK, HEAD_DIM), jnp.bfloat16)],
        compiler_params=_params(("parallel", "arbitrary"), VMEM_LIMIT_SMALL),
        name="window_gqa",
    )(sink_l, proj, proj, proj, cq, sq, ck, sk, bias)


def _pool_kernel(u_ref, w_ref, scale_ref, o_ref, pad_ref, *, seq):
    g = pl.program_id(1)
    zeros = jnp.zeros((POOL_HALO, POOL_GROUP_DIM), jnp.float32)
    pad_ref[0:POOL_HALO, :] = zeros
    pad_ref[POOL_HALO + seq:, :] = zeros
    pad_ref[POOL_HALO:POOL_HALO + seq, :] = u_ref[...]
    wmat = w_ref[...].astype(jnp.bfloat16)
    scale = scale_ref[...]

    def run(win):
        half = win // 2
        ext = POOL_CHUNK + 2 * POOL_HALO

        def body(c, carry):
            t0 = pl.multiple_of(c * POOL_CHUNK, POOL_CHUNK)
            xe = pad_ref[pl.ds(t0, ext), :]
            a = xe
            step = 1
            while step < half:
                a = a + pltpu.roll(a, ext - step, 0)
                step *= 2
            a = a + pltpu.roll(a, half, 0)
            wsum = a[POOL_HALO:POOL_HALO + POOL_CHUNK, :]
            x = xe[POOL_HALO:POOL_HALO + POOL_CHUNK, :]
            t = t0 + lax.broadcasted_iota(jnp.int32, (POOL_CHUNK, 1), 0)
            lo = jnp.maximum(t - half, 0)
            hi = jnp.minimum(t + half - 1, seq - 1)
            cnt = (hi - lo + 1).astype(jnp.float32)
            pooled = wsum / cnt - x
            mixed = jnp.dot(pooled.astype(jnp.bfloat16), wmat,
                            preferred_element_type=jnp.float32)
            o_ref[pl.ds(t0, POOL_CHUNK), :] = (mixed * scale).astype(o_ref.dtype)
            return carry

        lax.fori_loop(0, seq // POOL_CHUNK, body, 0, unroll=2)

    for gi, win in enumerate(POOL_WINDOWS):
        pl.when(g == gi)(functools.partial(run, win))


def _pool(proj, pool_w, pool_scale3, layer, batch, seq):
    m = proj.shape[0]
    c = POOL_GROUP_DIM
    return pl.pallas_call(
        functools.partial(_pool_kernel, seq=seq),
        grid=(batch, N_POOL_GROUPS),
        in_specs=[pl.BlockSpec((seq, c), lambda b, g: (b, U_OFF // c + g)),
                  pl.BlockSpec((None, None, c, c), lambda b, g: (layer, g, 0, 0)),
                  pl.BlockSpec((None, 1, c), lambda b, g: (layer, 0, g))],
        out_specs=pl.BlockSpec((seq, c), lambda b, g: (b, g)),
        out_shape=jax.ShapeDtypeStruct((m, POOL_WIDTH), jnp.bfloat16),
        scratch_shapes=[pltpu.VMEM((seq + 2 * POOL_HALO, c), jnp.float32)],
        compiler_params=_params(("parallel", "arbitrary"), VMEM_LIMIT_SMALL),
        name="multiscale_pool",
    )(proj, pool_w, pool_scale3)


def _merge_kernel(a_ref, p_ref, wa_ref, wp_ref, ga_ref, gb_ref, o_ref):
    wa = wa_ref[...].astype(jnp.bfloat16)
    wp = wp_ref[...].astype(jnp.bfloat16)
    for r, n in _row_chunks(a_ref.shape[0], MERGE_SUB_M):
        ya = jnp.dot(a_ref[r:r + n, :], wa, preferred_element_type=jnp.float32)
        yp = jnp.dot(p_ref[r:r + n, :], wp, preferred_element_type=jnp.float32)
        merged = (jax.nn.sigmoid(ga_ref[r:r + n, :]) * ya
                  + jax.nn.sigmoid(gb_ref[r:r + n, :]) * yp)
        o_ref[r:r + n, :] = merged.astype(o_ref.dtype)


def _merge(o_attn, mixed, wba, wbp, proj, layer):
    m = o_attn.shape[0]
    tm, tn, panel_buffers = TILES["merge"]
    return pl.pallas_call(
        _merge_kernel,
        grid=(m // tm, D_MODEL // tn),
        in_specs=[_panel_spec(tm, ATTN_WIDTH, panel_buffers),
                  _panel_spec(tm, POOL_WIDTH, panel_buffers),
                  pl.BlockSpec((None, ATTN_WIDTH, tn), lambda i, j: (layer, 0, j)),
                  pl.BlockSpec((None, POOL_WIDTH, tn), lambda i, j: (layer, 0, j)),
                  pl.BlockSpec((tm, tn), lambda i, j: (i, GA_OFF // tn + j)),
                  pl.BlockSpec((tm, tn), lambda i, j: (i, GB_OFF // tn + j))],
        out_specs=pl.BlockSpec((tm, tn), lambda i, j: (i, j)),
        out_shape=jax.ShapeDtypeStruct((m, D_MODEL), jnp.bfloat16),
        compiler_params=_params(("parallel", "arbitrary"), VMEM_LIMIT_MATMUL),
        name="branch_merge",
    )(o_attn, mixed, wba, wbp, proj, proj)


def _gate_up_kernel(x_ref, inv_ref, wg_ref, wu_ref, wd_ref, o_ref, wd_bf16_ref):
    wd_bf16_ref[...] = wd_ref[...].astype(jnp.bfloat16)
    wg = wg_ref[...].astype(jnp.bfloat16)
    wu = wu_ref[...].astype(jnp.bfloat16)
    for r, n in _row_chunks(x_ref.shape[0]):
        x = x_ref[r:r + n, :]
        inv = _row_factor(inv_ref, r, n, wg.shape[1])
        gate = jnp.dot(x, wg, preferred_element_type=jnp.float32) * inv
        up = jnp.dot(x, wu, preferred_element_type=jnp.float32) * inv
        o_ref[r:r + n, :] = (jax.nn.silu(gate) * up).astype(o_ref.dtype)


def _gate_up(xg, row_inv, w_gate_up, w_down, layer):
    m, k = xg.shape
    tm, tn, panel_buffers = TILES["gate_up"]
    nt = D_FF // tn
    steps = (m // tm) * nt
    slab = D_FF // steps
    assert slab * steps == D_FF and slab % 16 == 0
    return pl.pallas_call(
        _gate_up_kernel,
        grid=(m // tm, nt),
        in_specs=[_panel_spec(tm, k, panel_buffers),
                  pl.BlockSpec((tm, LANES), lambda i, j: (i, 0)),
                  pl.BlockSpec((None, k, tn), lambda i, j: (layer, 0, j)),
                  pl.BlockSpec((None, k, tn), lambda i, j: (layer, 0, nt + j)),
                  pl.BlockSpec((None, slab, D_MODEL), lambda i, j: (layer, i * nt + j, 0))],
        out_specs=[pl.BlockSpec((tm, tn), lambda i, j: (i, j)),
                   pl.BlockSpec((slab, D_MODEL), lambda i, j: (i * nt + j, 0))],
        out_shape=[jax.ShapeDtypeStruct((m, D_FF), jnp.bfloat16),
                   jax.ShapeDtypeStruct((D_FF, D_MODEL), jnp.bfloat16)],
        compiler_params=_params(("arbitrary", "arbitrary"), VMEM_LIMIT_MATMUL),
        name="swiglu_gate_up",
    )(xg, row_inv, w_gate_up, w_gate_up, w_down)


def _rotary_tables(seq):
    pos = jnp.arange(seq, dtype=jnp.float32)
    inv_freq = 1.0 / jnp.power(jnp.float32(ROPE_THETA),
                               jnp.arange(0, ROT_DIM, 2, dtype=jnp.float32) / ROT_DIM)
    ang = pos[:, None] * inv_freq[None, :]
    cos, sin = jnp.cos(ang), jnp.sin(ang)
    ones = jnp.ones((seq, HEAD_DIM - ROT_DIM), jnp.float32)
    zeros = jnp.zeros((seq, HEAD_DIM - ROT_DIM), jnp.float32)
    c_full = jnp.concatenate([cos, cos, ones], axis=-1)
    s_full = jnp.concatenate([-sin, sin, zeros], axis=-1)
    scale = HEAD_DIM ** -0.5
    return c_full * scale, s_full * scale, c_full, s_full


def kernel(x, norm1_g, w_in, attn_sink, pool_w, pool_scale, w_branch_attn, w_branch_pool,
           w_out, norm2_g, w_gate_up, w_down, final_norm_g):
    batch, seq, d = x.shape
    m = batch * seq
    bf16 = jnp.bfloat16
    tables = _rotary_tables(seq)
    pool_scale3 = pool_scale.reshape(DEPTH, 1, POOL_WIDTH)

    xs = x.reshape(m, d)
    h = _rmsnorm(xs, norm1_g[0].reshape(1, d), bf16)
    row_inv = None
    for l in range(DEPTH):
        proj = _matmul(h, w_in, l, jnp.float32, "in_proj", row_inv=row_inv)
        o_attn = _attention(proj, attn_sink[l], tables, batch, seq)
        mixed = _pool(proj, pool_w, pool_scale3, l, batch, seq)
        merged = _merge(o_attn, mixed, w_branch_attn, w_branch_pool, proj, l)
        xs, xg, row_inv = _matmul(merged, w_out, l, jnp.float32, "out_proj", residual=xs,
                                  next_gain=norm2_g[l].reshape(1, d))
        act, w_down_b = _gate_up(xg, row_inv, w_gate_up, w_down, l)
        if l + 1 < DEPTH:
            xs, h, row_inv = _matmul(act, w_down_b[None], 0, jnp.float32, "ffn_down",
                                     residual=xs, next_gain=norm1_g[l + 1].reshape(1, d))
        else:
            xs = _matmul(act, w_down_b[None], 0, jnp.float32, "ffn_down", residual=xs)
    out = _rmsnorm(xs, final_norm_g.reshape(1, d), jnp.float32)
    return out.reshape(batch, seq, d)
```

```python
import functools
from typing import NamedTuple

import jax
import jax.numpy as jnp
import numpy as np
from jax import lax
from jax.experimental import pallas as pl
from jax.experimental.pallas import tpu as pltpu

D_MODEL = 4096
DEPTH = 2
HEAD_DIM = 128
N_HEADS = 16
N_KV_HEADS = 4
GROUP = N_HEADS // N_KV_HEADS
ATTN_WIDTH = N_HEADS * HEAD_DIM
KV_WIDTH = N_KV_HEADS * HEAD_DIM
BLOCK = 128
ROPE_THETA = 500000.0
ROT_DIM = HEAD_DIM // 4
POOL_WIDTH = D_MODEL // 2
POOL_WINDOWS = (2, 4, 8, 16)
N_POOL_GROUPS = len(POOL_WINDOWS)
POOL_GROUP_DIM = POOL_WIDTH // N_POOL_GROUPS
Q_OFF = 0
K_OFF = ATTN_WIDTH
V_OFF = K_OFF + KV_WIDTH
U_OFF = V_OFF + KV_WIDTH
GA_OFF = U_OFF + POOL_WIDTH
GB_OFF = GA_OFF + D_MODEL
IN_WIDTH = GB_OFF + D_MODEL
D_FF = 11008
RMS_EPS = 1e-6
MASK_NEG = -1e30

POOL_HALO = 8
POOL_CHUNK = 128
LANES = 128
SUB_M = 1024
TAIL_M = 256
MERGE_SUB_M = 256
MIB = 1024 * 1024
VMEM_BYTES = 64 * MIB
VMEM_COMPILER_RESERVE = 8 * MIB
VMEM_LIMIT_MATMUL = VMEM_BYTES - VMEM_COMPILER_RESERVE
VMEM_LIMIT_SMALL = 40 * MIB


class _Tiles(NamedTuple):
    tm: int
    tn: int
    panel_buffers: float


TILES = {
    "in_proj": _Tiles(2048, 512, 1),
    "merge": _Tiles(2048, 256, 2),
    "out_proj": _Tiles(2048, 256, 2),
    "gate_up": _Tiles(2048, 256, 1),
    "ffn_down": _Tiles(1024, 256, 1.5),
}


def _params(semantics, vmem_bytes):
    return pltpu.CompilerParams(dimension_semantics=semantics, vmem_limit_bytes=vmem_bytes)


def _rmsnorm_kernel(x_ref, g_ref, o_ref):
    x = x_ref[...]
    ms = jnp.mean(x * x, axis=-1, keepdims=True)
    y = x * lax.rsqrt(ms + RMS_EPS)
    o_ref[...] = (y * g_ref[...]).astype(o_ref.dtype)


def _rmsnorm(x2d, g_row, out_dtype, tr=256):
    m, d = x2d.shape
    return pl.pallas_call(
        _rmsnorm_kernel,
        grid=(m // tr,),
        in_specs=[pl.BlockSpec((tr, d), lambda i: (i, 0)),
                  pl.BlockSpec((1, d), lambda i: (0, 0))],
        out_specs=pl.BlockSpec((tr, d), lambda i: (i, 0)),
        out_shape=jax.ShapeDtypeStruct((m, d), out_dtype),
        compiler_params=_params(("parallel",), VMEM_LIMIT_SMALL),
        name="rmsnorm",
    )(x2d, g_row)


def _row_chunks(tm, sub=None):
    if sub is None:
        sizes, left = [], tm
        while left > 0:
            size = min(SUB_M, left)
            while size > TAIL_M and size * 2 > left:
                size //= 2
            sizes.append(size)
            left -= size
    else:
        sizes = [min(tm, sub)] * (tm // min(tm, sub))
    starts = np.cumsum([0] + sizes[:-1])
    assert sum(sizes) == tm
    return [(int(r), int(n)) for r, n in zip(starts, sizes)]


def _row_factor(inv_ref, r, n, width):
    inv = inv_ref[r:r + n, :]
    return jnp.concatenate([inv] * (width // LANES), axis=1)


def _lane_group_sum(y2):
    part = y2[:, 0:LANES]
    for c in range(LANES, y2.shape[1], LANES):
        part = part + y2[:, c:c + LANES]
    return part


def _panel_rows(x_refs, r, n):
    part = x_refs[0].shape[0]
    return x_refs[r // part][r % part:r % part + n, :]


def _rotary_heads(y, cos_ref, sin_ref, r, n):
    c = cos_ref[r:r + n, :]
    s = sin_ref[r:r + n, :]
    lane = lax.broadcasted_iota(jnp.int32, (n, HEAD_DIM), 1)
    heads = []
    for h in range(0, y.shape[1], HEAD_DIM):
        t = y[:, h:h + HEAD_DIM]
        heads.append(t * c + _rotary_mix(t, lane) * s)
    return jnp.concatenate(heads, axis=1)


def _mm_kernel(*refs, n_panel_refs, has_residual, has_row_scale, emits_norm, rotary_tiles):
    refs = list(refs)
    x_refs, w_ref = refs[:n_panel_refs], refs[n_panel_refs]
    tm = sum(x.shape[0] for x in x_refs)
    pos = n_panel_refs + 1
    r_ref = g_ref = inv_in_ref = cos_ref = sin_ref = None
    if has_residual:
        r_ref = refs[pos]; pos += 1
    if emits_norm:
        g_ref = refs[pos]; pos += 1
    if has_row_scale:
        inv_in_ref = refs[pos]; pos += 1
    if rotary_tiles:
        cos_ref, sin_ref = refs[pos], refs[pos + 1]; pos += 2
    o_ref = refs[pos]; pos += 1
    if emits_norm:
        xg_ref, ss_ref = refs[pos], refs[pos + 1]

        @pl.when(pl.program_id(1) == 0)
        def _():
            ss_ref[...] = jnp.zeros(ss_ref.shape, ss_ref.dtype)

    def run_chunks(with_rotary):
        w = w_ref[...].astype(jnp.bfloat16)
        for r, n in _row_chunks(tm):
            y = jnp.dot(_panel_rows(x_refs, r, n), w, preferred_element_type=jnp.float32)
            if has_row_scale:
                y = y * _row_factor(inv_in_ref, r, n, y.shape[1])
            if has_residual:
                y = r_ref[r:r + n, :] + y
            if with_rotary:
                y = _rotary_heads(y, cos_ref, sin_ref, r, n)
            o_ref[r:r + n, :] = y.astype(o_ref.dtype)
            if emits_norm:
                xg_ref[r:r + n, :] = (y * g_ref[...]).astype(xg_ref.dtype)
                ss_ref[r:r + n, :] += _lane_group_sum(y * y)

    if rotary_tiles:
        is_rotary_tile = pl.program_id(1) < rotary_tiles
        pl.when(is_rotary_tile)(functools.partial(run_chunks, True))
        pl.when(jnp.logical_not(is_rotary_tile))(functools.partial(run_chunks, False))
    else:
        run_chunks(False)

    if emits_norm:
        @pl.when(pl.program_id(1) == pl.num_programs(1) - 1)
        def _():
            total = jnp.sum(ss_ref[...], axis=-1, keepdims=True)
            inv = lax.rsqrt(total * (1.0 / D_MODEL) + RMS_EPS)
            ss_ref[...] = jnp.broadcast_to(inv, ss_ref.shape)


def _panel_specs(tm, k, buffers=1):
    if buffers == 1.5:
        half = tm // 2
        return [pl.BlockSpec((half, k), lambda i, j: (2 * i, 0), pipeline_mode=pl.Buffered(2)),
                pl.BlockSpec((half, k), lambda i, j: (2 * i + 1, 0),
                             pipeline_mode=pl.Buffered(1))]
    return [pl.BlockSpec((tm, k), lambda i, j: (i, 0), pipeline_mode=pl.Buffered(buffers))]


def _matmul(x, w_stack, layer, out_dtype, name, residual=None, row_inv=None, next_gain=None,
            rotary=None):
    m, k = x.shape
    n = w_stack.shape[-1]
    tm, tn, panel_buffers = TILES[name]
    panel_specs = _panel_specs(tm, k, panel_buffers)
    in_specs = panel_specs + [pl.BlockSpec((None, k, tn), lambda i, j: (layer, 0, j))]
    args = [x] * len(panel_specs) + [w_stack]
    if residual is not None:
        in_specs.append(pl.BlockSpec((tm, tn), lambda i, j: (i, j)))
        args.append(residual)
    if next_gain is not None:
        in_specs.append(pl.BlockSpec((1, tn), lambda i, j: (0, j)))
        args.append(next_gain)
    if row_inv is not None:
        in_specs.append(pl.BlockSpec((tm, LANES), lambda i, j: (i, 0)))
        args.append(row_inv)
    rotary_tiles = 0
    if rotary is not None:
        cos2, sin2, seq = rotary
        assert tn == KV_WIDTH and seq % tm == 0 and Q_OFF == 0 and K_OFF == ATTN_WIDTH
        q_tiles = ATTN_WIDTH // tn
        rotary_tiles = q_tiles + 1

        def table_index(i, j):
            return (jnp.clip(j - (q_tiles - 1), 0, 1), i % (seq // tm), 0)

        in_specs += [pl.BlockSpec((None, tm, HEAD_DIM), table_index)] * 2
        args += [cos2, sin2]
    out_specs = pl.BlockSpec((tm, tn), lambda i, j: (i, j))
    out_shape = jax.ShapeDtypeStruct((m, n), out_dtype)
    if next_gain is not None:
        out_specs = [out_specs,
                     pl.BlockSpec((tm, tn), lambda i, j: (i, j)),
                     pl.BlockSpec((tm, LANES), lambda i, j: (i, 0))]
        out_shape = [out_shape,
                     jax.ShapeDtypeStruct((m, n), jnp.bfloat16),
                     jax.ShapeDtypeStruct((m, LANES), jnp.float32)]
    return pl.pallas_call(
        functools.partial(_mm_kernel, n_panel_refs=len(panel_specs),
                          has_residual=residual is not None,
                          has_row_scale=row_inv is not None,
                          emits_norm=next_gain is not None, rotary_tiles=rotary_tiles),
        grid=(m // tm, n // tn),
        in_specs=in_specs,
        out_specs=out_specs,
        out_shape=out_shape,
        compiler_params=_params(("parallel", "arbitrary"), VMEM_LIMIT_MATMUL),
        name=name,
    )(*args)


def _rotary_mix(t, lane):
    half = ROT_DIM // 2
    return jnp.where(lane < half, pltpu.roll(t, HEAD_DIM - half, 1), pltpu.roll(t, half, 1))


def _attn_kernel(sink_ref, q_ref, k_ref, v_ref, bias_ref, o_ref, kpad_ref, vpad_ref, *, seq):
    g = pl.program_id(1)
    nb = seq // BLOCK
    zeros = jnp.zeros((BLOCK, HEAD_DIM), jnp.bfloat16)
    kpad_ref[0:BLOCK, :] = zeros
    kpad_ref[BLOCK + seq:, :] = zeros
    vpad_ref[0:BLOCK, :] = zeros
    vpad_ref[BLOCK + seq:, :] = zeros
    kpad_ref[BLOCK:BLOCK + seq, :] = k_ref[...].astype(jnp.bfloat16)
    vpad_ref[BLOCK:BLOCK + seq, :] = v_ref[...].astype(jnp.bfloat16)

    rows = GROUP * BLOCK
    row_head = lax.broadcasted_iota(jnp.int32, (rows, 1), 0) // BLOCK
    sink = jnp.zeros((rows, 1), jnp.float32)
    for h in range(GROUP):
        sink = jnp.where(row_head == h, sink_ref[g * GROUP + h], sink)

    def body(n, carry):
        r0 = pl.multiple_of(n * BLOCK, BLOCK)
        q = jnp.concatenate(
            [q_ref[pl.ds(r0, BLOCK), h * HEAD_DIM:(h + 1) * HEAD_DIM].astype(jnp.bfloat16)
             for h in range(GROUP)], axis=0)
        kw = kpad_ref[pl.ds(r0, 3 * BLOCK), :]
        vw = vpad_ref[pl.ds(r0, 3 * BLOCK), :]
        s = lax.dot_general(q, kw, (((1,), (1,)), ((), ())),
                            preferred_element_type=jnp.float32)
        which = jnp.where(n == 0, 0, jnp.where(n == nb - 1, 2, 1))
        s = s + jnp.concatenate([bias_ref[which]] * GROUP, axis=0)
        m = jnp.maximum(jnp.max(s, axis=-1, keepdims=True), sink)
        p = jnp.exp(s - m)
        denom = jnp.sum(p, axis=-1, keepdims=True) + jnp.exp(sink - m)
        o = jnp.dot(p.astype(jnp.bfloat16), vw, preferred_element_type=jnp.float32)
        o = o / denom
        for h in range(GROUP):
            o_ref[pl.ds(r0, BLOCK), h * HEAD_DIM:(h + 1) * HEAD_DIM] = (
                o[h * BLOCK:(h + 1) * BLOCK, :].astype(o_ref.dtype))
        return carry

    lax.fori_loop(0, nb, body, 0, unroll=2)


def _window_mask_bias():
    qi = np.arange(BLOCK)[:, None]
    kj = np.arange(3 * BLOCK)[None, :]
    band = (kj >= qi) & (kj <= qi + 2 * BLOCK)
    first = band & (kj >= BLOCK)
    last = band & (kj < 2 * BLOCK)
    return np.where(np.stack([first, band, last]), 0.0, MASK_NEG).astype(np.float32)


def _attention(proj, sink_l, batch, seq):
    m = proj.shape[0]
    qw = GROUP * HEAD_DIM
    assert seq // BLOCK >= 2
    bias = jnp.asarray(_window_mask_bias())
    return pl.pallas_call(
        functools.partial(_attn_kernel, seq=seq),
        grid=(batch, N_KV_HEADS),
        in_specs=[pl.BlockSpec(memory_space=pltpu.SMEM),
                  pl.BlockSpec((seq, qw), lambda b, g: (b, Q_OFF // qw + g)),
                  pl.BlockSpec((seq, HEAD_DIM), lambda b, g: (b, K_OFF // HEAD_DIM + g)),
                  pl.BlockSpec((seq, HEAD_DIM), lambda b, g: (b, V_OFF // HEAD_DIM + g)),
                  pl.BlockSpec(bias.shape, lambda b, g: (0, 0, 0))],
        out_specs=pl.BlockSpec((seq, qw), lambda b, g: (b, g)),
        out_shape=jax.ShapeDtypeStruct((m, ATTN_WIDTH), jnp.bfloat16),
        scratch_shapes=[pltpu.VMEM((seq + 2 * BLOCK, HEAD_DIM), jnp.bfloat16),
                        pltpu.VMEM((seq + 2 * BLOCK, HEAD_DIM), jnp.bfloat16)],
        compiler_params=_params(("parallel", "arbitrary"), VMEM_LIMIT_SMALL),
        name="window_gqa",
    )(sink_l, proj, proj, proj, bias)


def _pool_kernel(u_ref, w_ref, scale_ref, o_ref, pad_ref, *, seq):
    g = pl.program_id(1)
    zeros = jnp.zeros((POOL_HALO, POOL_GROUP_DIM), jnp.float32)
    pad_ref[0:POOL_HALO, :] = zeros
    pad_ref[POOL_HALO + seq:, :] = zeros
    pad_ref[POOL_HALO:POOL_HALO + seq, :] = u_ref[...]
    wmat = w_ref[...].astype(jnp.bfloat16)
    scale = scale_ref[...]

    def run(win):
        half = win // 2
        ext = POOL_CHUNK + 2 * POOL_HALO

        def body(c, carry):
            t0 = pl.multiple_of(c * POOL_CHUNK, POOL_CHUNK)
            xe = pad_ref[pl.ds(t0, ext), :]
            a = xe
            step = 1
            while step < half:
                a = a + pltpu.roll(a, ext - step, 0)
                step *= 2
            a = a + pltpu.roll(a, half, 0)
            wsum = a[POOL_HALO:POOL_HALO + POOL_CHUNK, :]
            x = xe[POOL_HALO:POOL_HALO + POOL_CHUNK, :]
            t = t0 + lax.broadcasted_iota(jnp.int32, (POOL_CHUNK, 1), 0)
            lo = jnp.maximum(t - half, 0)
            hi = jnp.minimum(t + half - 1, seq - 1)
            cnt = (hi - lo + 1).astype(jnp.float32)
            pooled = wsum / cnt - x
            mixed = jnp.dot(pooled.astype(jnp.bfloat16), wmat,
                            preferred_element_type=jnp.float32)
            o_ref[pl.ds(t0, POOL_CHUNK), :] = (mixed * scale).astype(o_ref.dtype)
            return carry

        lax.fori_loop(0, seq // POOL_CHUNK, body, 0, unroll=2)

    for gi, win in enumerate(POOL_WINDOWS):
        pl.when(g == gi)(functools.partial(run, win))


def _pool(proj, pool_w, pool_scale3, layer, batch, seq):
    m = proj.shape[0]
    c = POOL_GROUP_DIM
    return pl.pallas_call(
        functools.partial(_pool_kernel, seq=seq),
        grid=(batch, N_POOL_GROUPS),
        in_specs=[pl.BlockSpec((seq, c), lambda b, g: (b, U_OFF // c + g)),
                  pl.BlockSpec((None, None, c, c), lambda b, g: (layer, g, 0, 0)),
                  pl.BlockSpec((None, 1, c), lambda b, g: (layer, 0, g))],
        out_specs=pl.BlockSpec((seq, c), lambda b, g: (b, g)),
        out_shape=jax.ShapeDtypeStruct((m, POOL_WIDTH), jnp.bfloat16),
        scratch_shapes=[pltpu.VMEM((seq + 2 * POOL_HALO, c), jnp.float32)],
        compiler_params=_params(("parallel", "arbitrary"), VMEM_LIMIT_SMALL),
        name="multiscale_pool",
    )(proj, pool_w, pool_scale3)


def _merge_kernel(a_ref, p_ref, wa_ref, wp_ref, ga_ref, gb_ref, o_ref):
    wa = wa_ref[...].astype(jnp.bfloat16)
    wp = wp_ref[...].astype(jnp.bfloat16)
    for r, n in _row_chunks(a_ref.shape[0], MERGE_SUB_M):
        ya = jnp.dot(a_ref[r:r + n, :], wa, preferred_element_type=jnp.float32)
        yp = jnp.dot(p_ref[r:r + n, :], wp, preferred_element_type=jnp.float32)
        merged = (jax.nn.sigmoid(ga_ref[r:r + n, :]) * ya
                  + jax.nn.sigmoid(gb_ref[r:r + n, :]) * yp)
        o_ref[r:r + n, :] = merged.astype(o_ref.dtype)


def _merge(o_attn, mixed, wba, wbp, proj, layer):
    m = o_attn.shape[0]
    tm, tn, panel_buffers = TILES["merge"]
    return pl.pallas_call(
        _merge_kernel,
        grid=(m // tm, D_MODEL // tn),
        in_specs=[_panel_specs(tm, ATTN_WIDTH, panel_buffers)[0],
                  _panel_specs(tm, POOL_WIDTH, panel_buffers)[0],
                  pl.BlockSpec((None, ATTN_WIDTH, tn), lambda i, j: (layer, 0, j)),
                  pl.BlockSpec((None, POOL_WIDTH, tn), lambda i, j: (layer, 0, j)),
                  pl.BlockSpec((tm, tn), lambda i, j: (i, GA_OFF // tn + j)),
                  pl.BlockSpec((tm, tn), lambda i, j: (i, GB_OFF // tn + j))],
        out_specs=pl.BlockSpec((tm, tn), lambda i, j: (i, j)),
        out_shape=jax.ShapeDtypeStruct((m, D_MODEL), jnp.bfloat16),
        compiler_params=_params(("parallel", "arbitrary"), VMEM_LIMIT_MATMUL),
        name="branch_merge",
    )(o_attn, mixed, wba, wbp, proj, proj)


def _gate_up_kernel(x_ref, inv_ref, wg_ref, wu_ref, wd_ref, o_ref, wd_bf16_ref):
    wd_bf16_ref[...] = wd_ref[...].astype(jnp.bfloat16)
    wg = wg_ref[...].astype(jnp.bfloat16)
    wu = wu_ref[...].astype(jnp.bfloat16)
    for r, n in _row_chunks(x_ref.shape[0]):
        x = x_ref[r:r + n, :]
        inv = _row_factor(inv_ref, r, n, wg.shape[1])
        gate = jnp.dot(x, wg, preferred_element_type=jnp.float32) * inv
        up = jnp.dot(x, wu, preferred_element_type=jnp.float32) * inv
        o_ref[r:r + n, :] = (jax.nn.silu(gate) * up).astype(o_ref.dtype)


def _gate_up(xg, row_inv, w_gate_up, w_down, layer):
    m, k = xg.shape
    tm, tn, panel_buffers = TILES["gate_up"]
    nt = D_FF // tn
    steps = (m // tm) * nt
    slab = D_FF // steps
    assert slab * steps == D_FF and slab % 16 == 0
    return pl.pallas_call(
        _gate_up_kernel,
        grid=(m // tm, nt),
        in_specs=[_panel_specs(tm, k, panel_buffers)[0],
                  pl.BlockSpec((tm, LANES), lambda i, j: (i, 0)),
                  pl.BlockSpec((None, k, tn), lambda i, j: (layer, 0, j)),
                  pl.BlockSpec((None, k, tn), lambda i, j: (layer, 0, nt + j)),
                  pl.BlockSpec((None, slab, D_MODEL), lambda i, j: (layer, i * nt + j, 0))],
        out_specs=[pl.BlockSpec((tm, tn), lambda i, j: (i, j)),
                   pl.BlockSpec((slab, D_MODEL), lambda i, j: (i * nt + j, 0))],
        out_shape=[jax.ShapeDtypeStruct((m, D_FF), jnp.bfloat16),
                   jax.ShapeDtypeStruct((D_FF, D_MODEL), jnp.bfloat16)],
        compiler_params=_params(("arbitrary", "arbitrary"), VMEM_LIMIT_MATMUL),
        name="swiglu_gate_up",
    )(xg, row_inv, w_gate_up, w_gate_up, w_down)


def _rotary_tables(seq):
    pos = jnp.arange(seq, dtype=jnp.float32)
    inv_freq = 1.0 / jnp.power(jnp.float32(ROPE_THETA),
                               jnp.arange(0, ROT_DIM, 2, dtype=jnp.float32) / ROT_DIM)
    ang = pos[:, None] * inv_freq[None, :]
    cos, sin = jnp.cos(ang), jnp.sin(ang)
    ones = jnp.ones((seq, HEAD_DIM - ROT_DIM), jnp.float32)
    zeros = jnp.zeros((seq, HEAD_DIM - ROT_DIM), jnp.float32)
    c_full = jnp.concatenate([cos, cos, ones], axis=-1)
    s_full = jnp.concatenate([-sin, sin, zeros], axis=-1)
    scale = HEAD_DIM ** -0.5
    return jnp.stack([c_full * scale, c_full]), jnp.stack([s_full * scale, s_full])


def kernel(x, norm1_g, w_in, attn_sink, pool_w, pool_scale, w_branch_attn, w_branch_pool,
           w_out, norm2_g, w_gate_up, w_down, final_norm_g):
    batch, seq, d = x.shape
    m = batch * seq
    bf16 = jnp.bfloat16
    cos2, sin2 = _rotary_tables(seq)
    pool_scale3 = pool_scale.reshape(DEPTH, 1, POOL_WIDTH)

    xs = x.reshape(m, d)
    h = _rmsnorm(xs, norm1_g[0].reshape(1, d), bf16)
    row_inv = None
    for l in range(DEPTH):
        proj = _matmul(h, w_in, l, jnp.float32, "in_proj", row_inv=row_inv,
                       rotary=(cos2, sin2, seq))
        o_attn = _attention(proj, attn_sink[l], batch, seq)
        mixed = _pool(proj, pool_w, pool_scale3, l, batch, seq)
        merged = _merge(o_attn, mixed, w_branch_attn, w_branch_pool, proj, l)
        xs, xg, row_inv = _matmul(merged, w_out, l, jnp.float32, "out_proj", residual=xs,
                                  next_gain=norm2_g[l].reshape(1, d))
        act, w_down_b = _gate_up(xg, row_inv, w_gate_up, w_down, l)
        if l + 1 < DEPTH:
            xs, h, row_inv = _matmul(act, w_down_b[None], 0, jnp.float32, "ffn_down",
                                     residual=xs, next_gain=norm1_g[l + 1].reshape(1, d))
        else:
            xs = _matmul(act, w_down_b[None], 0, jnp.float32, "ffn_down", residual=xs)
    out = _rmsnorm(xs, final_norm_g.reshape(1, d), jnp.float32)
    return out.reshape(batch, seq, d)
```

```python
import functools
from typing import NamedTuple

import jax
import jax.numpy as jnp
import numpy as np
from jax import lax
from jax.experimental import pallas as pl
from jax.experimental.pallas import tpu as pltpu

D_MODEL = 4096
DEPTH = 2
HEAD_DIM = 128
N_HEADS = 16
N_KV_HEADS = 4
GROUP = N_HEADS // N_KV_HEADS
ATTN_WIDTH = N_HEADS * HEAD_DIM
KV_WIDTH = N_KV_HEADS * HEAD_DIM
BLOCK = 128
ROPE_THETA = 500000.0
ROT_DIM = HEAD_DIM // 4
POOL_WIDTH = D_MODEL // 2
POOL_WINDOWS = (2, 4, 8, 16)
N_POOL_GROUPS = len(POOL_WINDOWS)
POOL_GROUP_DIM = POOL_WIDTH // N_POOL_GROUPS
Q_OFF = 0
K_OFF = ATTN_WIDTH
V_OFF = K_OFF + KV_WIDTH
U_OFF = V_OFF + KV_WIDTH
GA_OFF = U_OFF + POOL_WIDTH
GB_OFF = GA_OFF + D_MODEL
IN_WIDTH = GB_OFF + D_MODEL
D_FF = 11008
RMS_EPS = 1e-6
MASK_NEG = -1e30

POOL_HALO = 8
POOL_CHUNK = 128
LANES = 128
SUB_M = 1024
TAIL_M = 256
MERGE_SUB_M = 256
MIB = 1024 * 1024
VMEM_BYTES = 64 * MIB
VMEM_COMPILER_RESERVE = 8 * MIB
VMEM_LIMIT_MATMUL = VMEM_BYTES - VMEM_COMPILER_RESERVE
VMEM_LIMIT_SMALL = 40 * MIB


class _Tiles(NamedTuple):
    tm: int
    tn: int
    panel_buffers: int


TILES = {
    "in_proj": _Tiles(2048, 512, 1),
    "merge": _Tiles(2048, 256, 2),
    "out_proj": _Tiles(2048, 256, 2),
    "gate_up": _Tiles(2048, 256, 1),
    "ffn_down": _Tiles(512, 512, 2),
}


def _params(semantics, vmem_bytes):
    return pltpu.CompilerParams(dimension_semantics=semantics, vmem_limit_bytes=vmem_bytes)


def _rmsnorm_kernel(x_ref, g_ref, o_ref):
    x = x_ref[...]
    ms = jnp.mean(x * x, axis=-1, keepdims=True)
    y = x * lax.rsqrt(ms + RMS_EPS)
    o_ref[...] = (y * g_ref[...]).astype(o_ref.dtype)


def _rmsnorm(x2d, g_row, out_dtype, tr=256):
    m, d = x2d.shape
    return pl.pallas_call(
        _rmsnorm_kernel,
        grid=(m // tr,),
        in_specs=[pl.BlockSpec((tr, d), lambda i: (i, 0)),
                  pl.BlockSpec((1, d), lambda i: (0, 0))],
        out_specs=pl.BlockSpec((tr, d), lambda i: (i, 0)),
        out_shape=jax.ShapeDtypeStruct((m, d), out_dtype),
        compiler_params=_params(("parallel",), VMEM_LIMIT_SMALL),
        name="rmsnorm",
    )(x2d, g_row)


def _row_chunks(tm, sub=None):
    if sub is None:
        sizes, left = [], tm
        while left > 0:
            size = min(SUB_M, left)
            while size > TAIL_M and size * 2 > left:
                size //= 2
            sizes.append(size)
            left -= size
    else:
        sizes = [min(tm, sub)] * (tm // min(tm, sub))
    starts = np.cumsum([0] + sizes[:-1])
    assert sum(sizes) == tm
    return [(int(r), int(n)) for r, n in zip(starts, sizes)]


def _row_factor(inv_ref, r, n, width):
    inv = inv_ref[r:r + n, :]
    return jnp.concatenate([inv] * (width // LANES), axis=1)


def _lane_group_sum(y2):
    part = y2[:, 0:LANES]
    for c in range(LANES, y2.shape[1], LANES):
        part = part + y2[:, c:c + LANES]
    return part


def _mm_kernel(*refs, has_residual, has_row_scale, emits_norm):
    refs = list(refs)
    x_ref, w_ref = refs[0], refs[1]
    pos = 2
    r_ref = g_ref = inv_in_ref = None
    if has_residual:
        r_ref = refs[pos]; pos += 1
    if emits_norm:
        g_ref = refs[pos]; pos += 1
    if has_row_scale:
        inv_in_ref = refs[pos]; pos += 1
    o_ref = refs[pos]; pos += 1
    if emits_norm:
        xg_ref, ss_ref = refs[pos], refs[pos + 1]

        @pl.when(pl.program_id(1) == 0)
        def _():
            ss_ref[...] = jnp.zeros(ss_ref.shape, ss_ref.dtype)

    w = w_ref[...].astype(jnp.bfloat16)
    for r, n in _row_chunks(x_ref.shape[0]):
        y = jnp.dot(x_ref[r:r + n, :], w, preferred_element_type=jnp.float32)
        if has_row_scale:
            y = y * _row_factor(inv_in_ref, r, n, y.shape[1])
        if has_residual:
            y = r_ref[r:r + n, :] + y
        o_ref[r:r + n, :] = y.astype(o_ref.dtype)
        if emits_norm:
            xg_ref[r:r + n, :] = (y * g_ref[...]).astype(xg_ref.dtype)
            ss_ref[r:r + n, :] += _lane_group_sum(y * y)

    if emits_norm:
        @pl.when(pl.program_id(1) == pl.num_programs(1) - 1)
        def _():
            total = jnp.sum(ss_ref[...], axis=-1, keepdims=True)
            inv = lax.rsqrt(total * (1.0 / D_MODEL) + RMS_EPS)
            ss_ref[...] = jnp.broadcast_to(inv, ss_ref.shape)


def _panel_spec(tm, k, buffers=1):
    return pl.BlockSpec((tm, k), lambda i, j: (i, 0), pipeline_mode=pl.Buffered(buffers))


def _matmul(x, w_stack, layer, out_dtype, name, residual=None, row_inv=None, next_gain=None):
    m, k = x.shape
    n = w_stack.shape[-1]
    tm, tn, panel_buffers = TILES[name]
    in_specs = [_panel_spec(tm, k, panel_buffers),
                pl.BlockSpec((None, k, tn), lambda i, j: (layer, 0, j))]
    args = [x, w_stack]
    if residual is not None:
        in_specs.append(pl.BlockSpec((tm, tn), lambda i, j: (i, j)))
        args.append(residual)
    if next_gain is not None:
        in_specs.append(pl.BlockSpec((1, tn), lambda i, j: (0, j)))
        args.append(next_gain)
    if row_inv is not None:
        in_specs.append(pl.BlockSpec((tm, LANES), lambda i, j: (i, 0)))
        args.append(row_inv)
    out_specs = pl.BlockSpec((tm, tn), lambda i, j: (i, j))
    out_shape = jax.ShapeDtypeStruct((m, n), out_dtype)
    if next_gain is not None:
        out_specs = [out_specs,
                     pl.BlockSpec((tm, tn), lambda i, j: (i, j)),
                     pl.BlockSpec((tm, LANES), lambda i, j: (i, 0))]
        out_shape = [out_shape,
                     jax.ShapeDtypeStruct((m, n), jnp.bfloat16),
                     jax.ShapeDtypeStruct((m, LANES), jnp.float32)]
    return pl.pallas_call(
        functools.partial(_mm_kernel, has_residual=residual is not None,
                          has_row_scale=row_inv is not None,
                          emits_norm=next_gain is not None),
        grid=(m // tm, n // tn),
        in_specs=in_specs,
        out_specs=out_specs,
        out_shape=out_shape,
        compiler_params=_params(("parallel", "arbitrary"), VMEM_LIMIT_MATMUL),
        name=name,
    )(*args)


def _mm_final_norm_kernel(x_ref, w_ref, r_ref, g_ref, o_ref, ss_ref):
    j = pl.program_id(1)
    tn = w_ref.shape[1]

    @pl.when(j == 0)
    def _():
        ss_ref[...] = jnp.zeros(ss_ref.shape, ss_ref.dtype)

    w = w_ref[...].astype(jnp.bfloat16)
    col = pl.multiple_of(j * tn, tn)
    for r, n in _row_chunks(x_ref.shape[0]):
        y = r_ref[r:r + n, :] + jnp.dot(x_ref[r:r + n, :], w,
                                        preferred_element_type=jnp.float32)
        o_ref[r:r + n, pl.ds(col, tn)] = y
        ss_ref[r:r + n, :] += _lane_group_sum(y * y)

    @pl.when(j == pl.num_programs(1) - 1)
    def _():
        gain = g_ref[...]
        for r in range(0, o_ref.shape[0], BLOCK):
            total = jnp.sum(ss_ref[r:r + BLOCK, :], axis=-1, keepdims=True)
            inv = lax.rsqrt(total * (1.0 / D_MODEL) + RMS_EPS)
            o_ref[r:r + BLOCK, :] = o_ref[r:r + BLOCK, :] * inv * gain


def _matmul_final_norm(x, w, residual, gain, name):
    m, k = x.shape
    n = w.shape[-1]
    tm, tn, panel_buffers = TILES[name]
    return pl.pallas_call(
        _mm_final_norm_kernel,
        grid=(m // tm, n // tn),
        in_specs=[_panel_spec(tm, k, panel_buffers),
                  pl.BlockSpec((k, tn), lambda i, j: (0, j)),
                  pl.BlockSpec((tm, tn), lambda i, j: (i, j)),
                  pl.BlockSpec((1, n), lambda i, j: (0, 0))],
        out_specs=pl.BlockSpec((tm, n), lambda i, j: (i, 0), pipeline_mode=pl.Buffered(1)),
        out_shape=jax.ShapeDtypeStruct((m, n), jnp.float32),
        scratch_shapes=[pltpu.VMEM((tm, LANES), jnp.float32)],
        compiler_params=_params(("parallel", "arbitrary"), VMEM_LIMIT_MATMUL),
        name=name + "_final_norm",
    )(x, w, residual, gain)


def _rotary_mix(t, lane):
    half = ROT_DIM // 2
    return jnp.where(lane < half, pltpu.roll(t, HEAD_DIM - half, 1), pltpu.roll(t, half, 1))


def _attn_kernel(sink_ref, q_ref, k_ref, v_ref, cq_ref, sq_ref, ck_ref, sk_ref, bias_ref,
                 o_ref, kpad_ref, vpad_ref, *, seq):
    g = pl.program_id(1)
    nb = seq // BLOCK
    zeros = jnp.zeros((BLOCK, HEAD_DIM), jnp.bfloat16)
    kpad_ref[0:BLOCK, :] = zeros
    kpad_ref[BLOCK + seq:, :] = zeros
    vpad_ref[0:BLOCK, :] = zeros
    vpad_ref[BLOCK + seq:, :] = zeros
    lane_full = lax.broadcasted_iota(jnp.int32, (seq, HEAD_DIM), 1)
    k = k_ref[...]
    k_rot = k * ck_ref[...] + _rotary_mix(k, lane_full) * sk_ref[...]
    kpad_ref[BLOCK:BLOCK + seq, :] = k_rot.astype(jnp.bfloat16)
    vpad_ref[BLOCK:BLOCK + seq, :] = v_ref[...].astype(jnp.bfloat16)

    rows = GROUP * BLOCK
    row_head = lax.broadcasted_iota(jnp.int32, (rows, 1), 0) // BLOCK
    sink = jnp.zeros((rows, 1), jnp.float32)
    for h in range(GROUP):
        sink = jnp.where(row_head == h, sink_ref[g * GROUP + h], sink)

    def body(n, carry):
        r0 = pl.multiple_of(n * BLOCK, BLOCK)
        lane_blk = lax.broadcasted_iota(jnp.int32, (BLOCK, HEAD_DIM), 1)
        cq = cq_ref[pl.ds(r0, BLOCK), :]
        sq = sq_ref[pl.ds(r0, BLOCK), :]
        heads = []
        for h in range(GROUP):
            qh = q_ref[pl.ds(r0, BLOCK), h * HEAD_DIM:(h + 1) * HEAD_DIM]
            heads.append((qh * cq + _rotary_mix(qh, lane_blk) * sq).astype(jnp.bfloat16))
        q = jnp.concatenate(heads, axis=0)
        kw = kpad_ref[pl.ds(r0, 3 * BLOCK), :]
        vw = vpad_ref[pl.ds(r0, 3 * BLOCK), :]
        s = lax.dot_general(q, kw, (((1,), (1,)), ((), ())),
                            preferred_element_type=jnp.float32)
        which = jnp.where(n == 0, 0, jnp.where(n == nb - 1, 2, 1))
        s = s + jnp.concatenate([bias_ref[which]] * GROUP, axis=0)
        m = jnp.maximum(jnp.max(s, axis=-1, keepdims=True), sink)
        p = jnp.exp(s - m)
        denom = jnp.sum(p, axis=-1, keepdims=True) + jnp.exp(sink - m)
        o = jnp.dot(p.astype(jnp.bfloat16), vw, preferred_element_type=jnp.float32)
        o = o / denom
        for h in range(GROUP):
            o_ref[pl.ds(r0, BLOCK), h * HEAD_DIM:(h + 1) * HEAD_DIM] = (
                o[h * BLOCK:(h + 1) * BLOCK, :].astype(o_ref.dtype))
        return carry

    lax.fori_loop(0, nb, body, 0, unroll=2)


def _window_mask_bias():
    qi = np.arange(BLOCK)[:, None]
    kj = np.arange(3 * BLOCK)[None, :]
    band = (kj >= qi) & (kj <= qi + 2 * BLOCK)
    first = band & (kj >= BLOCK)
    last = band & (kj < 2 * BLOCK)
    return np.where(np.stack([first, band, last]), 0.0, MASK_NEG).astype(np.float32)


def _attention(proj, sink_l, tables, batch, seq):
    cq, sq, ck, sk = tables
    m = proj.shape[0]
    qw = GROUP * HEAD_DIM
    assert seq // BLOCK >= 2
    bias = jnp.asarray(_window_mask_bias())
    tab_spec = pl.BlockSpec((seq, HEAD_DIM), lambda b, g: (0, 0))
    return pl.pallas_call(
        functools.partial(_attn_kernel, seq=seq),
        grid=(batch, N_KV_HEADS),
        in_specs=[pl.BlockSpec(memory_space=pltpu.SMEM),
                  pl.BlockSpec((seq, qw), lambda b, g: (b, Q_OFF // qw + g)),
                  pl.BlockSpec((seq, HEAD_DIM), lambda b, g: (b, K_OFF // HEAD_DIM + g)),
                  pl.BlockSpec((seq, HEAD_DIM), lambda b, g: (b, V_OFF // HEAD_DIM + g)),
                  tab_spec, tab_spec, tab_spec, tab_spec,
                  pl.BlockSpec(bias.shape, lambda b, g: (0, 0, 0))],
        out_specs=pl.BlockSpec((seq, qw), lambda b, g: (b, g)),
        out_shape=jax.ShapeDtypeStruct((m, ATTN_WIDTH), jnp.bfloat16),
        scratch_shapes=[pltpu.VMEM((seq + 2 * BLOCK, HEAD_DIM), jnp.bfloat16),
                        pltpu.VMEM((seq + 2 * BLOCK, HEAD_DIM), jnp.bfloat16)],
        compiler_params=_params(("parallel", "arbitrary"), VMEM_LIMIT_SMALL),
        name="window_gqa",
    )(sink_l, proj, proj, proj, cq, sq, ck, sk, bias)


def _pool_kernel(u_ref, w_ref, scale_ref, o_ref, pad_ref, *, seq):
    g = pl.program_id(1)
    zeros = jnp.zeros((POOL_HALO, POOL_GROUP_DIM), jnp.float32)
    pad_ref[0:POOL_HALO, :] = zeros
    pad_ref[POOL_HALO + seq:, :] = zeros
    pad_ref[POOL_HALO:POOL_HALO + seq, :] = u_ref[...]
    wmat = w_ref[...].astype(jnp.bfloat16)
    scale = scale_ref[...]

    def run(win):
        half = win // 2
        ext = POOL_CHUNK + 2 * POOL_HALO

        def body(c, carry):
            t0 = pl.multiple_of(c * POOL_CHUNK, POOL_CHUNK)
            xe = pad_ref[pl.ds(t0, ext), :]
            a = xe
            step = 1
            while step < half:
                a = a + pltpu.roll(a, ext - step, 0)
                step *= 2
            a = a + pltpu.roll(a, half, 0)
            wsum = a[POOL_HALO:POOL_HALO + POOL_CHUNK, :]
            x = xe[POOL_HALO:POOL_HALO + POOL_CHUNK, :]
            t = t0 + lax.broadcasted_iota(jnp.int32, (POOL_CHUNK, 1), 0)
            lo = jnp.maximum(t - half, 0)
            hi = jnp.minimum(t + half - 1, seq - 1)
            cnt = (hi - lo + 1).astype(jnp.float32)
            pooled = wsum / cnt - x
            mixed = jnp.dot(pooled.astype(jnp.bfloat16), wmat,
                            preferred_element_type=jnp.float32)
            o_ref[pl.ds(t0, POOL_CHUNK), :] = (mixed * scale).astype(o_ref.dtype)
            return carry

        lax.fori_loop(0, seq // POOL_CHUNK, body, 0, unroll=2)

    for gi, win in enumerate(POOL_WINDOWS):
        pl.when(g == gi)(functools.partial(run, win))


def _pool(proj, pool_w, pool_scale3, layer, batch, seq):
    m = proj.shape[0]
    c = POOL_GROUP_DIM
    return pl.pallas_call(
        functools.partial(_pool_kernel, seq=seq),
        grid=(batch, N_POOL_GROUPS),
        in_specs=[pl.BlockSpec((seq, c), lambda b, g: (b, U_OFF // c + g)),
                  pl.BlockSpec((None, None, c, c), lambda b, g: (layer, g, 0, 0)),
                  pl.BlockSpec((None, 1, c), lambda b, g: (layer, 0, g))],
        out_specs=pl.BlockSpec((seq, c), lambda b, g: (b, g)),
        out_shape=jax.ShapeDtypeStruct((m, POOL_WIDTH), jnp.bfloat16),
        scratch_shapes=[pltpu.VMEM((seq + 2 * POOL_HALO, c), jnp.float32)],
        compiler_params=_params(("parallel", "arbitrary"), VMEM_LIMIT_SMALL),
        name="multiscale_pool",
    )(proj, pool_w, pool_scale3)


def _merge_kernel(a_ref, p_ref, wa_ref, wp_ref, ga_ref, gb_ref, o_ref):
    wa = wa_ref[...].astype(jnp.bfloat16)
    wp = wp_ref[...].astype(jnp.bfloat16)
    for r, n in _row_chunks(a_ref.shape[0], MERGE_SUB_M):
        ya = jnp.dot(a_ref[r:r + n, :], wa, preferred_element_type=jnp.float32)
        yp = jnp.dot(p_ref[r:r + n, :], wp, preferred_element_type=jnp.float32)
        merged = (jax.nn.sigmoid(ga_ref[r:r + n, :]) * ya
                  + jax.nn.sigmoid(gb_ref[r:r + n, :]) * yp)
        o_ref[r:r + n, :] = merged.astype(o_ref.dtype)


def _merge(o_attn, mixed, wba, wbp, proj, layer):
    m = o_attn.shape[0]
    tm, tn, panel_buffers = TILES["merge"]
    return pl.pallas_call(
        _merge_kernel,
        grid=(m // tm, D_MODEL // tn),
        in_specs=[_panel_spec(tm, ATTN_WIDTH, panel_buffers),
                  _panel_spec(tm, POOL_WIDTH, panel_buffers),
                  pl.BlockSpec((None, ATTN_WIDTH, tn), lambda i, j: (layer, 0, j)),
                  pl.BlockSpec((None, POOL_WIDTH, tn), lambda i, j: (layer, 0, j)),
                  pl.BlockSpec((tm, tn), lambda i, j: (i, GA_OFF // tn + j)),
                  pl.BlockSpec((tm, tn), lambda i, j: (i, GB_OFF // tn + j))],
        out_specs=pl.BlockSpec((tm, tn), lambda i, j: (i, j)),
        out_shape=jax.ShapeDtypeStruct((m, D_MODEL), jnp.bfloat16),
        compiler_params=_params(("parallel", "arbitrary"), VMEM_LIMIT_MATMUL),
        name="branch_merge",
    )(o_attn, mixed, wba, wbp, proj, proj)


def _gate_up_kernel(x_ref, inv_ref, wg_ref, wu_ref, wd_ref, o_ref, wd_bf16_ref):
    wd_bf16_ref[...] = wd_ref[...].astype(jnp.bfloat16)
    wg = wg_ref[...].astype(jnp.bfloat16)
    wu = wu_ref[...].astype(jnp.bfloat16)
    for r, n in _row_chunks(x_ref.shape[0]):
        x = x_ref[r:r + n, :]
        inv = _row_factor(inv_ref, r, n, wg.shape[1])
        gate = jnp.dot(x, wg, preferred_element_type=jnp.float32) * inv
        up = jnp.dot(x, wu, preferred_element_type=jnp.float32) * inv
        o_ref[r:r + n, :] = (jax.nn.silu(gate) * up).astype(o_ref.dtype)


def _gate_up(xg, row_inv, w_gate_up, w_down, layer):
    m, k = xg.shape
    tm, tn, panel_buffers = TILES["gate_up"]
    nt = D_FF // tn
    steps = (m // tm) * nt
    slab = D_FF // steps
    assert slab * steps == D_FF and slab % 16 == 0
    return pl.pallas_call(
        _gate_up_kernel,
        grid=(m // tm, nt),
        in_specs=[_panel_spec(tm, k, panel_buffers),
                  pl.BlockSpec((tm, LANES), lambda i, j: (i, 0)),
                  pl.BlockSpec((None, k, tn), lambda i, j: (layer, 0, j)),
                  pl.BlockSpec((None, k, tn), lambda i, j: (layer, 0, nt + j)),
                  pl.BlockSpec((None, slab, D_MODEL), lambda i, j: (layer, i * nt + j, 0))],
        out_specs=[pl.BlockSpec((tm, tn), lambda i, j: (i, j)),
                   pl.BlockSpec((slab, D_MODEL), lambda i, j: (i * nt + j, 0))],
        out_shape=[jax.ShapeDtypeStruct((m, D_FF), jnp.bfloat16),
                   jax.ShapeDtypeStruct((D_FF, D_MODEL), jnp.bfloat16)],
        compiler_params=_params(("arbitrary", "arbitrary"), VMEM_LIMIT_MATMUL),
        name="swiglu_gate_up",
    )(xg, row_inv, w_gate_up, w_gate_up, w_down)


def _rotary_tables(seq):
    pos = jnp.arange(seq, dtype=jnp.float32)
    inv_freq = 1.0 / jnp.power(jnp.float32(ROPE_THETA),
                               jnp.arange(0, ROT_DIM, 2, dtype=jnp.float32) / ROT_DIM)
    ang = pos[:, None] * inv_freq[None, :]
    cos, sin = jnp.cos(ang), jnp.sin(ang)
    ones = jnp.ones((seq, HEAD_DIM - ROT_DIM), jnp.float32)
    zeros = jnp.zeros((seq, HEAD_DIM - ROT_DIM), jnp.float32)
    c_full = jnp.concatenate([cos, cos, ones], axis=-1)
    s_full = jnp.concatenate([-sin, sin, zeros], axis=-1)
    scale = HEAD_DIM ** -0.5
    return c_full * scale, s_full * scale, c_full, s_full


def kernel(x, norm1_g, w_in, attn_sink, pool_w, pool_scale, w_branch_attn, w_branch_pool,
           w_out, norm2_g, w_gate_up, w_down, final_norm_g):
    batch, seq, d = x.shape
    m = batch * seq
    bf16 = jnp.bfloat16
    tables = _rotary_tables(seq)
    pool_scale3 = pool_scale.reshape(DEPTH, 1, POOL_WIDTH)

    xs = x.reshape(m, d)
    h = _rmsnorm(xs, norm1_g[0].reshape(1, d), bf16)
    row_inv = None
    for l in range(DEPTH):
        proj = _matmul(h, w_in, l, jnp.float32, "in_proj", row_inv=row_inv)
        o_attn = _attention(proj, attn_sink[l], tables, batch, seq)
        mixed = _pool(proj, pool_w, pool_scale3, l, batch, seq)
        merged = _merge(o_attn, mixed, w_branch_attn, w_branch_pool, proj, l)
        xs, xg, row_inv = _matmul(merged, w_out, l, jnp.float32, "out_proj", residual=xs,
                                  next_gain=norm2_g[l].reshape(1, d))
        act, w_down_b = _gate_up(xg, row_inv, w_gate_up, w_down, l)
        if l + 1 < DEPTH:
            xs, h, row_inv = _matmul(act, w_down_b[None], 0, jnp.float32, "ffn_down",
                                     residual=xs, next_gain=norm1_g[l + 1].reshape(1, d))
        else:
            out = _matmul_final_norm(act, w_down_b, xs, final_norm_g.reshape(1, d), "ffn_down")
    return out.reshape(batch, seq, d)
```

```python
import functools
from typing import NamedTuple

import jax
import jax.numpy as jnp
import numpy as np
from jax import lax
from jax.experimental import pallas as pl
from jax.experimental.pallas import tpu as pltpu

D_MODEL = 4096
DEPTH = 2
HEAD_DIM = 128
N_HEADS = 16
N_KV_HEADS = 4
GROUP = N_HEADS // N_KV_HEADS
ATTN_WIDTH = N_HEADS * HEAD_DIM
KV_WIDTH = N_KV_HEADS * HEAD_DIM
BLOCK = 128
ROPE_THETA = 500000.0
ROT_DIM = HEAD_DIM // 4
POOL_WIDTH = D_MODEL // 2
POOL_WINDOWS = (2, 4, 8, 16)
N_POOL_GROUPS = len(POOL_WINDOWS)
POOL_GROUP_DIM = POOL_WIDTH // N_POOL_GROUPS
Q_OFF = 0
K_OFF = ATTN_WIDTH
V_OFF = K_OFF + KV_WIDTH
U_OFF = V_OFF + KV_WIDTH
GA_OFF = U_OFF + POOL_WIDTH
GB_OFF = GA_OFF + D_MODEL
IN_WIDTH = GB_OFF + D_MODEL
D_FF = 11008
RMS_EPS = 1e-6
MASK_NEG = -1e30

POOL_HALO = 8
POOL_CHUNK = 128
LANES = 128
SUB_M = 1024
TAIL_M = 256
MERGE_SUB_M = 256
MIB = 1024 * 1024
VMEM_BYTES = 64 * MIB
VMEM_COMPILER_RESERVE = 8 * MIB
VMEM_LIMIT_MATMUL = VMEM_BYTES - VMEM_COMPILER_RESERVE
VMEM_LIMIT_SMALL = 40 * MIB


class _Tiles(NamedTuple):
    tm: int
    tn: int
    panel_buffers: int


TILES = {
    "in_proj": _Tiles(2048, 512, 1),
    "merge": _Tiles(2048, 256, 2),
    "out_proj": _Tiles(2048, 256, 2),
    "gate_up": _Tiles(2048, 256, 1),
    "ffn_down": _Tiles(512, 512, 2),
}


def _params(semantics, vmem_bytes):
    return pltpu.CompilerParams(dimension_semantics=semantics, vmem_limit_bytes=vmem_bytes)


def _rmsnorm_kernel(x_ref, g_ref, o_ref):
    x = x_ref[...]
    ms = jnp.mean(x * x, axis=-1, keepdims=True)
    y = x * lax.rsqrt(ms + RMS_EPS)
    o_ref[...] = (y * g_ref[...]).astype(o_ref.dtype)


def _rmsnorm(x2d, g_row, out_dtype, tr=256):
    m, d = x2d.shape
    return pl.pallas_call(
        _rmsnorm_kernel,
        grid=(m // tr,),
        in_specs=[pl.BlockSpec((tr, d), lambda i: (i, 0)),
                  pl.BlockSpec((1, d), lambda i: (0, 0))],
        out_specs=pl.BlockSpec((tr, d), lambda i: (i, 0)),
        out_shape=jax.ShapeDtypeStruct((m, d), out_dtype),
        compiler_params=_params(("parallel",), VMEM_LIMIT_SMALL),
        name="rmsnorm",
    )(x2d, g_row)


def _row_chunks(tm, sub=None):
    if sub is None:
        sizes, left = [], tm
        while left > 0:
            size = min(SUB_M, left)
            while size > TAIL_M and size * 2 > left:
                size //= 2
            sizes.append(size)
            left -= size
    else:
        sizes = [min(tm, sub)] * (tm // min(tm, sub))
    starts = np.cumsum([0] + sizes[:-1])
    assert sum(sizes) == tm
    return [(int(r), int(n)) for r, n in zip(starts, sizes)]


def _row_factor(inv_ref, r, n, width):
    inv = inv_ref[r:r + n, :]
    return jnp.concatenate([inv] * (width // LANES), axis=1)


def _lane_group_sum(y2):
    part = y2[:, 0:LANES]
    for c in range(LANES, y2.shape[1], LANES):
        part = part + y2[:, c:c + LANES]
    return part


def _mm_kernel(*refs, has_residual, has_row_scale, emits_norm):
    refs = list(refs)
    x_ref, w_ref = refs[0], refs[1]
    pos = 2
    r_ref = g_ref = inv_in_ref = None
    if has_residual:
        r_ref = refs[pos]; pos += 1
    if emits_norm:
        g_ref = refs[pos]; pos += 1
    if has_row_scale:
        inv_in_ref = refs[pos]; pos += 1
    o_ref = refs[pos]; pos += 1
    if emits_norm:
        xg_ref, ss_ref = refs[pos], refs[pos + 1]

        @pl.when(pl.program_id(1) == 0)
        def _():
            ss_ref[...] = jnp.zeros(ss_ref.shape, ss_ref.dtype)

    w = w_ref[...].astype(jnp.bfloat16)
    for r, n in _row_chunks(x_ref.shape[0]):
        y = jnp.dot(x_ref[r:r + n, :], w, preferred_element_type=jnp.float32)
        if has_row_scale:
            y = y * _row_factor(inv_in_ref, r, n, y.shape[1])
        if has_residual:
            y = r_ref[r:r + n, :] + y
        o_ref[r:r + n, :] = y.astype(o_ref.dtype)
        if emits_norm:
            xg_ref[r:r + n, :] = (y * g_ref[...]).astype(xg_ref.dtype)
            ss_ref[r:r + n, :] += _lane_group_sum(y * y)

    if emits_norm:
        @pl.when(pl.program_id(1) == pl.num_programs(1) - 1)
        def _():
            total = jnp.sum(ss_ref[...], axis=-1, keepdims=True)
            inv = lax.rsqrt(total * (1.0 / D_MODEL) + RMS_EPS)
            ss_ref[...] = jnp.broadcast_to(inv, ss_ref.shape)


def _panel_spec(tm, k, buffers=1):
    return pl.BlockSpec((tm, k), lambda i, j: (i, 0), pipeline_mode=pl.Buffered(buffers))


def _matmul(x, w_stack, layer, out_dtype, name, residual=None, row_inv=None, next_gain=None):
    m, k = x.shape
    n = w_stack.shape[-1]
    tm, tn, panel_buffers = TILES[name]
    in_specs = [_panel_spec(tm, k, panel_buffers),
                pl.BlockSpec((None, k, tn), lambda i, j: (layer, 0, j))]
    args = [x, w_stack]
    if residual is not None:
        in_specs.append(pl.BlockSpec((tm, tn), lambda i, j: (i, j)))
        args.append(residual)
    if next_gain is not None:
        in_specs.append(pl.BlockSpec((1, tn), lambda i, j: (0, j)))
        args.append(next_gain)
    if row_inv is not None:
        in_specs.append(pl.BlockSpec((tm, LANES), lambda i, j: (i, 0)))
        args.append(row_inv)
    out_specs = pl.BlockSpec((tm, tn), lambda i, j: (i, j))
    out_shape = jax.ShapeDtypeStruct((m, n), out_dtype)
    if next_gain is not None:
        out_specs = [out_specs,
                     pl.BlockSpec((tm, tn), lambda i, j: (i, j)),
                     pl.BlockSpec((tm, LANES), lambda i, j: (i, 0))]
        out_shape = [out_shape,
                     jax.ShapeDtypeStruct((m, n), jnp.bfloat16),
                     jax.ShapeDtypeStruct((m, LANES), jnp.float32)]
    return pl.pallas_call(
        functools.partial(_mm_kernel, has_residual=residual is not None,
                          has_row_scale=row_inv is not None,
                          emits_norm=next_gain is not None),
        grid=(m // tm, n // tn),
        in_specs=in_specs,
        out_specs=out_specs,
        out_shape=out_shape,
        compiler_params=_params(("parallel", "arbitrary"), VMEM_LIMIT_MATMUL),
        name=name,
    )(*args)


def _mm_final_norm_kernel(x_ref, w_ref, r_ref, g_ref, o_ref, ss_ref):
    j = pl.program_id(1)
    tn = w_ref.shape[1]

    @pl.when(j == 0)
    def _():
        ss_ref[...] = jnp.zeros(ss_ref.shape, ss_ref.dtype)

    w = w_ref[...].astype(jnp.bfloat16)
    col = pl.multiple_of(j * tn, tn)
    for r, n in _row_chunks(x_ref.shape[0]):
        y = r_ref[r:r + n, :] + jnp.dot(x_ref[r:r + n, :], w,
                                        preferred_element_type=jnp.float32)
        o_ref[r:r + n, pl.ds(col, tn)] = y
        ss_ref[r:r + n, :] += _lane_group_sum(y * y)

    @pl.when(j == pl.num_programs(1) - 1)
    def _():
        gain = g_ref[...]
        for r in range(0, o_ref.shape[0], BLOCK):
            total = jnp.sum(ss_ref[r:r + BLOCK, :], axis=-1, keepdims=True)
            inv = lax.rsqrt(total * (1.0 / D_MODEL) + RMS_EPS)
            o_ref[r:r + BLOCK, :] = o_ref[r:r + BLOCK, :] * inv * gain


def _matmul_final_norm(x, w, residual, gain, name):
    m, k = x.shape
    n = w.shape[-1]
    tm, tn, panel_buffers = TILES[name]
    return pl.pallas_call(
        _mm_final_norm_kernel,
        grid=(m // tm, n // tn),
        in_specs=[_panel_spec(tm, k, panel_buffers),
                  pl.BlockSpec((k, tn), lambda i, j: (0, j)),
                  pl.BlockSpec((tm, tn), lambda i, j: (i, j)),
                  pl.BlockSpec((1, n), lambda i, j: (0, 0))],
        out_specs=pl.BlockSpec((tm, n), lambda i, j: (i, 0), pipeline_mode=pl.Buffered(1)),
        out_shape=jax.ShapeDtypeStruct((m, n), jnp.float32),
        scratch_shapes=[pltpu.VMEM((tm, LANES), jnp.float32)],
        compiler_params=_params(("parallel", "arbitrary"), VMEM_LIMIT_MATMUL),
        name=name + "_final_norm",
    )(x, w, residual, gain)


def _rotary_mix(t, lane):
    half = ROT_DIM // 2
    return jnp.where(lane < half, pltpu.roll(t, HEAD_DIM - half, 1), pltpu.roll(t, half, 1))


def _attn_kernel(sink_ref, q_ref, k_ref, v_ref, cq_ref, sq_ref, ck_ref, sk_ref, bias_ref,
                 o_ref, kpad_ref, vpad_ref, *, seq):
    g = pl.program_id(1)
    nb = seq // BLOCK
    zeros = jnp.zeros((BLOCK, HEAD_DIM), jnp.bfloat16)
    kpad_ref[0:BLOCK, :] = zeros
    kpad_ref[BLOCK + seq:, :] = zeros
    vpad_ref[0:BLOCK, :] = zeros
    vpad_ref[BLOCK + seq:, :] = zeros
    lane_full = lax.broadcasted_iota(jnp.int32, (seq, HEAD_DIM), 1)
    k = k_ref[...]
    k_rot = k * ck_ref[...] + _rotary_mix(k, lane_full) * sk_ref[...]
    kpad_ref[BLOCK:BLOCK + seq, :] = k_rot.astype(jnp.bfloat16)
    vpad_ref[BLOCK:BLOCK + seq, :] = v_ref[...].astype(jnp.bfloat16)

    rows = GROUP * BLOCK
    row_head = lax.broadcasted_iota(jnp.int32, (rows, 1), 0) // BLOCK
    sink = jnp.zeros((rows, 1), jnp.float32)
    for h in range(GROUP):
        sink = jnp.where(row_head == h, sink_ref[g * GROUP + h], sink)

    def body(n, carry):
        r0 = pl.multiple_of(n * BLOCK, BLOCK)
        lane_blk = lax.broadcasted_iota(jnp.int32, (BLOCK, HEAD_DIM), 1)
        cq = cq_ref[pl.ds(r0, BLOCK), :]
        sq = sq_ref[pl.ds(r0, BLOCK), :]
        heads = []
        for h in range(GROUP):
            qh = q_ref[pl.ds(r0, BLOCK), h * HEAD_DIM:(h + 1) * HEAD_DIM]
            heads.append((qh * cq + _rotary_mix(qh, lane_blk) * sq).astype(jnp.bfloat16))
        q = jnp.concatenate(heads, axis=0)
        kw = kpad_ref[pl.ds(r0, 3 * BLOCK), :]
        vw = vpad_ref[pl.ds(r0, 3 * BLOCK), :]
        s = lax.dot_general(q, kw, (((1,), (1,)), ((), ())),
                            preferred_element_type=jnp.float32)
        which = jnp.where(n == 0, 0, jnp.where(n == nb - 1, 2, 1))
        s = s + jnp.concatenate([bias_ref[which]] * GROUP, axis=0)
        m = jnp.maximum(jnp.max(s, axis=-1, keepdims=True), sink)
        p = jnp.exp(s - m)
        denom = jnp.sum(p, axis=-1, keepdims=True) + jnp.exp(sink - m)
        o = jnp.dot(p.astype(jnp.bfloat16), vw, preferred_element_type=jnp.float32)
        o = o / denom
        for h in range(GROUP):
            o_ref[pl.ds(r0, BLOCK), h * HEAD_DIM:(h + 1) * HEAD_DIM] = (
                o[h * BLOCK:(h + 1) * BLOCK, :].astype(o_ref.dtype))
        return carry

    lax.fori_loop(0, nb, body, 0, unroll=2)


def _window_mask_bias():
    qi = np.arange(BLOCK)[:, None]
    kj = np.arange(3 * BLOCK)[None, :]
    band = (kj >= qi) & (kj <= qi + 2 * BLOCK)
    first = band & (kj >= BLOCK)
    last = band & (kj < 2 * BLOCK)
    return np.where(np.stack([first, band, last]), 0.0, MASK_NEG).astype(np.float32)


def _attention(proj, sink_l, tables, batch, seq):
    cq, sq, ck, sk = tables
    m = proj.shape[0]
    qw = GROUP * HEAD_DIM
    assert seq // BLOCK >= 2
    bias = jnp.asarray(_window_mask_bias())
    tab_spec = pl.BlockSpec((seq, HEAD_DIM), lambda b, g: (0, 0))
    return pl.pallas_call(
        functools.partial(_attn_kernel, seq=seq),
        grid=(batch, N_KV_HEADS),
        in_specs=[pl.BlockSpec(memory_space=pltpu.SMEM),
                  pl.BlockSpec((seq, qw), lambda b, g: (b, Q_OFF // qw + g)),
                  pl.BlockSpec((seq, HEAD_DIM), lambda b, g: (b, K_OFF // HEAD_DIM + g)),
                  pl.BlockSpec((seq, HEAD_DIM), lambda b, g: (b, V_OFF // HEAD_DIM + g)),
                  tab_spec, tab_spec, tab_spec, tab_spec,
                  pl.BlockSpec(bias.shape, lambda b, g: (0, 0, 0))],
        out_specs=pl.BlockSpec((seq, qw), lambda b, g: (b, g)),
        out_shape=jax.ShapeDtypeStruct((m, ATTN_WIDTH), jnp.bfloat16),
        scratch_shapes=[pltpu.VMEM((seq + 2 * BLOCK, HEAD_DIM), jnp.bfloat16),
                        pltpu.VMEM((seq + 2 * BLOCK, HEAD_DIM), jnp.bfloat16)],
        compiler_params=_params(("parallel", "arbitrary"), VMEM_LIMIT_SMALL),
        name="window_gqa",
    )(sink_l, proj, proj, proj, cq, sq, ck, sk, bias)


def _pool_kernel(u_ref, w_ref, scale_ref, o_ref, *, seq):
    g = pl.program_id(1)
    zeros = jnp.zeros((POOL_HALO, POOL_GROUP_DIM), jnp.float32)
    wmat = w_ref[...].astype(jnp.bfloat16)
    scale = scale_ref[...]
    n_chunks = seq // POOL_CHUNK

    def run(win):
        half = win // 2
        ext = POOL_CHUNK + 2 * POOL_HALO

        def chunk(t0, xe):
            a = xe
            step = 1
            while step < half:
                a = a + pltpu.roll(a, ext - step, 0)
                step *= 2
            a = a + pltpu.roll(a, half, 0)
            wsum = a[POOL_HALO:POOL_HALO + POOL_CHUNK, :]
            x = xe[POOL_HALO:POOL_HALO + POOL_CHUNK, :]
            t = t0 + lax.broadcasted_iota(jnp.int32, (POOL_CHUNK, 1), 0)
            lo = jnp.maximum(t - half, 0)
            hi = jnp.minimum(t + half - 1, seq - 1)
            cnt = (hi - lo + 1).astype(jnp.float32)
            pooled = wsum / cnt - x
            mixed = jnp.dot(pooled.astype(jnp.bfloat16), wmat,
                            preferred_element_type=jnp.float32)
            o_ref[pl.ds(t0, POOL_CHUNK), :] = (mixed * scale).astype(o_ref.dtype)

        def body(c, carry):
            t0 = pl.multiple_of(c * POOL_CHUNK, POOL_CHUNK)
            start = pl.multiple_of(t0 - POOL_HALO, POOL_HALO)
            chunk(t0, u_ref[pl.ds(start, ext), :])
            return carry

        chunk(0, jnp.concatenate([zeros, u_ref[0:POOL_CHUNK + POOL_HALO, :]], axis=0))
        lax.fori_loop(1, n_chunks - 1, body, 0, unroll=2)
        last = seq - POOL_CHUNK
        chunk(last, jnp.concatenate([u_ref[last - POOL_HALO:seq, :], zeros], axis=0))

    for gi, win in enumerate(POOL_WINDOWS):
        pl.when(g == gi)(functools.partial(run, win))


def _pool(proj, pool_w, pool_scale3, layer, batch, seq):
    m = proj.shape[0]
    c = POOL_GROUP_DIM
    return pl.pallas_call(
        functools.partial(_pool_kernel, seq=seq),
        grid=(batch, N_POOL_GROUPS),
        in_specs=[pl.BlockSpec((seq, c), lambda b, g: (b, U_OFF // c + g)),
                  pl.BlockSpec((None, None, c, c), lambda b, g: (layer, g, 0, 0)),
                  pl.BlockSpec((None, 1, c), lambda b, g: (layer, 0, g))],
        out_specs=pl.BlockSpec((seq, c), lambda b, g: (b, g)),
        out_shape=jax.ShapeDtypeStruct((m, POOL_WIDTH), jnp.bfloat16),
        compiler_params=_params(("parallel", "arbitrary"), VMEM_LIMIT_SMALL),
        name="multiscale_pool",
    )(proj, pool_w, pool_scale3)


def _merge_kernel(a_ref, p_ref, wa_ref, wp_ref, ga_ref, gb_ref, o_ref):
    wa = wa_ref[...].astype(jnp.bfloat16)
    wp = wp_ref[...].astype(jnp.bfloat16)
    for r, n in _row_chunks(a_ref.shape[0], MERGE_SUB_M):
        ya = jnp.dot(a_ref[r:r + n, :], wa, preferred_element_type=jnp.float32)
        yp = jnp.dot(p_ref[r:r + n, :], wp, preferred_element_type=jnp.float32)
        merged = (jax.nn.sigmoid(ga_ref[r:r + n, :]) * ya
                  + jax.nn.sigmoid(gb_ref[r:r + n, :]) * yp)
        o_ref[r:r + n, :] = merged.astype(o_ref.dtype)


def _merge(o_attn, mixed, wba, wbp, proj, layer):
    m = o_attn.shape[0]
    tm, tn, panel_buffers = TILES["merge"]
    return pl.pallas_call(
        _merge_kernel,
        grid=(m // tm, D_MODEL // tn),
        in_specs=[_panel_spec(tm, ATTN_WIDTH, panel_buffers),
                  _panel_spec(tm, POOL_WIDTH, panel_buffers),
                  pl.BlockSpec((None, ATTN_WIDTH, tn), lambda i, j: (layer, 0, j)),
                  pl.BlockSpec((None, POOL_WIDTH, tn), lambda i, j: (layer, 0, j)),
                  pl.BlockSpec((tm, tn), lambda i, j: (i, GA_OFF // tn + j)),
                  pl.BlockSpec((tm, tn), lambda i, j: (i, GB_OFF // tn + j))],
        out_specs=pl.BlockSpec((tm, tn), lambda i, j: (i, j)),
        out_shape=jax.ShapeDtypeStruct((m, D_MODEL), jnp.bfloat16),
        compiler_params=_params(("parallel", "arbitrary"), VMEM_LIMIT_MATMUL),
        name="branch_merge",
    )(o_attn, mixed, wba, wbp, proj, proj)


def _gate_up_kernel(x_ref, inv_ref, wg_ref, wu_ref, wd_ref, o_ref, wd_bf16_ref):
    wd_bf16_ref[...] = wd_ref[...].astype(jnp.bfloat16)
    wg = wg_ref[...].astype(jnp.bfloat16)
    wu = wu_ref[...].astype(jnp.bfloat16)
    for r, n in _row_chunks(x_ref.shape[0]):
        x = x_ref[r:r + n, :]
        inv = _row_factor(inv_ref, r, n, wg.shape[1])
        gate = jnp.dot(x, wg, preferred_element_type=jnp.float32) * inv
        up = jnp.dot(x, wu, preferred_element_type=jnp.float32) * inv
        o_ref[r:r + n, :] = (jax.nn.silu(gate) * up).astype(o_ref.dtype)


def _gate_up(xg, row_inv, w_gate_up, w_down, layer):
    m, k = xg.shape
    tm, tn, panel_buffers = TILES["gate_up"]
    nt = D_FF // tn
    steps = (m // tm) * nt
    slab = D_FF // steps
    assert slab * steps == D_FF and slab % 16 == 0
    return pl.pallas_call(
        _gate_up_kernel,
        grid=(m // tm, nt),
        in_specs=[_panel_spec(tm, k, panel_buffers),
                  pl.BlockSpec((tm, LANES), lambda i, j: (i, 0)),
                  pl.BlockSpec((None, k, tn), lambda i, j: (layer, 0, j)),
                  pl.BlockSpec((None, k, tn), lambda i, j: (layer, 0, nt + j)),
                  pl.BlockSpec((None, slab, D_MODEL), lambda i, j: (layer, i * nt + j, 0))],
        out_specs=[pl.BlockSpec((tm, tn), lambda i, j: (i, j)),
                   pl.BlockSpec((slab, D_MODEL), lambda i, j: (i * nt + j, 0))],
        out_shape=[jax.ShapeDtypeStruct((m, D_FF), jnp.bfloat16),
                   jax.ShapeDtypeStruct((D_FF, D_MODEL), jnp.bfloat16)],
        compiler_params=_params(("arbitrary", "arbitrary"), VMEM_LIMIT_MATMUL),
        name="swiglu_gate_up",
    )(xg, row_inv, w_gate_up, w_gate_up, w_down)


def _rotary_tables(seq):
    pos = jnp.arange(seq, dtype=jnp.float32)
    inv_freq = 1.0 / jnp.power(jnp.float32(ROPE_THETA),
                               jnp.arange(0, ROT_DIM, 2, dtype=jnp.float32) / ROT_DIM)
    ang = pos[:, None] * inv_freq[None, :]
    cos, sin = jnp.cos(ang), jnp.sin(ang)
    ones = jnp.ones((seq, HEAD_DIM - ROT_DIM), jnp.float32)
    zeros = jnp.zeros((seq, HEAD_DIM - ROT_DIM), jnp.float32)
    c_full = jnp.concatenate([cos, cos, ones], axis=-1)
    s_full = jnp.concatenate([-sin, sin, zeros], axis=-1)
    scale = HEAD_DIM ** -0.5
    return c_full * scale, s_full * scale, c_full, s_full


def kernel(x, norm1_g, w_in, attn_sink, pool_w, pool_scale, w_branch_attn, w_branch_pool,
           w_out, norm2_g, w_gate_up, w_down, final_norm_g):
    batch, seq, d = x.shape
    m = batch * seq
    bf16 = jnp.bfloat16
    tables = _rotary_tables(seq)
    pool_scale3 = pool_scale.reshape(DEPTH, 1, POOL_WIDTH)

    xs = x.reshape(m, d)
    h = _rmsnorm(xs, norm1_g[0].reshape(1, d), bf16)
    row_inv = None
    for l in range(DEPTH):
        proj = _matmul(h, w_in, l, jnp.float32, "in_proj", row_inv=row_inv)
        o_attn = _attention(proj, attn_sink[l], tables, batch, seq)
        mixed = _pool(proj, pool_w, pool_scale3, l, batch, seq)
        merged = _merge(o_attn, mixed, w_branch_attn, w_branch_pool, proj, l)
        xs, xg, row_inv = _matmul(merged, w_out, l, jnp.float32, "out_proj", residual=xs,
                                  next_gain=norm2_g[l].reshape(1, d))
        act, w_down_b = _gate_up(xg, row_inv, w_gate_up, w_down, l)
        if l + 1 < DEPTH:
            xs, h, row_inv = _matmul(act, w_down_b[None], 0, jnp.float32, "ffn_down",
                                     residual=xs, next_gain=norm1_g[l + 1].reshape(1, d))
        else:
            out = _matmul_final_norm(act, w_down_b, xs, final_norm_g.reshape(1, d), "ffn_down")
    return out.reshape(batch, seq, d)
```

```python
import functools
from typing import NamedTuple

import jax
import jax.numpy as jnp
import numpy as np
from jax import lax
from jax.experimental import pallas as pl
from jax.experimental.pallas import tpu as pltpu

D_MODEL = 4096
DEPTH = 2
HEAD_DIM = 128
N_HEADS = 16
N_KV_HEADS = 4
GROUP = N_HEADS // N_KV_HEADS
ATTN_WIDTH = N_HEADS * HEAD_DIM
KV_WIDTH = N_KV_HEADS * HEAD_DIM
BLOCK = 128
ROPE_THETA = 500000.0
ROT_DIM = HEAD_DIM // 4
POOL_WIDTH = D_MODEL // 2
POOL_WINDOWS = (2, 4, 8, 16)
N_POOL_GROUPS = len(POOL_WINDOWS)
POOL_GROUP_DIM = POOL_WIDTH // N_POOL_GROUPS
Q_OFF = 0
K_OFF = ATTN_WIDTH
V_OFF = K_OFF + KV_WIDTH
U_OFF = V_OFF + KV_WIDTH
GA_OFF = U_OFF + POOL_WIDTH
GB_OFF = GA_OFF + D_MODEL
IN_WIDTH = GB_OFF + D_MODEL
D_FF = 11008
RMS_EPS = 1e-6
MASK_NEG = -1e30

POOL_HALO = 8
POOL_CHUNK = 256
LANES = 128
SUB_M = 1024
TAIL_M = 256
MERGE_SUB_M = 256
MIB = 1024 * 1024
VMEM_BYTES = 64 * MIB
VMEM_COMPILER_RESERVE = 8 * MIB
VMEM_LIMIT_MATMUL = VMEM_BYTES - VMEM_COMPILER_RESERVE
VMEM_LIMIT_SMALL = 40 * MIB


class _Tiles(NamedTuple):
    tm: int
    tn: int
    panel_buffers: int


TILES = {
    "in_proj": _Tiles(2048, 512, 1),
    "merge": _Tiles(2048, 256, 2),
    "out_proj": _Tiles(2048, 256, 2),
    "gate_up": _Tiles(2048, 256, 1),
    "ffn_down": _Tiles(512, 512, 2),
}


def _params(semantics, vmem_bytes):
    return pltpu.CompilerParams(dimension_semantics=semantics, vmem_limit_bytes=vmem_bytes)


def _rmsnorm_kernel(x_ref, g_ref, o_ref):
    x = x_ref[...]
    ms = jnp.mean(x * x, axis=-1, keepdims=True)
    y = x * lax.rsqrt(ms + RMS_EPS)
    o_ref[...] = (y * g_ref[...]).astype(o_ref.dtype)


def _rmsnorm(x2d, g_row, out_dtype, tr=256):
    m, d = x2d.shape
    return pl.pallas_call(
        _rmsnorm_kernel,
        grid=(m // tr,),
        in_specs=[pl.BlockSpec((tr, d), lambda i: (i, 0)),
                  pl.BlockSpec((1, d), lambda i: (0, 0))],
        out_specs=pl.BlockSpec((tr, d), lambda i: (i, 0)),
        out_shape=jax.ShapeDtypeStruct((m, d), out_dtype),
        compiler_params=_params(("parallel",), VMEM_LIMIT_SMALL),
        name="rmsnorm",
    )(x2d, g_row)


def _row_chunks(tm, sub=None):
    if sub is None:
        sizes, left = [], tm
        while left > 0:
            size = min(SUB_M, left)
            while size > TAIL_M and size * 2 > left:
                size //= 2
            sizes.append(size)
            left -= size
    else:
        sizes = [min(tm, sub)] * (tm // min(tm, sub))
    starts = np.cumsum([0] + sizes[:-1])
    assert sum(sizes) == tm
    return [(int(r), int(n)) for r, n in zip(starts, sizes)]


def _row_factor(inv_ref, r, n, width):
    inv = inv_ref[r:r + n, :]
    return jnp.concatenate([inv] * (width // LANES), axis=1)


def _lane_group_sum(y2):
    part = y2[:, 0:LANES]
    for c in range(LANES, y2.shape[1], LANES):
        part = part + y2[:, c:c + LANES]
    return part


def _mm_kernel(*refs, has_residual, has_row_scale, emits_norm):
    refs = list(refs)
    x_ref, w_ref = refs[0], refs[1]
    pos = 2
    r_ref = g_ref = inv_in_ref = None
    if has_residual:
        r_ref = refs[pos]; pos += 1
    if emits_norm:
        g_ref = refs[pos]; pos += 1
    if has_row_scale:
        inv_in_ref = refs[pos]; pos += 1
    o_ref = refs[pos]; pos += 1
    if emits_norm:
        xg_ref, ss_ref = refs[pos], refs[pos + 1]

        @pl.when(pl.program_id(1) == 0)
        def _():
            ss_ref[...] = jnp.zeros(ss_ref.shape, ss_ref.dtype)

    w = w_ref[...].astype(jnp.bfloat16)
    for r, n in _row_chunks(x_ref.shape[0]):
        y = jnp.dot(x_ref[r:r + n, :], w, preferred_element_type=jnp.float32)
        if has_row_scale:
            y = y * _row_factor(inv_in_ref, r, n, y.shape[1])
        if has_residual:
            y = r_ref[r:r + n, :] + y
        o_ref[r:r + n, :] = y.astype(o_ref.dtype)
        if emits_norm:
            xg_ref[r:r + n, :] = (y * g_ref[...]).astype(xg_ref.dtype)
            ss_ref[r:r + n, :] += _lane_group_sum(y * y)

    if emits_norm:
        @pl.when(pl.program_id(1) == pl.num_programs(1) - 1)
        def _():
            total = jnp.sum(ss_ref[...], axis=-1, keepdims=True)
            inv = lax.rsqrt(total * (1.0 / D_MODEL) + RMS_EPS)
            ss_ref[...] = jnp.broadcast_to(inv, ss_ref.shape)


def _panel_spec(tm, k, buffers=1):
    return pl.BlockSpec((tm, k), lambda i, j: (i, 0), pipeline_mode=pl.Buffered(buffers))


def _matmul(x, w_stack, layer, out_dtype, name, residual=None, row_inv=None, next_gain=None):
    m, k = x.shape
    n = w_stack.shape[-1]
    tm, tn, panel_buffers = TILES[name]
    in_specs = [_panel_spec(tm, k, panel_buffers),
                pl.BlockSpec((None, k, tn), lambda i, j: (layer, 0, j))]
    args = [x, w_stack]
    if residual is not None:
        in_specs.append(pl.BlockSpec((tm, tn), lambda i, j: (i, j)))
        args.append(residual)
    if next_gain is not None:
        in_specs.append(pl.BlockSpec((1, tn), lambda i, j: (0, j)))
        args.append(next_gain)
    if row_inv is not None:
        in_specs.append(pl.BlockSpec((tm, LANES), lambda i, j: (i, 0)))
        args.append(row_inv)
    out_specs = pl.BlockSpec((tm, tn), lambda i, j: (i, j))
    out_shape = jax.ShapeDtypeStruct((m, n), out_dtype)
    if next_gain is not None:
        out_specs = [out_specs,
                     pl.BlockSpec((tm, tn), lambda i, j: (i, j)),
                     pl.BlockSpec((tm, LANES), lambda i, j: (i, 0))]
        out_shape = [out_shape,
                     jax.ShapeDtypeStruct((m, n), jnp.bfloat16),
                     jax.ShapeDtypeStruct((m, LANES), jnp.float32)]
    return pl.pallas_call(
        functools.partial(_mm_kernel, has_residual=residual is not None,
                          has_row_scale=row_inv is not None,
                          emits_norm=next_gain is not None),
        grid=(m // tm, n // tn),
        in_specs=in_specs,
        out_specs=out_specs,
        out_shape=out_shape,
        compiler_params=_params(("parallel", "arbitrary"), VMEM_LIMIT_MATMUL),
        name=name,
    )(*args)


def _mm_final_norm_kernel(x_ref, w_ref, r_ref, g_ref, o_ref, ss_ref):
    j = pl.program_id(1)
    tn = w_ref.shape[1]

    @pl.when(j == 0)
    def _():
        ss_ref[...] = jnp.zeros(ss_ref.shape, ss_ref.dtype)

    w = w_ref[...].astype(jnp.bfloat16)
    col = pl.multiple_of(j * tn, tn)
    for r, n in _row_chunks(x_ref.shape[0]):
        y = r_ref[r:r + n, :] + jnp.dot(x_ref[r:r + n, :], w,
                                        preferred_element_type=jnp.float32)
        o_ref[r:r + n, pl.ds(col, tn)] = y
        ss_ref[r:r + n, :] += _lane_group_sum(y * y)

    @pl.when(j == pl.num_programs(1) - 1)
    def _():
        gain = g_ref[...]
        for r in range(0, o_ref.shape[0], BLOCK):
            total = jnp.sum(ss_ref[r:r + BLOCK, :], axis=-1, keepdims=True)
            inv = lax.rsqrt(total * (1.0 / D_MODEL) + RMS_EPS)
            o_ref[r:r + BLOCK, :] = o_ref[r:r + BLOCK, :] * inv * gain


def _matmul_final_norm(x, w, residual, gain, name):
    m, k = x.shape
    n = w.shape[-1]
    tm, tn, panel_buffers = TILES[name]
    return pl.pallas_call(
        _mm_final_norm_kernel,
        grid=(m // tm, n // tn),
        in_specs=[_panel_spec(tm, k, panel_buffers),
                  pl.BlockSpec((k, tn), lambda i, j: (0, j)),
                  pl.BlockSpec((tm, tn), lambda i, j: (i, j)),
                  pl.BlockSpec((1, n), lambda i, j: (0, 0))],
        out_specs=pl.BlockSpec((tm, n), lambda i, j: (i, 0), pipeline_mode=pl.Buffered(1)),
        out_shape=jax.ShapeDtypeStruct((m, n), jnp.float32),
        scratch_shapes=[pltpu.VMEM((tm, LANES), jnp.float32)],
        compiler_params=_params(("parallel", "arbitrary"), VMEM_LIMIT_MATMUL),
        name=name + "_final_norm",
    )(x, w, residual, gain)


def _rotary_mix(t, lane):
    half = ROT_DIM // 2
    return jnp.where(lane < half, pltpu.roll(t, HEAD_DIM - half, 1), pltpu.roll(t, half, 1))


def _attn_kernel(sink_ref, q_ref, k_ref, v_ref, cq_ref, sq_ref, ck_ref, sk_ref, bias_ref,
                 o_ref, kpad_ref, vpad_ref, *, seq):
    g = pl.program_id(1)
    nb = seq // BLOCK
    zeros = jnp.zeros((BLOCK, HEAD_DIM), jnp.bfloat16)
    kpad_ref[0:BLOCK, :] = zeros
    kpad_ref[BLOCK + seq:, :] = zeros
    vpad_ref[0:BLOCK, :] = zeros
    vpad_ref[BLOCK + seq:, :] = zeros
    lane_full = lax.broadcasted_iota(jnp.int32, (seq, HEAD_DIM), 1)
    k = k_ref[...]
    k_rot = k * ck_ref[...] + _rotary_mix(k, lane_full) * sk_ref[...]
    kpad_ref[BLOCK:BLOCK + seq, :] = k_rot.astype(jnp.bfloat16)
    vpad_ref[BLOCK:BLOCK + seq, :] = v_ref[...].astype(jnp.bfloat16)

    rows = GROUP * BLOCK
    row_head = lax.broadcasted_iota(jnp.int32, (rows, 1), 0) // BLOCK
    sink = jnp.zeros((rows, 1), jnp.float32)
    for h in range(GROUP):
        sink = jnp.where(row_head == h, sink_ref[g * GROUP + h], sink)

    def body(n, carry):
        r0 = pl.multiple_of(n * BLOCK, BLOCK)
        lane_blk = lax.broadcasted_iota(jnp.int32, (BLOCK, HEAD_DIM), 1)
        cq = cq_ref[pl.ds(r0, BLOCK), :]
        sq = sq_ref[pl.ds(r0, BLOCK), :]
        heads = []
        for h in range(GROUP):
            qh = q_ref[pl.ds(r0, BLOCK), h * HEAD_DIM:(h + 1) * HEAD_DIM]
            heads.append((qh * cq + _rotary_mix(qh, lane_blk) * sq).astype(jnp.bfloat16))
        q = jnp.concatenate(heads, axis=0)
        kw = kpad_ref[pl.ds(r0, 3 * BLOCK), :]
        vw = vpad_ref[pl.ds(r0, 3 * BLOCK), :]
        s = lax.dot_general(q, kw, (((1,), (1,)), ((), ())),
                            preferred_element_type=jnp.float32)
        which = jnp.where(n == 0, 0, jnp.where(n == nb - 1, 2, 1))
        s = s + jnp.concatenate([bias_ref[which]] * GROUP, axis=0)
        m = jnp.maximum(jnp.max(s, axis=-1, keepdims=True), sink)
        p = jnp.exp(s - m)
        denom = jnp.sum(p, axis=-1, keepdims=True) + jnp.exp(sink - m)
        o = jnp.dot(p.astype(jnp.bfloat16), vw, preferred_element_type=jnp.float32)
        o = o / denom
        for h in range(GROUP):
            o_ref[pl.ds(r0, BLOCK), h * HEAD_DIM:(h + 1) * HEAD_DIM] = (
                o[h * BLOCK:(h + 1) * BLOCK, :].astype(o_ref.dtype))
        return carry

    lax.fori_loop(0, nb, body, 0, unroll=2)


def _window_mask_bias():
    qi = np.arange(BLOCK)[:, None]
    kj = np.arange(3 * BLOCK)[None, :]
    band = (kj >= qi) & (kj <= qi + 2 * BLOCK)
    first = band & (kj >= BLOCK)
    last = band & (kj < 2 * BLOCK)
    return np.where(np.stack([first, band, last]), 0.0, MASK_NEG).astype(np.float32)


def _attention(proj, sink_l, tables, batch, seq):
    cq, sq, ck, sk = tables
    m = proj.shape[0]
    qw = GROUP * HEAD_DIM
    assert seq // BLOCK >= 2
    bias = jnp.asarray(_window_mask_bias())
    tab_spec = pl.BlockSpec((seq, HEAD_DIM), lambda b, g: (0, 0))
    return pl.pallas_call(
        functools.partial(_attn_kernel, seq=seq),
        grid=(batch, N_KV_HEADS),
        in_specs=[pl.BlockSpec(memory_space=pltpu.SMEM),
                  pl.BlockSpec((seq, qw), lambda b, g: (b, Q_OFF // qw + g)),
                  pl.BlockSpec((seq, HEAD_DIM), lambda b, g: (b, K_OFF // HEAD_DIM + g)),
                  pl.BlockSpec((seq, HEAD_DIM), lambda b, g: (b, V_OFF // HEAD_DIM + g)),
                  tab_spec, tab_spec, tab_spec, tab_spec,
                  pl.BlockSpec(bias.shape, lambda b, g: (0, 0, 0))],
        out_specs=pl.BlockSpec((seq, qw), lambda b, g: (b, g)),
        out_shape=jax.ShapeDtypeStruct((m, ATTN_WIDTH), jnp.bfloat16),
        scratch_shapes=[pltpu.VMEM((seq + 2 * BLOCK, HEAD_DIM), jnp.bfloat16),
                        pltpu.VMEM((seq + 2 * BLOCK, HEAD_DIM), jnp.bfloat16)],
        compiler_params=_params(("parallel", "arbitrary"), VMEM_LIMIT_SMALL),
        name="window_gqa",
    )(sink_l, proj, proj, proj, cq, sq, ck, sk, bias)


def _pool_kernel(u_ref, w_ref, scale_ref, o_ref, *, seq):
    g = pl.program_id(1)
    zeros = jnp.zeros((POOL_HALO, POOL_GROUP_DIM), jnp.float32)
    wmat = w_ref[...].astype(jnp.bfloat16)
    scale = scale_ref[...]
    n_chunks = seq // POOL_CHUNK

    def run(win):
        half = win // 2
        ext = POOL_CHUNK + 2 * POOL_HALO

        def chunk(t0, xe):
            a = xe
            step = 1
            while step < half:
                a = a + pltpu.roll(a, ext - step, 0)
                step *= 2
            a = a + pltpu.roll(a, half, 0)
            wsum = a[POOL_HALO:POOL_HALO + POOL_CHUNK, :]
            x = xe[POOL_HALO:POOL_HALO + POOL_CHUNK, :]
            t = t0 + lax.broadcasted_iota(jnp.int32, (POOL_CHUNK, 1), 0)
            lo = jnp.maximum(t - half, 0)
            hi = jnp.minimum(t + half - 1, seq - 1)
            cnt = (hi - lo + 1).astype(jnp.float32)
            pooled = wsum / cnt - x
            mixed = jnp.dot(pooled.astype(jnp.bfloat16), wmat,
                            preferred_element_type=jnp.float32)
            o_ref[pl.ds(t0, POOL_CHUNK), :] = (mixed * scale).astype(o_ref.dtype)

        def body(c, carry):
            t0 = pl.multiple_of(c * POOL_CHUNK, POOL_CHUNK)
            start = pl.multiple_of(t0 - POOL_HALO, POOL_HALO)
            chunk(t0, u_ref[pl.ds(start, ext), :])
            return carry

        chunk(0, jnp.concatenate([zeros, u_ref[0:POOL_CHUNK + POOL_HALO, :]], axis=0))
        lax.fori_loop(1, n_chunks - 1, body, 0, unroll=2)
        last = seq - POOL_CHUNK
        chunk(last, jnp.concatenate([u_ref[last - POOL_HALO:seq, :], zeros], axis=0))

    for gi, win in enumerate(POOL_WINDOWS):
        pl.when(g == gi)(functools.partial(run, win))


def _pool(proj, pool_w, pool_scale3, layer, batch, seq):
    m = proj.shape[0]
    c = POOL_GROUP_DIM
    return pl.pallas_call(
        functools.partial(_pool_kernel, seq=seq),
        grid=(batch, N_POOL_GROUPS),
        in_specs=[pl.BlockSpec((seq, c), lambda b, g: (b, U_OFF // c + g)),
                  pl.BlockSpec((None, None, c, c), lambda b, g: (layer, g, 0, 0)),
                  pl.BlockSpec((None, 1, c), lambda b, g: (layer, 0, g))],
        out_specs=pl.BlockSpec((seq, c), lambda b, g: (b, g)),
        out_shape=jax.ShapeDtypeStruct((m, POOL_WIDTH), jnp.bfloat16),
        compiler_params=_params(("parallel", "arbitrary"), VMEM_LIMIT_SMALL),
        name="multiscale_pool",
    )(proj, pool_w, pool_scale3)


def _merge_kernel(a_ref, p_ref, wa_ref, wp_ref, ga_ref, gb_ref, o_ref):
    wa = wa_ref[...].astype(jnp.bfloat16)
    wp = wp_ref[...].astype(jnp.bfloat16)
    for r, n in _row_chunks(a_ref.shape[0], MERGE_SUB_M):
        ya = jnp.dot(a_ref[r:r + n, :], wa, preferred_element_type=jnp.float32)
        yp = jnp.dot(p_ref[r:r + n, :], wp, preferred_element_type=jnp.float32)
        merged = (jax.nn.sigmoid(ga_ref[r:r + n, :]) * ya
                  + jax.nn.sigmoid(gb_ref[r:r + n, :]) * yp)
        o_ref[r:r + n, :] = merged.astype(o_ref.dtype)


def _merge(o_attn, mixed, wba, wbp, proj, layer):
    m = o_attn.shape[0]
    tm, tn, panel_buffers = TILES["merge"]
    return pl.pallas_call(
        _merge_kernel,
        grid=(m // tm, D_MODEL // tn),
        in_specs=[_panel_spec(tm, ATTN_WIDTH, panel_buffers),
                  _panel_spec(tm, POOL_WIDTH, panel_buffers),
                  pl.BlockSpec((None, ATTN_WIDTH, tn), lambda i, j: (layer, 0, j)),
                  pl.BlockSpec((None, POOL_WIDTH, tn), lambda i, j: (layer, 0, j)),
                  pl.BlockSpec((tm, tn), lambda i, j: (i, GA_OFF // tn + j)),
                  pl.BlockSpec((tm, tn), lambda i, j: (i, GB_OFF // tn + j))],
        out_specs=pl.BlockSpec((tm, tn), lambda i, j: (i, j)),
        out_shape=jax.ShapeDtypeStruct((m, D_MODEL), jnp.bfloat16),
        compiler_params=_params(("parallel", "arbitrary"), VMEM_LIMIT_MATMUL),
        name="branch_merge",
    )(o_attn, mixed, wba, wbp, proj, proj)


def _gate_up_kernel(x_ref, inv_ref, wg_ref, wu_ref, wd_ref, o_ref, wd_bf16_ref):
    wd_bf16_ref[...] = wd_ref[...].astype(jnp.bfloat16)
    wg = wg_ref[...].astype(jnp.bfloat16)
    wu = wu_ref[...].astype(jnp.bfloat16)
    for r, n in _row_chunks(x_ref.shape[0]):
        x = x_ref[r:r + n, :]
        inv = _row_factor(inv_ref, r, n, wg.shape[1])
        gate = jnp.dot(x, wg, preferred_element_type=jnp.float32) * inv
        up = jnp.dot(x, wu, preferred_element_type=jnp.float32) * inv
        o_ref[r:r + n, :] = (jax.nn.silu(gate) * up).astype(o_ref.dtype)


def _gate_up(xg, row_inv, w_gate_up, w_down, layer):
    m, k = xg.shape
    tm, tn, panel_buffers = TILES["gate_up"]
    nt = D_FF // tn
    steps = (m // tm) * nt
    slab = D_FF // steps
    assert slab * steps == D_FF and slab % 16 == 0
    return pl.pallas_call(
        _gate_up_kernel,
        grid=(m // tm, nt),
        in_specs=[_panel_spec(tm, k, panel_buffers),
                  pl.BlockSpec((tm, LANES), lambda i, j: (i, 0)),
                  pl.BlockSpec((None, k, tn), lambda i, j: (layer, 0, j)),
                  pl.BlockSpec((None, k, tn), lambda i, j: (layer, 0, nt + j)),
                  pl.BlockSpec((None, slab, D_MODEL), lambda i, j: (layer, i * nt + j, 0))],
        out_specs=[pl.BlockSpec((tm, tn), lambda i, j: (i, j)),
                   pl.BlockSpec((slab, D_MODEL), lambda i, j: (i * nt + j, 0))],
        out_shape=[jax.ShapeDtypeStruct((m, D_FF), jnp.bfloat16),
                   jax.ShapeDtypeStruct((D_FF, D_MODEL), jnp.bfloat16)],
        compiler_params=_params(("arbitrary", "arbitrary"), VMEM_LIMIT_MATMUL),
        name="swiglu_gate_up",
    )(xg, row_inv, w_gate_up, w_gate_up, w_down)


def _rotary_tables(seq):
    pos = jnp.arange(seq, dtype=jnp.float32)
    inv_freq = 1.0 / jnp.power(jnp.float32(ROPE_THETA),
                               jnp.arange(0, ROT_DIM, 2, dtype=jnp.float32) / ROT_DIM)
    ang = pos[:, None] * inv_freq[None, :]
    cos, sin = jnp.cos(ang), jnp.sin(ang)
    ones = jnp.ones((seq, HEAD_DIM - ROT_DIM), jnp.float32)
    zeros = jnp.zeros((seq, HEAD_DIM - ROT_DIM), jnp.float32)
    c_full = jnp.concatenate([cos, cos, ones], axis=-1)
    s_full = jnp.concatenate([-sin, sin, zeros], axis=-1)
    scale = HEAD_DIM ** -0.5
    return c_full * scale, s_full * scale, c_full, s_full


def kernel(x, norm1_g, w_in, attn_sink, pool_w, pool_scale, w_branch_attn, w_branch_pool,
           w_out, norm2_g, w_gate_up, w_down, final_norm_g):
    batch, seq, d = x.shape
    m = batch * seq
    bf16 = jnp.bfloat16
    tables = _rotary_tables(seq)
    pool_scale3 = pool_scale.reshape(DEPTH, 1, POOL_WIDTH)

    xs = x.reshape(m, d)
    h = _rmsnorm(xs, norm1_g[0].reshape(1, d), bf16)
    row_inv = None
    for l in range(DEPTH):
        proj = _matmul(h, w_in, l, jnp.float32, "in_proj", row_inv=row_inv)
        o_attn = _attention(proj, attn_sink[l], tables, batch, seq)
        mixed = _pool(proj, pool_w, pool_scale3, l, batch, seq)
        merged = _merge(o_attn, mixed, w_branch_attn, w_branch_pool, proj, l)
        xs, xg, row_inv = _matmul(merged, w_out, l, jnp.float32, "out_proj", residual=xs,
                                  next_gain=norm2_g[l].reshape(1, d))
        act, w_down_b = _gate_up(xg, row_inv, w_gate_up, w_down, l)
        if l + 1 < DEPTH:
            xs, h, row_inv = _matmul(act, w_down_b[None], 0, jnp.float32, "ffn_down",
                                     residual=xs, next_gain=norm1_g[l + 1].reshape(1, d))
        else:
            out = _matmul_final_norm(act, w_down_b, xs, final_norm_g.reshape(1, d), "ffn_down")
    return out.reshape(batch, seq, d)
```
